```python
import jax, jax.numpy as jnp
from jax import lax
import numpy as np

D_MODEL = 1024
BATCH = 2
SEQ = 8192
DEPTH = 1

F_WIDTH = D_MODEL
F_GROUPS = 4
F_GROUP_DIM = F_WIDTH // F_GROUPS
M_WIDTH = 2 * D_MODEL
M_HEADS = 4
M_HEAD_DIM = M_WIDTH // M_HEADS
QKV_BLOCK = 4
CONV_K = 5
CHUNK = 128
N_BRANCHES = 2
EPS = 1e-6
IN_COLS = 2 * F_WIDTH + 3 * M_WIDTH + N_BRANCHES * D_MODEL

kernel_name = "hybrid_fnet_mlstm_gated_block"


def rmsnorm(x, w):
    xf = x.astype(jnp.float32)
    y = xf * lax.rsqrt(jnp.mean(xf * xf, axis=-1, keepdims=True) + EPS)
    return (y * w.astype(jnp.float32)).astype(x.dtype)


def fourier_mix(xf):
    b, s, _ = xf.shape
    xg = xf.astype(jnp.float32).reshape(b, s, F_GROUPS, F_GROUP_DIM)
    y = jnp.fft.fft2(xg, axes=(1, 3), norm="ortho").real
    return y.reshape(b, s, F_WIDTH).astype(xf.dtype)


def block_diag_proj(x, w):
    b, s, width = x.shape
    xb = x.reshape(b, s, width // QKV_BLOCK, QKV_BLOCK)
    return jnp.einsum('bsnc,ncd->bsnd', xb, w).reshape(b, s, width)


def centred_dwconv(x, w, bias):
    y = lax.conv_general_dilated(
        x, w[:, None, :], window_strides=(1,),
        padding=[(CONV_K // 2, CONV_K // 2)],
        dimension_numbers=('NWC', 'WIO', 'NWC'),
        feature_group_count=x.shape[-1])
    return y + bias


def mlstm_chunkwise(q, k, v, i_pre, log_f):
    b, h, s, d = q.shape
    nc = s // CHUNK

    def to_chunks(t):
        t = t.reshape(b, h, nc, CHUNK, *t.shape[3:])
        return jnp.moveaxis(t, 2, 0)

    xs = (to_chunks(q), to_chunks(k), to_chunks(v), to_chunks(i_pre), to_chunks(log_f))
    lower = jnp.tril(jnp.ones((CHUNK, CHUNK), dtype=bool))

    def step(carry, inp):
        c_st, n_st, m_st = carry
        qt, kt, vt, it, ft = inp
        bcum = jnp.cumsum(ft, axis=-1)
        dmat = bcum[..., :, None] - bcum[..., None, :] + it[..., None, :]
        dmat = jnp.where(lower, dmat, -jnp.inf)
        m_t = jnp.maximum(bcum + m_st[..., None], jnp.max(dmat, axis=-1))
        s_mat = jnp.einsum('bhtd,bhsd->bhts', qt, kt) * jnp.exp(dmat - m_t[..., None])
        inter = jnp.exp(bcum + m_st[..., None] - m_t)
        num = (jnp.einsum('bhts,bhsd->bhtd', s_mat, vt)
               + inter[..., None] * jnp.einsum('bhvk,bhtk->bhtv', c_st, qt))
        den = jnp.sum(s_mat, axis=-1) + inter * jnp.einsum('bhk,bhtk->bht', n_st, qt)
        h_out = num / jnp.maximum(jnp.abs(den), jnp.exp(-m_t))[..., None]
        g = bcum[..., -1]
        w_log = g[..., None] - bcum + it
        m_new = jnp.maximum(g + m_st, jnp.max(w_log, axis=-1))
        decay = jnp.exp(g + m_st - m_new)
        ws = jnp.exp(w_log - m_new[..., None])
        c_new = decay[..., None, None] * c_st + jnp.einsum('bhs,bhsv,bhsk->bhvk', ws, vt, kt)
        n_new = decay[..., None] * n_st + jnp.einsum('bhs,bhsk->bhk', ws, kt)
        return (c_new, n_new, m_new), h_out

    init = (jnp.zeros((b, h, d, d), jnp.float32),
            jnp.zeros((b, h, d), jnp.float32),
            jnp.zeros((b, h), jnp.float32))
    _, hs = lax.scan(step, init, xs)
    return jnp.moveaxis(hs, 0, 2).reshape(b, h, s, d)


def setup_inputs(seed: int = 0) -> dict:
    key = jax.random.key(seed)
    ks = jax.random.split(key, 24)

    def nrm(k, shape, scale):
        return jax.random.normal(k, shape, jnp.float32) * scale

    gate_in = 3 * M_WIDTH
    f_bias = jnp.broadcast_to(jnp.linspace(3.0, 6.0, M_HEADS, dtype=jnp.float32), (DEPTH, M_HEADS))
    nb = M_WIDTH // QKV_BLOCK
    return {
        "x": nrm(ks[0], (BATCH, SEQ, D_MODEL), 1.0),
        "norm_w": 1.0 + nrm(ks[1], (DEPTH, D_MODEL), 0.02),
        "w_in": nrm(ks[2], (DEPTH, D_MODEL, IN_COLS), D_MODEL ** -0.5),
        "w_fourier": nrm(ks[3], (DEPTH, F_WIDTH, D_MODEL), F_WIDTH ** -0.5),
        "conv_w": nrm(ks[4], (DEPTH, CONV_K, M_WIDTH), CONV_K ** -0.5),
        "conv_b": nrm(ks[5], (DEPTH, M_WIDTH), 0.02),
        "w_q": nrm(ks[6], (DEPTH, nb, QKV_BLOCK, QKV_BLOCK), QKV_BLOCK ** -0.5),
        "w_k": nrm(ks[7], (DEPTH, nb, QKV_BLOCK, QKV_BLOCK), QKV_BLOCK ** -0.5),
        "w_v": nrm(ks[8], (DEPTH, nb, QKV_BLOCK, QKV_BLOCK), QKV_BLOCK ** -0.5),
        "w_igate_fwd": nrm(ks[9], (DEPTH, gate_in, M_HEADS), 0.1 * gate_in ** -0.5),
        "b_igate_fwd": nrm(ks[10], (DEPTH, M_HEADS), 0.1),
        "w_fgate_fwd": nrm(ks[11], (DEPTH, gate_in, M_HEADS), 0.1 * gate_in ** -0.5),
        "b_fgate_fwd": f_bias + nrm(ks[12], (DEPTH, M_HEADS), 0.01),
        "w_igate_bwd": nrm(ks[13], (DEPTH, gate_in, M_HEADS), 0.1 * gate_in ** -0.5),
        "b_igate_bwd": nrm(ks[14], (DEPTH, M_HEADS), 0.1),
        "w_fgate_bwd": nrm(ks[15], (DEPTH, gate_in, M_HEADS), 0.1 * gate_in ** -0.5),
        "b_fgate_bwd": f_bias + nrm(ks[16], (DEPTH, M_HEADS), 0.01),
        "hnorm_w": 1.0 + nrm(ks[17], (DEPTH, M_WIDTH), 0.02),
        "skip_w": 1.0 + nrm(ks[18], (DEPTH, M_WIDTH), 0.02),
        "w_mlstm": nrm(ks[19], (DEPTH, M_WIDTH, D_MODEL), M_WIDTH ** -0.5),
        "w_out": nrm(ks[20], (DEPTH, D_MODEL, D_MODEL), D_MODEL ** -0.5),
        "final_norm_w": 1.0 + nrm(ks[21], (D_MODEL,), 0.02),
    }


def reference(x, norm_w, w_in, w_fourier, conv_w, conv_b, w_q, w_k, w_v,
              w_igate_fwd, b_igate_fwd, w_fgate_fwd, b_fgate_fwd,
              w_igate_bwd, b_igate_bwd, w_fgate_bwd, b_fgate_bwd,
              hnorm_w, skip_w, w_mlstm, w_out, final_norm_w):
    bsz, seq, _ = x.shape
    split_points = np.cumsum([F_WIDTH, F_WIDTH, M_WIDTH, M_WIDTH, M_WIDTH]).tolist()

    def to_heads(t):
        return jnp.transpose(t.reshape(bsz, seq, M_HEADS, M_HEAD_DIM), (0, 2, 1, 3)).astype(jnp.float32)

    for l in range(DEPTH):
        h = rmsnorm(x, norm_w[l])
        proj = h @ w_in[l]
        x_f, z_f, x_m, z_m, o_m, g_logits = jnp.split(proj, split_points, axis=-1)

        y_a = (fourier_mix(x_f) * jax.nn.silu(z_f)) @ w_fourier[l]

        x_c = jax.nn.silu(centred_dwconv(x_m, conv_w[l], conv_b[l]))
        q = block_diag_proj(x_c, w_q[l])
        k = block_diag_proj(x_c, w_k[l])
        v = block_diag_proj(x_m, w_v[l])
        qkv = jnp.concatenate([q, k, v], axis=-1)
        i_fwd = jnp.transpose(qkv @ w_igate_fwd[l] + b_igate_fwd[l], (0, 2, 1)).astype(jnp.float32)
        f_fwd = jax.nn.log_sigmoid(jnp.transpose(qkv @ w_fgate_fwd[l] + b_fgate_fwd[l], (0, 2, 1)).astype(jnp.float32))
        i_bwd = jnp.transpose(qkv @ w_igate_bwd[l] + b_igate_bwd[l], (0, 2, 1)).astype(jnp.float32)
        f_bwd = jax.nn.log_sigmoid(jnp.transpose(qkv @ w_fgate_bwd[l] + b_fgate_bwd[l], (0, 2, 1)).astype(jnp.float32))
        qh = to_heads(q) * (M_HEAD_DIM ** -0.5)
        kh = to_heads(k)
        vh = to_heads(v)
        h_fwd = mlstm_chunkwise(qh, kh, vh, i_fwd, f_fwd)
        h_bwd = jnp.flip(mlstm_chunkwise(jnp.flip(qh, 2), jnp.flip(kh, 2), jnp.flip(vh, 2),
                                         jnp.flip(i_bwd, 2), jnp.flip(f_bwd, 2)), 2)
        h_cell = jnp.transpose(h_fwd + h_bwd, (0, 2, 1, 3))
        h_cell = h_cell * jax.nn.sigmoid(o_m.astype(jnp.float32)).reshape(bsz, seq, M_HEADS, M_HEAD_DIM)
        h_cell = h_cell * lax.rsqrt(jnp.mean(h_cell * h_cell, axis=-1, keepdims=True) + EPS)
        h_cell = h_cell.reshape(bsz, seq, M_WIDTH).astype(x.dtype) * hnorm_w[l]
        y_b = ((h_cell + skip_w[l] * x_c) * jax.nn.silu(z_m)) @ w_mlstm[l]

        g_a, g_b = jnp.split(jax.nn.sigmoid(g_logits), N_BRANCHES, axis=-1)
        x = x + (g_a * y_a + g_b * y_b) @ w_out[l]

    return rmsnorm(x, final_norm_w)
```

```python
import functools

import numpy as np
import jax
import jax.numpy as jnp
from jax import lax
from jax.experimental import pallas as pl
from jax.experimental.pallas import tpu as pltpu

F32 = jnp.float32
BF16 = jnp.bfloat16

D_MODEL = 1024
F_WIDTH = D_MODEL
F_GROUPS = 4
F_GROUP_DIM = F_WIDTH // F_GROUPS
M_WIDTH = 2 * D_MODEL
M_HEADS = 4
M_HEAD_DIM = M_WIDTH // M_HEADS
QKV_BLOCK = 4
CONV_K = 5
CHUNK = 128
EPS = 1e-6
IN_COLS = 2 * F_WIDTH + 3 * M_WIDTH + 2 * D_MODEL

COL_XF = 0
COL_ZF = F_WIDTH
COL_XM = 2 * F_WIDTH
COL_ZM = COL_XM + M_WIDTH
COL_OM = COL_ZM + M_WIDTH
COL_G = COL_OM + M_WIDTH

LANES = 128
BF16_SUBLANES = 16
MXU_DIM = 256
GATE_LANES = 128
VMEM_LIMIT = 56 * 1024 * 1024

FFT_N2 = 32


def _dot(a, b):
    return jnp.dot(a, b, preferred_element_type=F32)


def _inproj_kernel(x_ref, nw_ref, w_ref, o_ref, h_ref):
    @pl.when(pl.program_id(1) == 0)
    def _():
        x = x_ref[...]
        ms = jnp.mean(x * x, axis=-1, keepdims=True)
        h_ref[...] = (x * lax.rsqrt(ms + EPS) * nw_ref[...]).astype(BF16)

    o_ref[...] = _dot(h_ref[...], w_ref[...]).astype(BF16)


def _inproj(x2, norm_w, w_in_bf, tm=512, tn=1024):
    t, d = x2.shape
    n = w_in_bf.shape[1]
    return pl.pallas_call(
        _inproj_kernel,
        grid=(t // tm, n // tn),
        in_specs=[
            pl.BlockSpec((tm, d), lambda i, j: (i, 0)),
            pl.BlockSpec((1, d), lambda i, j: (0, 0)),
            pl.BlockSpec((d, tn), lambda i, j: (0, j)),
        ],
        out_specs=pl.BlockSpec((tm, tn), lambda i, j: (i, j)),
        out_shape=jax.ShapeDtypeStruct((t, n), BF16),
        scratch_shapes=[pltpu.VMEM((tm, d), BF16)],
        compiler_params=pltpu.CompilerParams(
            dimension_semantics=("parallel", "arbitrary"),
            vmem_limit_bytes=VMEM_LIMIT),
        name="inproj",
    )(x2, norm_w.reshape(1, d), w_in_bf)


def _fft_constants(seq):
    n2 = FFT_N2
    n1 = seq // n2
    sub = BF16_SUBLANES
    nblk = n1 // sub
    i = np.arange(nblk)[:, None, None, None]
    k2 = np.arange(n2)[None, :, None, None]
    j = np.arange(sub)[None, None, :, None]
    m2 = np.arange(n2)[None, None, None, :]
    ang = 2.0 * np.pi * (((m2 * k2) % n2) / n2 + (((sub * i + j) * k2) % seq) / seq)
    val = np.stack([np.cos(ang), -np.sin(ang)], axis=1)
    wa = np.zeros((nblk, 2, n2, sub, n2, sub), np.float64)
    for jj in range(sub):
        wa[:, :, :, jj, :, jj] = val[:, :, :, jj, :]
    wa = wa.reshape(nblk, 2 * n2 * sub, n2 * sub)
    a = np.arange(n1)
    angb = 2.0 * np.pi * ((a[:, None] * a[None, :]) % n1) / n1
    wr, wi = np.cos(angb), -np.sin(angb)
    wb = np.block([[wr, -wi], [wi, wr]])
    c = np.arange(F_GROUP_DIM)
    angc = 2.0 * np.pi * ((c[:, None] * c[None, :]) % F_GROUP_DIM) / F_GROUP_DIM
    wc = np.concatenate([np.cos(angc), np.sin(angc)], axis=0) / np.sqrt(seq * F_GROUP_DIM)
    return wa, wb, wc


def _fft_a_kernel(x_ref, wa_ref, o_ref, *, nsub):
    n2 = FFT_N2
    sub = BF16_SUBLANES
    c = x_ref.shape[-1]
    for i in range(nsub):
        rows = slice(sub * i, sub * (i + 1))
        xs = x_ref[:, rows, :].reshape(n2 * sub, c)
        z = _dot(wa_ref[i], xs)
        o_ref[:, :, rows, :] = z.astype(BF16).reshape(2, n2, sub, c)


def _fft_b_kernel(z_ref, wb_ref, wc_ref, o_ref):
    n1 = z_ref.shape[1]
    c = z_ref.shape[2]
    xs = z_ref[...].reshape(2 * n1, c)
    g = _dot(wb_ref[...], xs).astype(BF16)
    for grp in range(c // F_GROUP_DIM):
        cols = slice(F_GROUP_DIM * grp, F_GROUP_DIM * (grp + 1))
        lhs = jnp.concatenate([g[:n1, cols], g[n1:, cols]], axis=1)
        o_ref[:, cols] = _dot(lhs, wc_ref[...]).astype(BF16)


def _fourier_mix(proj, bsz, seq, consts, nsub=2):
    wa, wb, wc = consts
    n2 = FFT_N2
    n1 = seq // n2
    rblk = nsub * BF16_SUBLANES
    proj4 = proj.reshape(bsz, n2, n1, IN_COLS)
    za = pl.pallas_call(
        functools.partial(_fft_a_kernel, nsub=nsub),
        grid=(bsz, n1 // rblk),
        in_specs=[
            pl.BlockSpec((None, n2, rblk, F_WIDTH), lambda b, i: (b, 0, i, COL_XF // F_WIDTH)),
            pl.BlockSpec((nsub,) + wa.shape[1:], lambda b, i: (i, 0, 0)),
        ],
        out_specs=pl.BlockSpec((None, 2, n2, rblk, F_WIDTH), lambda b, i: (b, 0, 0, i, 0)),
        out_shape=jax.ShapeDtypeStruct((bsz, 2, n2, n1, F_WIDTH), BF16),
        compiler_params=pltpu.CompilerParams(
            dimension_semantics=("parallel", "parallel"),
            vmem_limit_bytes=VMEM_LIMIT),
        name="fft_stage_a",
    )(proj4, wa)
    yt = pl.pallas_call(
        _fft_b_kernel,
        grid=(bsz, n2),
        in_specs=[
            pl.BlockSpec((None, 2, None, n1, F_WIDTH), lambda b, k: (b, 0, k, 0, 0)),
            pl.BlockSpec(wb.shape, lambda b, k: (0, 0)),
            pl.BlockSpec(wc.shape, lambda b, k: (0, 0)),
        ],
        out_specs=pl.BlockSpec((None, None, n1, F_WIDTH), lambda b, k: (b, k, 0, 0)),
        out_shape=jax.ShapeDtypeStruct((bsz, n2, n1, F_WIDTH), BF16),
        compiler_params=pltpu.CompilerParams(
            dimension_semantics=("parallel", "parallel"),
            vmem_limit_bytes=VMEM_LIMIT),
        name="fft_stage_b",
    )(za, wb, wc)
    return yt


def _log_sigmoid(x):
    return jnp.minimum(x, 0.0) - jnp.log1p(jnp.exp(-jnp.abs(x)))


def _conv_qkv_kernel(prev_ref, cur_ref, next_ref, cw_ref, cb_ref, wq_ref, wk_ref, wv_ref,
                     wg_ref, bg_ref, q_ref, k_ref, v_ref, xc_ref, g_ref, xs_ref,
                     *, tiles_per_seq):
    tm = cur_ref.shape[0]
    halo = BF16_SUBLANES
    pos = pl.program_id(0) % tiles_per_seq
    keep_prev = (pos != 0).astype(F32)
    keep_next = (pos != tiles_per_seq - 1).astype(F32)
    xs_ref[0:halo, :] = prev_ref[...].astype(F32) * keep_prev
    xs_ref[halo:halo + tm, :] = cur_ref[...].astype(F32)
    xs_ref[halo + tm:, :] = next_ref[...].astype(F32) * keep_next

    scale = M_HEAD_DIM ** -0.5
    gacc = jnp.zeros((tm, GATE_LANES), F32)
    for g in range(M_WIDTH // MXU_DIM):
        cols = slice(MXU_DIM * g, MXU_DIM * (g + 1))
        acc = jnp.broadcast_to(cb_ref[:, cols], (tm, MXU_DIM))
        for j in range(CONV_K):
            start = halo - CONV_K // 2 + j
            acc = acc + cw_ref[j:j + 1, cols] * xs_ref[start:start + tm, cols]
        xc = acc * jax.nn.sigmoid(acc)
        xcb = xc.astype(BF16)
        q = _dot(xcb, wq_ref[g])
        k = _dot(xcb, wk_ref[g])
        v = _dot(cur_ref[:, cols], wv_ref[g])
        qb, kb, vb = q.astype(BF16), k.astype(BF16), v.astype(BF16)
        gacc = gacc + _dot(qb, wg_ref[0, g]) + _dot(kb, wg_ref[1, g]) + _dot(vb, wg_ref[2, g])
        q_ref[:, cols] = (q * scale).astype(BF16)
        k_ref[:, cols] = kb
        v_ref[:, cols] = vb
        xc_ref[:, cols] = xcb
    gates = gacc + bg_ref[...]
    lane = lax.broadcasted_iota(jnp.int32, gates.shape, 1)
    is_forget = (lane // M_HEADS) % 2 == 1
    g_ref[...] = jnp.where(is_forget, _log_sigmoid(gates), gates)


def _conv_qkv(proj, seq, cw, cb, wq, wk, wv, wg, bg, tm=256):
    t = proj.shape[0]
    halo = BF16_SUBLANES
    hb = tm // halo
    nhalo = t // halo
    xm_blk = COL_XM // M_WIDTH
    out_bf = jax.ShapeDtypeStruct((t, M_WIDTH), BF16)
    row_spec = pl.BlockSpec((tm, M_WIDTH), lambda i: (i, 0))

    def full(a):
        nd = a.ndim
        return pl.BlockSpec(a.shape, lambda i: (0,) * nd)

    return pl.pallas_call(
        functools.partial(_conv_qkv_kernel, tiles_per_seq=seq // tm),
        grid=(t // tm,),
        in_specs=[
            pl.BlockSpec((halo, M_WIDTH), lambda i: (jnp.maximum(i * hb - 1, 0), xm_blk)),
            pl.BlockSpec((tm, M_WIDTH), lambda i: (i, xm_blk)),
            pl.BlockSpec((halo, M_WIDTH), lambda i: (jnp.minimum((i + 1) * hb, nhalo - 1), xm_blk)),
            full(cw), full(cb), full(wq), full(wk), full(wv), full(wg), full(bg),
        ],
        out_specs=[row_spec, row_spec, row_spec, row_spec,
                   pl.BlockSpec((tm, GATE_LANES), lambda i: (i, 0))],
        out_shape=[out_bf, out_bf, out_bf, out_bf,
                   jax.ShapeDtypeStruct((t, GATE_LANES), F32)],
        scratch_shapes=[pltpu.VMEM((tm + 2 * halo, M_WIDTH), F32)],
        compiler_params=pltpu.CompilerParams(
            dimension_semantics=("parallel",),
            vmem_limit_bytes=VMEM_LIMIT),
        name="conv_qkv_gates",
    )(proj, proj, proj, cw, cb, wq, wk, wv, wg, bg)


def _mlstm_chunk(q_ref, k_ref, v_ref, g_ref, h_ref, ct_ref, n_ref, m_ref, *, reverse, irow, frow):
    L = q_ref.shape[0]
    gr = g_ref[...]
    lane8 = lax.broadcasted_iota(jnp.int32, gr.shape, 1)
    cs = gr
    sh = 1
    while sh < L:
        if reverse:
            cs = cs + jnp.where(lane8 < L - sh, pltpu.roll(cs, L - sh, axis=1), 0.0)
        else:
            cs = cs + jnp.where(lane8 >= sh, pltpu.roll(cs, sh, axis=1), 0.0)
        sh *= 2
    i_row = gr[irow:irow + 1, :]
    b_row = cs[frow:frow + 1, :]
    sq = jnp.concatenate([i_row, b_row, jnp.zeros((L - 2, L), F32)], axis=0)
    tr = sq.T
    i_col = tr[:, 0:1]
    b_col = tr[:, 1:2]
    m_prev = m_ref[:, 0:1]

    r = lax.broadcasted_iota(jnp.int32, (L, L), 0)
    c = lax.broadcasted_iota(jnp.int32, (L, L), 1)
    mask = (c >= r) if reverse else (c <= r)
    dmat = jnp.where(mask, b_col - b_row + i_row, -jnp.inf)
    m_t = jnp.maximum(b_col + m_prev, jnp.max(dmat, axis=1, keepdims=True))
    p = jnp.exp(dmat - m_t)

    q = q_ref[...]
    k = k_ref[...]
    v = v_ref[...]
    s = lax.dot_general(q, k, (((1,), (1,)), ((), ())), preferred_element_type=F32) * p
    inter = jnp.exp(b_col + m_prev - m_t)
    num = _dot(s.astype(BF16), v) + inter * _dot(q, ct_ref[...].astype(BF16))
    qn = jnp.sum(q.astype(F32) * n_ref[...], axis=1, keepdims=True)
    den = jnp.sum(s, axis=1, keepdims=True) + inter * qn
    h = num / jnp.maximum(jnp.abs(den), jnp.exp(-m_t))
    h_ref[...] = h.astype(h_ref.dtype)

    g = b_row[:, 0:1] if reverse else b_row[:, L - 1:L]
    m_new = jnp.maximum(g + m_prev, jnp.max(g - b_row + i_row, axis=1, keepdims=True))
    decay = jnp.exp(g + m_prev - m_new)
    ws_col = jnp.exp(g - b_col + i_col - m_new)
    vw = (ws_col * v.astype(F32)).astype(BF16)
    upd = lax.dot_general(k, vw, (((0,), (0,)), ((), ())), preferred_element_type=F32)
    ct_ref[...] = decay * ct_ref[...] + upd
    n_ref[...] = decay * n_ref[...] + jnp.sum(ws_col * k.astype(F32), axis=0, keepdims=True)
    m_ref[...] = jnp.broadcast_to(m_new, m_ref.shape)


def _mlstm_kernel(qf_ref, kf_ref, vf_ref, gf_ref, qb_ref, kb_ref, vb_ref, gb_ref,
                  hf_ref, hb_ref, ct_ref, n_ref, m_ref):
    @pl.when(pl.program_id(2) == 0)
    def _():
        ct_ref[...] = jnp.zeros_like(ct_ref)
        n_ref[...] = jnp.zeros_like(n_ref)
        m_ref[...] = jnp.zeros_like(m_ref)

    _mlstm_chunk(qf_ref, kf_ref, vf_ref, gf_ref, hf_ref, ct_ref.at[0], n_ref.at[0], m_ref.at[0],
                 reverse=False, irow=0, frow=1)
    _mlstm_chunk(qb_ref, kb_ref, vb_ref, gb_ref, hb_ref, ct_ref.at[1], n_ref.at[1], m_ref.at[1],
                 reverse=True, irow=2, frow=3)


def _mlstm(q, k, v, grows, bsz, seq):
    t = q.shape[0]
    nc = seq // CHUNK
    dh = M_HEAD_DIM

    def fwd_rows(b, h, c):
        return (b * nc + c, h)

    def bwd_rows(b, h, c):
        return (b * nc + nc - 1 - c, h)

    qkv_f = pl.BlockSpec((CHUNK, dh), fwd_rows)
    qkv_b = pl.BlockSpec((CHUNK, dh), bwd_rows)
    g_f = pl.BlockSpec((None, None, None, 8, CHUNK), lambda b, h, c: (b, h, c, 0, 0))
    g_b = pl.BlockSpec((None, None, None, 8, CHUNK), lambda b, h, c: (b, h, nc - 1 - c, 0, 0))
    out = jax.ShapeDtypeStruct((t, M_WIDTH), BF16)
    return pl.pallas_call(
        _mlstm_kernel,
        grid=(bsz, M_HEADS, nc),
        in_specs=[qkv_f, qkv_f, qkv_f, g_f, qkv_b, qkv_b, qkv_b, g_b],
        out_specs=[qkv_f, qkv_b],
        out_shape=[out, out],
        scratch_shapes=[
            pltpu.VMEM((2, dh, dh), F32),
            pltpu.VMEM((2, 1, dh), F32),
            pltpu.VMEM((2, 1, LANES), F32),
        ],
        compiler_params=pltpu.CompilerParams(
            dimension_semantics=("parallel", "parallel", "arbitrary"),
            vmem_limit_bytes=VMEM_LIMIT),
        name="mlstm",
    )(q, k, v, grows, q, k, v, grows)


def _silu(x):
    return x * jax.nn.sigmoid(x)


def _merge_kernel(hf_ref, hb_ref, om_ref, zm_ref, xc_ref, yf_ref, zf_ref, gl_ref, x_ref,
                  hn_ref, sk_ref, wf_ref, wm_ref, wo_ref, fn_ref, o_ref, *, final_norm):
    y_b = None
    for h in range(M_HEADS):
        cols = slice(M_HEAD_DIM * h, M_HEAD_DIM * (h + 1))
        hc = (hf_ref[:, cols].astype(F32) + hb_ref[:, cols].astype(F32))
        hc = hc * jax.nn.sigmoid(om_ref[:, cols].astype(F32))
        hc = hc * lax.rsqrt(jnp.mean(hc * hc, axis=-1, keepdims=True) + EPS)
        u = hc * hn_ref[:, cols] + sk_ref[:, cols] * xc_ref[:, cols].astype(F32)
        u = u * _silu(zm_ref[:, cols].astype(F32))
        part = _dot(u.astype(BF16), wm_ref[cols, :])
        y_b = part if y_b is None else y_b + part
    ya_in = yf_ref[...].astype(F32) * _silu(zf_ref[...].astype(F32))
    y_a = _dot(ya_in.astype(BF16), wf_ref[...])
    g_a = jax.nn.sigmoid(gl_ref[:, :D_MODEL].astype(F32))
    g_b = jax.nn.sigmoid(gl_ref[:, D_MODEL:].astype(F32))
    merged = g_a * y_a + g_b * y_b
    xn = x_ref[...] + _dot(merged.astype(BF16), wo_ref[...])
    if final_norm:
        xn = xn * lax.rsqrt(jnp.mean(xn * xn, axis=-1, keepdims=True) + EPS) * fn_ref[...]
    o_ref[...] = xn


def _merge(hf, hb, proj, xc, yf, x2, hn, sk, wf, wm, wo, fn, final_norm, tm=256):
    t = x2.shape[0]

    def rows(width, blk):
        return pl.BlockSpec((tm, width), lambda i: (i, blk))

    def full(a):
        nd = a.ndim
        return pl.BlockSpec(a.shape, lambda i: (0,) * nd)

    return pl.pallas_call(
        functools.partial(_merge_kernel, final_norm=final_norm),
        grid=(t // tm,),
        in_specs=[
            rows(M_WIDTH, 0), rows(M_WIDTH, 0),
            rows(M_WIDTH, COL_OM // M_WIDTH), rows(M_WIDTH, COL_ZM // M_WIDTH),
            rows(M_WIDTH, 0),
            rows(F_WIDTH, 0), rows(F_WIDTH, COL_ZF // F_WIDTH),
            rows(2 * D_MODEL, COL_G // (2 * D_MODEL)),
            rows(D_MODEL, 0),
            full(hn), full(sk), full(wf), full(wm), full(wo), full(fn),
        ],
        out_specs=rows(D_MODEL, 0),
        out_shape=jax.ShapeDtypeStruct((t, D_MODEL), F32),
        compiler_params=pltpu.CompilerParams(
            dimension_semantics=("parallel",),
            vmem_limit_bytes=VMEM_LIMIT),
        name="merge_out",
    )(hf, hb, proj, proj, xc, yf, proj, proj, x2, hn, sk, wf, wm, wo, fn)


def _block_diag_tiles(w):
    per = MXU_DIM // QKV_BLOCK
    w4 = w.reshape(-1, per, QKV_BLOCK, QKV_BLOCK)
    eye = jnp.eye(per, dtype=w.dtype)
    dense = jnp.einsum('gacd,ab->gacbd', w4, eye)
    return dense.reshape(-1, MXU_DIM, MXU_DIM).astype(BF16)


def kernel(x, norm_w, w_in, w_fourier, conv_w, conv_b, w_q, w_k, w_v,
           w_igate_fwd, b_igate_fwd, w_fgate_fwd, b_fgate_fwd,
           w_igate_bwd, b_igate_bwd, w_fgate_bwd, b_fgate_bwd,
           hnorm_w, skip_w, w_mlstm, w_out, final_norm_w):
    bsz, seq, d = x.shape
    depth = w_in.shape[0]
    assert d == D_MODEL and seq % (FFT_N2 * BF16_SUBLANES * 2) == 0 and seq % CHUNK == 0
    t = bsz * seq
    nc = seq // CHUNK
    consts = tuple(jnp.asarray(a, dtype=F32).astype(BF16) for a in _fft_constants(seq))
    fn = final_norm_w.reshape(1, d)

    x2 = x.reshape(t, d)
    for l in range(depth):
        proj = _inproj(x2, norm_w[l], w_in[l].astype(BF16))

        yt = _fourier_mix(proj, bsz, seq, consts)
        yf = jnp.transpose(yt, (0, 2, 1, 3)).reshape(t, F_WIDTH)

        wg = jnp.concatenate([w_igate_fwd[l], w_fgate_fwd[l], w_igate_bwd[l], w_fgate_bwd[l]], axis=1)
        wg = jnp.pad(wg, ((0, 0), (0, GATE_LANES - wg.shape[1])))
        wg = wg.reshape(3, M_WIDTH // MXU_DIM, MXU_DIM, GATE_LANES).astype(BF16)
        bg = jnp.concatenate([b_igate_fwd[l], b_fgate_fwd[l], b_igate_bwd[l], b_fgate_bwd[l]])
        bg = jnp.pad(bg, (0, GATE_LANES - bg.shape[0])).reshape(1, GATE_LANES)
        q, k, v, xc, gates = _conv_qkv(
            proj, seq, conv_w[l], conv_b[l].reshape(1, M_WIDTH),
            _block_diag_tiles(w_q[l]), _block_diag_tiles(w_k[l]), _block_diag_tiles(w_v[l]),
            wg, bg)

        gr = gates[:, :4 * M_HEADS].reshape(bsz, nc, CHUNK, 4, M_HEADS)
        gr = jnp.transpose(gr, (0, 4, 1, 3, 2))
        gr = jnp.pad(gr, ((0, 0), (0, 0), (0, 0), (0, 4), (0, 0)))
        hf, hb = _mlstm(q, k, v, gr, bsz, seq)

        x2 = _merge(hf, hb, proj, xc, yf, x2,
                    hnorm_w[l].reshape(1, M_WIDTH), skip_w[l].reshape(1, M_WIDTH),
                    w_fourier[l].astype(BF16), w_mlstm[l].astype(BF16), w_out[l].astype(BF16),
                    fn, final_norm=(l == depth - 1))
    return x2.reshape(bsz, seq, d)
```

```python
import functools

import numpy as np
import jax
import jax.numpy as jnp
from jax import lax
from jax.experimental import pallas as pl
from jax.experimental.pallas import tpu as pltpu

F32 = jnp.float32
BF16 = jnp.bfloat16

D_MODEL = 1024
F_WIDTH = D_MODEL
F_GROUPS = 4
F_GROUP_DIM = F_WIDTH // F_GROUPS
M_WIDTH = 2 * D_MODEL
M_HEADS = 4
M_HEAD_DIM = M_WIDTH // M_HEADS
QKV_BLOCK = 4
CONV_K = 5
EPS = 1e-6
IN_COLS = 2 * F_WIDTH + 3 * M_WIDTH + 2 * D_MODEL

MCHUNK = 256
MLSTM_HEADS_PER_STEP = 2

COL_XF = 0
COL_ZF = F_WIDTH
COL_XM = 2 * F_WIDTH
COL_ZM = COL_XM + M_WIDTH
COL_OM = COL_ZM + M_WIDTH
COL_G = COL_OM + M_WIDTH

LANES = 128
SUBLANES = 8
BF16_SUBLANES = 16
MXU_DIM = 256
GATE_LANES = 128
VMEM_LIMIT = 56 * 1024 * 1024

FFT_N2 = 32


def _dot(a, b):
    return jnp.dot(a, b, preferred_element_type=F32)


def _inproj_kernel(x_ref, nw_ref, w_ref, o_ref, *, tn):
    x = x_ref[...]
    ms = jnp.mean(x * x, axis=-1, keepdims=True)
    h = (x * lax.rsqrt(ms + EPS) * nw_ref[...]).astype(BF16)
    for j in range(w_ref.shape[1] // tn):
        cols = slice(tn * j, tn * (j + 1))
        o_ref[:, cols] = _dot(h, w_ref[:, cols]).astype(BF16)


def _inproj(x2, norm_w, w_in_bf, tm=512, tn=1024):
    t, d = x2.shape
    n = w_in_bf.shape[1]
    return pl.pallas_call(
        functools.partial(_inproj_kernel, tn=tn),
        grid=(t // tm,),
        in_specs=[
            pl.BlockSpec((tm, d), lambda i: (i, 0)),
            pl.BlockSpec((1, d), lambda i: (0, 0)),
            pl.BlockSpec((d, n), lambda i: (0, 0), pipeline_mode=pl.Buffered(1)),
        ],
        out_specs=pl.BlockSpec((tm, n), lambda i: (i, 0)),
        out_shape=jax.ShapeDtypeStruct((t, n), BF16),
        compiler_params=pltpu.CompilerParams(
            dimension_semantics=("parallel",),
            vmem_limit_bytes=VMEM_LIMIT),
        name="inproj",
    )(x2, norm_w.reshape(1, d), w_in_bf)


def _fft_constants(seq):
    n2 = FFT_N2
    n1 = seq // n2
    sub = BF16_SUBLANES
    nblk = n1 // sub
    i = np.arange(nblk)[:, None, None, None]
    k2 = np.arange(n2)[None, :, None, None]
    j = np.arange(sub)[None, None, :, None]
    m2 = np.arange(n2)[None, None, None, :]
    ang = 2.0 * np.pi * (((m2 * k2) % n2) / n2 + (((sub * i + j) * k2) % seq) / seq)
    val = np.stack([np.cos(ang), -np.sin(ang)], axis=1)
    wa = np.zeros((nblk, 2, n2, sub, n2, sub), np.float64)
    for jj in range(sub):
        wa[:, :, :, jj, :, jj] = val[:, :, :, jj, :]
    wa = wa.reshape(nblk, 2 * n2 * sub, n2 * sub)
    a = np.arange(n1)
    angb = 2.0 * np.pi * ((a[:, None] * a[None, :]) % n1) / n1
    wr, wi = np.cos(angb), -np.sin(angb)
    wb = np.block([[wr, -wi], [wi, wr]])
    c = np.arange(F_GROUP_DIM)
    angc = 2.0 * np.pi * ((c[:, None] * c[None, :]) % F_GROUP_DIM) / F_GROUP_DIM
    wc = np.concatenate([np.cos(angc), np.sin(angc)], axis=0) / np.sqrt(seq * F_GROUP_DIM)
    return wa, wb, wc


def _fft_a_kernel(x_ref, wa_ref, o_ref, *, nsub):
    n2 = FFT_N2
    sub = BF16_SUBLANES
    c = x_ref.shape[-1]
    for i in range(nsub):
        rows = slice(sub * i, sub * (i + 1))
        xs = x_ref[:, rows, :].reshape(n2 * sub, c)
        z = _dot(wa_ref[i], xs)
        o_ref[:, :, rows, :] = z.astype(BF16).reshape(2, n2, sub, c)


def _fft_b_kernel(z_ref, wb_ref, wc_ref, o_ref):
    n1 = z_ref.shape[1]
    c = z_ref.shape[2]
    xs = z_ref[...].reshape(2 * n1, c)
    g = _dot(wb_ref[...], xs).astype(BF16)
    for grp in range(c // F_GROUP_DIM):
        cols = slice(F_GROUP_DIM * grp, F_GROUP_DIM * (grp + 1))
        lhs = jnp.concatenate([g[:n1, cols], g[n1:, cols]], axis=1)
        o_ref[:, cols] = _dot(lhs, wc_ref[...]).astype(BF16)


def _fourier_mix(proj, bsz, seq, consts, nsub=2):
    wa, wb, wc = consts
    n2 = FFT_N2
    n1 = seq // n2
    rblk = nsub * BF16_SUBLANES
    proj4 = proj.reshape(bsz, n2, n1, IN_COLS)
    za = pl.pallas_call(
        functools.partial(_fft_a_kernel, nsub=nsub),
        grid=(bsz, n1 // rblk),
        in_specs=[
            pl.BlockSpec((None, n2, rblk, F_WIDTH), lambda b, i: (b, 0, i, COL_XF // F_WIDTH)),
            pl.BlockSpec((nsub,) + wa.shape[1:], lambda b, i: (i, 0, 0)),
        ],
        out_specs=pl.BlockSpec((None, 2, n2, rblk, F_WIDTH), lambda b, i: (b, 0, 0, i, 0)),
        out_shape=jax.ShapeDtypeStruct((bsz, 2, n2, n1, F_WIDTH), BF16),
        compiler_params=pltpu.CompilerParams(
            dimension_semantics=("parallel", "parallel"),
            vmem_limit_bytes=VMEM_LIMIT),
        name="fft_stage_a",
    )(proj4, wa)
    yt = pl.pallas_call(
        _fft_b_kernel,
        grid=(bsz, n2),
        in_specs=[
            pl.BlockSpec((None, 2, None, n1, F_WIDTH), lambda b, k: (b, 0, k, 0, 0)),
            pl.BlockSpec(wb.shape, lambda b, k: (0, 0)),
            pl.BlockSpec(wc.shape, lambda b, k: (0, 0)),
        ],
        out_specs=pl.BlockSpec((None, None, n1, F_WIDTH), lambda b, k: (b, k, 0, 0)),
        out_shape=jax.ShapeDtypeStruct((bsz, n2, n1, F_WIDTH), BF16),
        compiler_params=pltpu.CompilerParams(
            dimension_semantics=("parallel", "parallel"),
            vmem_limit_bytes=VMEM_LIMIT),
        name="fft_stage_b",
    )(za, wb, wc)
    return yt


def _log_sigmoid(x):
    return jnp.minimum(x, 0.0) - jnp.log1p(jnp.exp(-jnp.abs(x)))


def _conv_qkv_kernel(prev_ref, cur_ref, next_ref, cw_ref, cb_ref, wq_ref, wk_ref, wv_ref,
                     wg_ref, bg_ref, q_ref, k_ref, v_ref, xc_ref, g_ref, xs_ref,
                     *, tiles_per_seq):
    tm = cur_ref.shape[0]
    halo = BF16_SUBLANES
    pos = pl.program_id(0) % tiles_per_seq
    keep_prev = (pos != 0).astype(F32)
    keep_next = (pos != tiles_per_seq - 1).astype(F32)
    xs_ref[0:halo, :] = prev_ref[...].astype(F32) * keep_prev
    xs_ref[halo:halo + tm, :] = cur_ref[...].astype(F32)
    xs_ref[halo + tm:, :] = next_ref[...].astype(F32) * keep_next

    scale = M_HEAD_DIM ** -0.5
    gacc = jnp.zeros((tm, GATE_LANES), F32)
    for g in range(M_WIDTH // MXU_DIM):
        cols = slice(MXU_DIM * g, MXU_DIM * (g + 1))
        acc = jnp.broadcast_to(cb_ref[:, cols], (tm, MXU_DIM))
        for j in range(CONV_K):
            start = halo - CONV_K // 2 + j
            acc = acc + cw_ref[j:j + 1, cols] * xs_ref[start:start + tm, cols]
        xc = acc * jax.nn.sigmoid(acc)
        xcb = xc.astype(BF16)
        q = _dot(xcb, wq_ref[g])
        k = _dot(xcb, wk_ref[g])
        v = _dot(cur_ref[:, cols], wv_ref[g])
        qb, kb, vb = q.astype(BF16), k.astype(BF16), v.astype(BF16)
        gacc = gacc + _dot(qb, wg_ref[0, g]) + _dot(kb, wg_ref[1, g]) + _dot(vb, wg_ref[2, g])
        q_ref[:, cols] = (q * scale).astype(BF16)
        k_ref[:, cols] = kb
        v_ref[:, cols] = vb
        xc_ref[:, cols] = xcb
    gates = gacc + bg_ref[...]
    lane = lax.broadcasted_iota(jnp.int32, gates.shape, 1)
    is_forget = (lane // M_HEADS) % 2 == 1
    g_ref[...] = jnp.where(is_forget, _log_sigmoid(gates), gates)


def _conv_qkv(proj, seq, cw, cb, wq, wk, wv, wg, bg, tm=256):
    t = proj.shape[0]
    halo = BF16_SUBLANES
    hb = tm // halo
    nhalo = t // halo
    xm_blk = COL_XM // M_WIDTH
    out_bf = jax.ShapeDtypeStruct((t, M_WIDTH), BF16)
    row_spec = pl.BlockSpec((tm, M_WIDTH), lambda i: (i, 0))

    def full(a):
        nd = a.ndim
        return pl.BlockSpec(a.shape, lambda i: (0,) * nd)

    return pl.pallas_call(
        functools.partial(_conv_qkv_kernel, tiles_per_seq=seq // tm),
        grid=(t // tm,),
        in_specs=[
            pl.BlockSpec((halo, M_WIDTH), lambda i: (jnp.maximum(i * hb - 1, 0), xm_blk)),
            pl.BlockSpec((tm, M_WIDTH), lambda i: (i, xm_blk)),
            pl.BlockSpec((halo, M_WIDTH), lambda i: (jnp.minimum((i + 1) * hb, nhalo - 1), xm_blk)),
            full(cw), full(cb), full(wq), full(wk), full(wv), full(wg), full(bg),
        ],
        out_specs=[row_spec, row_spec, row_spec, row_spec,
                   pl.BlockSpec((tm, GATE_LANES), lambda i: (i, 0))],
        out_shape=[out_bf, out_bf, out_bf, out_bf,
                   jax.ShapeDtypeStruct((t, GATE_LANES), F32)],
        scratch_shapes=[pltpu.VMEM((tm + 2 * halo, M_WIDTH), F32)],
        compiler_params=pltpu.CompilerParams(
            dimension_semantics=("parallel",),
            vmem_limit_bytes=VMEM_LIMIT),
        name="conv_qkv_gates",
    )(proj, proj, proj, cw, cb, wq, wk, wv, wg, bg)


GS_C, GS_M, GS_INTER, GS_ENEGM, GS_WS, GS_DECAY, GS_ROWS = 0, 1, 2, 3, 4, 5, 6


def _lane_scan(x, op, fill, reverse):
    n = x.shape[1]
    lane = lax.broadcasted_iota(jnp.int32, x.shape, 1)
    sh = 1
    while sh < n:
        if reverse:
            x = op(x, jnp.where(lane < n - sh, pltpu.roll(x, n - sh, axis=1), fill))
        else:
            x = op(x, jnp.where(lane >= sh, pltpu.roll(x, sh, axis=1), fill))
        sh *= 2
    return x


def _gate_prep(i_pre, log_f, gs_ref, tmp_ref, *, reverse):
    nc, L = i_pre.shape
    b = _lane_scan(log_f, jnp.add, 0.0, reverse)
    c = i_pre - b
    cm = _lane_scan(c, jnp.maximum, -jnp.inf, reverse)
    end = slice(0, 1) if reverse else slice(L - 1, L)
    g = b[:, end]
    cmt = cm[:, end]
    tmp_ref[0] = jnp.broadcast_to(g, (nc, LANES))
    tmp_ref[1] = jnp.broadcast_to(g + cmt, (nc, LANES))

    def body(step, m):
        idx = nc - 1 - step if reverse else step
        tmp_ref[2, pl.ds(idx, 1), :] = m
        return jnp.maximum(m + tmp_ref[0, pl.ds(idx, 1), :], tmp_ref[1, pl.ds(idx, 1), :])

    lax.fori_loop(0, nc, body, jnp.zeros((1, LANES), F32))
    m_prev = tmp_ref[2][:, 0:1]
    big_m = jnp.maximum(m_prev, cm)
    mx = jnp.maximum(m_prev, cmt)
    gs_ref[GS_C] = c
    gs_ref[GS_M] = big_m
    gs_ref[GS_INTER] = jnp.exp(m_prev - big_m)
    gs_ref[GS_ENEGM] = jnp.exp(-b - big_m)
    gs_ref[GS_WS] = jnp.exp(c - mx)
    gs_ref[GS_DECAY] = jnp.broadcast_to(jnp.exp(m_prev - mx), (nc, L))


def _mlstm_chunk(q, k, v, gs_ref, ci, ct_ref, n_ref, *, reverse):
    L = q.shape[0]

    def row(r):
        return gs_ref[r, pl.ds(ci, 1), :]

    c_row = row(GS_C)
    sub = lax.broadcasted_iota(jnp.int32, (SUBLANES, L), 0)
    packed = jnp.where(sub == 0, row(GS_M),
                       jnp.where(sub == 1, row(GS_INTER),
                                 jnp.where(sub == 2, row(GS_ENEGM), row(GS_WS))))
    tr = jnp.concatenate([packed, jnp.zeros((LANES - SUBLANES, L), F32)], axis=0).T
    m_col, inter, enegm, ws = tr[:, 0:1], tr[:, 1:2], tr[:, 2:3], tr[:, 3:4]
    decay = row(GS_DECAY)[:, 0:1]

    r_i = lax.broadcasted_iota(jnp.int32, (L, L), 0)
    c_i = lax.broadcasted_iota(jnp.int32, (L, L), 1)
    mask = (c_i >= r_i) if reverse else (c_i <= r_i)
    p = jnp.exp(jnp.where(mask, c_row - m_col, -jnp.inf))
    s = lax.dot_general(q, k, (((1,), (1,)), ((), ())), preferred_element_type=F32) * p
    num = _dot(s.astype(BF16), v) + inter * _dot(q, ct_ref[...].astype(BF16))
    qn = jnp.sum(q.astype(F32) * n_ref[...], axis=1, keepdims=True)
    den = jnp.sum(s, axis=1, keepdims=True) + inter * qn
    h = num / jnp.maximum(jnp.abs(den), enegm)

    vw = (ws * v.astype(F32)).astype(BF16)
    upd = lax.dot_general(k, vw, (((0,), (0,)), ((), ())), preferred_element_type=F32)
    ct_ref[...] = decay * ct_ref[...] + upd
    n_ref[...] = decay * n_ref[...] + jnp.sum(ws * k.astype(F32), axis=0, keepdims=True)
    return h


def _mlstm_kernel(gates_ref, qf_ref, kf_ref, vf_ref, qb_ref, kb_ref, vb_ref, hf_ref, hb_ref,
                  ct_ref, n_ref, gs_ref, tmp_ref, *, hp, nc):
    c = pl.program_id(2)
    dh = M_HEAD_DIM

    @pl.when(c == 0)
    def _():
        ct_ref[...] = jnp.zeros_like(ct_ref)
        n_ref[...] = jnp.zeros_like(n_ref)
        for hh in range(hp):
            _gate_prep(gates_ref[hh, 0], gates_ref[hh, 1], gs_ref.at[hh, 0], tmp_ref, reverse=False)
            _gate_prep(gates_ref[hh, 2], gates_ref[hh, 3], gs_ref.at[hh, 1], tmp_ref, reverse=True)

    for hh in range(hp):
        cols = slice(dh * hh, dh * (hh + 1))
        hf = _mlstm_chunk(qf_ref[:, cols], kf_ref[:, cols], vf_ref[:, cols], gs_ref.at[hh, 0], c,
                          ct_ref.at[hh, 0], n_ref.at[hh, 0], reverse=False)
        hf_ref[:, cols] = hf.astype(hf_ref.dtype)
        hb = _mlstm_chunk(qb_ref[:, cols], kb_ref[:, cols], vb_ref[:, cols], gs_ref.at[hh, 1],
                          nc - 1 - c, ct_ref.at[hh, 1], n_ref.at[hh, 1], reverse=True)
        hb_ref[:, cols] = hb.astype(hb_ref.dtype)


def _mlstm(q, k, v, gates4, bsz, seq, hp=MLSTM_HEADS_PER_STEP):
    t = q.shape[0]
    L = MCHUNK
    nc = seq // L
    dh = M_HEAD_DIM
    width = hp * dh

    qkv_f = pl.BlockSpec((L, width), lambda b, h, c: (b * nc + c, h))
    qkv_b = pl.BlockSpec((L, width), lambda b, h, c: (b * nc + nc - 1 - c, h))
    g_spec = pl.BlockSpec((None, hp, 4, nc, L), lambda b, h, c: (b, h, 0, 0, 0))
    out = jax.ShapeDtypeStruct((t, M_WIDTH), BF16)
    return pl.pallas_call(
        functools.partial(_mlstm_kernel, hp=hp, nc=nc),
        grid=(bsz, M_HEADS // hp, nc),
        in_specs=[g_spec, qkv_f, qkv_f, qkv_f, qkv_b, qkv_b, qkv_b],
        out_specs=[qkv_f, qkv_b],
        out_shape=[out, out],
        scratch_shapes=[
            pltpu.VMEM((hp, 2, dh, dh), F32),
            pltpu.VMEM((hp, 2, 1, dh), F32),
            pltpu.VMEM((hp, 2, GS_ROWS, nc, L), F32),
            pltpu.VMEM((3, nc, LANES), F32),
        ],
        compiler_params=pltpu.CompilerParams(
            dimension_semantics=("parallel", "parallel", "arbitrary"),
            vmem_limit_bytes=VMEM_LIMIT),
        name="mlstm",
    )(gates4, q, k, v, q, k, v)


def _silu(x):
    return x * jax.nn.sigmoid(x)


def _merge_kernel(hf_ref, hb_ref, om_ref, zm_ref, xc_ref, yf_ref, zf_ref, gl_ref, x_ref,
                  hn_ref, sk_ref, wf_ref, wm_ref, wo_ref, fn_ref, o_ref, *, final_norm):
    y_b = None
    for h in range(M_HEADS):
        cols = slice(M_HEAD_DIM * h, M_HEAD_DIM * (h + 1))
        hc = (hf_ref[:, cols].astype(F32) + hb_ref[:, cols].astype(F32))
        hc = hc * jax.nn.sigmoid(om_ref[:, cols].astype(F32))
        hc = hc * lax.rsqrt(jnp.mean(hc * hc, axis=-1, keepdims=True) + EPS)
        u = hc * hn_ref[:, cols] + sk_ref[:, cols] * xc_ref[:, cols].astype(F32)
        u = u * _silu(zm_ref[:, cols].astype(F32))
        part = _dot(u.astype(BF16), wm_ref[cols, :])
        y_b = part if y_b is None else y_b + part
    ya_in = yf_ref[...].astype(F32) * _silu(zf_ref[...].astype(F32))
    y_a = _dot(ya_in.astype(BF16), wf_ref[...])
    g_a = jax.nn.sigmoid(gl_ref[:, :D_MODEL].astype(F32))
    g_b = jax.nn.sigmoid(gl_ref[:, D_MODEL:].astype(F32))
    merged = g_a * y_a + g_b * y_b
    xn = x_ref[...] + _dot(merged.astype(BF16), wo_ref[...])
    if final_norm:
        xn = xn * lax.rsqrt(jnp.mean(xn * xn, axis=-1, keepdims=True) + EPS) * fn_ref[...]
    o_ref[...] = xn


def _merge(hf, hb, proj, xc, yf, x2, hn, sk, wf, wm, wo, fn, final_norm, tm=256):
    t = x2.shape[0]

    def rows(width, blk):
        return pl.BlockSpec((tm, width), lambda i: (i, blk))

    def full(a):
        nd = a.ndim
        return pl.BlockSpec(a.shape, lambda i: (0,) * nd)

    return pl.pallas_call(
        functools.partial(_merge_kernel, final_norm=final_norm),
        grid=(t // tm,),
        in_specs=[
            rows(M_WIDTH, 0), rows(M_WIDTH, 0),
            rows(M_WIDTH, COL_OM // M_WIDTH), rows(M_WIDTH, COL_ZM // M_WIDTH),
            rows(M_WIDTH, 0),
            rows(F_WIDTH, 0), rows(F_WIDTH, COL_ZF // F_WIDTH),
            rows(2 * D_MODEL, COL_G // (2 * D_MODEL)),
            rows(D_MODEL, 0),
            full(hn), full(sk), full(wf), full(wm), full(wo), full(fn),
        ],
        out_specs=rows(D_MODEL, 0),
        out_shape=jax.ShapeDtypeStruct((t, D_MODEL), F32),
        compiler_params=pltpu.CompilerParams(
            dimension_semantics=("parallel",),
            vmem_limit_bytes=VMEM_LIMIT),
        name="merge_out",
    )(hf, hb, proj, proj, xc, yf, proj, proj, x2, hn, sk, wf, wm, wo, fn)


def _block_diag_tiles(w):
    per = MXU_DIM // QKV_BLOCK
    w4 = w.reshape(-1, per, QKV_BLOCK, QKV_BLOCK)
    eye = jnp.eye(per, dtype=w.dtype)
    dense = jnp.einsum('gacd,ab->gacbd', w4, eye)
    return dense.reshape(-1, MXU_DIM, MXU_DIM).astype(BF16)


def kernel(x, norm_w, w_in, w_fourier, conv_w, conv_b, w_q, w_k, w_v,
           w_igate_fwd, b_igate_fwd, w_fgate_fwd, b_fgate_fwd,
           w_igate_bwd, b_igate_bwd, w_fgate_bwd, b_fgate_bwd,
           hnorm_w, skip_w, w_mlstm, w_out, final_norm_w):
    bsz, seq, d = x.shape
    depth = w_in.shape[0]
    assert d == D_MODEL and seq % (FFT_N2 * BF16_SUBLANES * 2) == 0 and seq % MCHUNK == 0
    t = bsz * seq
    nc = seq // MCHUNK
    consts = tuple(jnp.asarray(a, dtype=F32).astype(BF16) for a in _fft_constants(seq))
    fn = final_norm_w.reshape(1, d)

    x2 = x.reshape(t, d)
    for l in range(depth):
        proj = _inproj(x2, norm_w[l], w_in[l].astype(BF16))

        yt = _fourier_mix(proj, bsz, seq, consts)
        yf = jnp.transpose(yt, (0, 2, 1, 3)).reshape(t, F_WIDTH)

        wg = jnp.concatenate([w_igate_fwd[l], w_fgate_fwd[l], w_igate_bwd[l], w_fgate_bwd[l]], axis=1)
        wg = jnp.pad(wg, ((0, 0), (0, GATE_LANES - wg.shape[1])))
        wg = wg.reshape(3, M_WIDTH // MXU_DIM, MXU_DIM, GATE_LANES).astype(BF16)
        bg = jnp.concatenate([b_igate_fwd[l], b_fgate_fwd[l], b_igate_bwd[l], b_fgate_bwd[l]])
        bg = jnp.pad(bg, (0, GATE_LANES - bg.shape[0])).reshape(1, GATE_LANES)
        q, k, v, xc, gates = _conv_qkv(
            proj, seq, conv_w[l], conv_b[l].reshape(1, M_WIDTH),
            _block_diag_tiles(w_q[l]), _block_diag_tiles(w_k[l]), _block_diag_tiles(w_v[l]),
            wg, bg)

        g4 = gates[:, :4 * M_HEADS].reshape(bsz, nc, MCHUNK, 4, M_HEADS)
        g4 = jnp.transpose(g4, (0, 4, 3, 1, 2))
        hf, hb = _mlstm(q, k, v, g4, bsz, seq)

        x2 = _merge(hf, hb, proj, xc, yf, x2,
                    hnorm_w[l].reshape(1, M_WIDTH), skip_w[l].reshape(1, M_WIDTH),
                    w_fourier[l].astype(BF16), w_mlstm[l].astype(BF16), w_out[l].astype(BF16),
                    fn, final_norm=(l == depth - 1))
    return x2.reshape(bsz, seq, d)
```

```python
import functools

import numpy as np
import jax
import jax.numpy as jnp
from jax import lax
from jax.experimental import pallas as pl
from jax.experimental.pallas import tpu as pltpu

F32 = jnp.float32
BF16 = jnp.bfloat16

D_MODEL = 1024
F_WIDTH = D_MODEL
F_GROUPS = 4
F_GROUP_DIM = F_WIDTH // F_GROUPS
M_WIDTH = 2 * D_MODEL
M_HEADS = 4
M_HEAD_DIM = M_WIDTH // M_HEADS
QKV_BLOCK = 4
CONV_K = 5
EPS = 1e-6
IN_COLS = 2 * F_WIDTH + 3 * M_WIDTH + 2 * D_MODEL

MCHUNK = 256
MLSTM_HEADS_PER_STEP = 2

COL_XF = 0
COL_ZF = F_WIDTH
COL_XM = 2 * F_WIDTH
COL_ZM = COL_XM + M_WIDTH
COL_OM = COL_ZM + M_WIDTH
COL_G = COL_OM + M_WIDTH

LANES = 128
SUBLANES = 8
BF16_SUBLANES = 16
MXU_DIM = 256
GATE_LANES = 128
VMEM_LIMIT = 56 * 1024 * 1024

FFT_N2 = 32


def _dot(a, b):
    return jnp.dot(a, b, preferred_element_type=F32)


def _dot_nt(a, b):
    return lax.dot_general(a, b, (((1,), (1,)), ((), ())), preferred_element_type=F32)


def _sigmoid(x):
    return 0.5 * jnp.tanh(0.5 * x) + 0.5


def _silu(x):
    hx = 0.5 * x
    return hx * jnp.tanh(hx) + hx


def _inproj_kernel(x_ref, nw_ref, w_ref, o_ref, *, tn):
    x = x_ref[...]
    ms = jnp.mean(x * x, axis=-1, keepdims=True)
    h = (x * lax.rsqrt(ms + EPS) * nw_ref[...]).astype(BF16)
    for j in range(w_ref.shape[1] // tn):
        cols = slice(tn * j, tn * (j + 1))
        o_ref[:, cols] = _dot(h, w_ref[:, cols]).astype(BF16)


def _inproj(x2, norm_w, w_in_bf, tm=512, tn=1024):
    t, d = x2.shape
    n = w_in_bf.shape[1]
    return pl.pallas_call(
        functools.partial(_inproj_kernel, tn=tn),
        grid=(t // tm,),
        in_specs=[
            pl.BlockSpec((tm, d), lambda i: (i, 0)),
            pl.BlockSpec((1, d), lambda i: (0, 0)),
            pl.BlockSpec((d, n), lambda i: (0, 0), pipeline_mode=pl.Buffered(1)),
        ],
        out_specs=pl.BlockSpec((tm, n), lambda i: (i, 0)),
        out_shape=jax.ShapeDtypeStruct((t, n), BF16),
        compiler_params=pltpu.CompilerParams(
            dimension_semantics=("parallel",),
            vmem_limit_bytes=VMEM_LIMIT),
        name="inproj",
    )(x2, norm_w.reshape(1, d), w_in_bf)


def _fft_constants(seq):
    n2 = FFT_N2
    n1 = seq // n2
    sub = BF16_SUBLANES
    nblk = n1 // sub
    i = np.arange(nblk)[:, None, None, None]
    k2 = np.arange(n2)[None, :, None, None]
    j = np.arange(sub)[None, None, :, None]
    m2 = np.arange(n2)[None, None, None, :]
    ang = 2.0 * np.pi * (((m2 * k2) % n2) / n2 + (((sub * i + j) * k2) % seq) / seq)
    val = np.stack([np.cos(ang), -np.sin(ang)], axis=1)
    wa = np.zeros((nblk, 2, n2, sub, n2, sub), np.float64)
    for jj in range(sub):
        wa[:, :, :, jj, :, jj] = val[:, :, :, jj, :]
    wa = wa.reshape(nblk, 2 * n2 * sub, n2 * sub)
    a = np.arange(n1)
    angb = 2.0 * np.pi * ((a[:, None] * a[None, :]) % n1) / n1
    wr, wi = np.cos(angb), -np.sin(angb)
    wb = np.block([[wr, -wi], [wi, wr]])
    c = np.arange(F_GROUP_DIM)
    angc = 2.0 * np.pi * ((c[:, None] * c[None, :]) % F_GROUP_DIM) / F_GROUP_DIM
    wc = np.concatenate([np.cos(angc), np.sin(angc)], axis=0) / np.sqrt(seq * F_GROUP_DIM)
    return wa, wb, wc


def _fft_a_kernel(x_ref, wa_ref, o_ref, *, nsub):
    n2 = FFT_N2
    sub = BF16_SUBLANES
    c = x_ref.shape[-1]
    for i in range(nsub):
        rows = slice(sub * i, sub * (i + 1))
        xs = x_ref[:, rows, :].reshape(n2 * sub, c)
        z = _dot(wa_ref[i], xs)
        o_ref[:, :, rows, :] = z.astype(BF16).reshape(2, n2, sub, c)


def _fft_b_kernel(z_ref, wb_ref, wc_ref, o_ref):
    n1 = z_ref.shape[1]
    c = z_ref.shape[2]
    xs = z_ref[...].reshape(2 * n1, c)
    g = _dot(wb_ref[...], xs).astype(BF16)
    for grp in range(c // F_GROUP_DIM):
        cols = slice(F_GROUP_DIM * grp, F_GROUP_DIM * (grp + 1))
        lhs = jnp.concatenate([g[:n1, cols], g[n1:, cols]], axis=1)
        o_ref[:, cols] = _dot(lhs, wc_ref[...]).astype(BF16)


def _fourier_mix(proj, bsz, seq, consts, nsub=2):
    wa, wb, wc = consts
    n2 = FFT_N2
    n1 = seq // n2
    rblk = nsub * BF16_SUBLANES
    proj4 = proj.reshape(bsz, n2, n1, IN_COLS)
    za = pl.pallas_call(
        functools.partial(_fft_a_kernel, nsub=nsub),
        grid=(bsz, n1 // rblk),
        in_specs=[
            pl.BlockSpec((None, n2, rblk, F_WIDTH), lambda b, i: (b, 0, i, COL_XF // F_WIDTH)),
            pl.BlockSpec((nsub,) + wa.shape[1:], lambda b, i: (i, 0, 0)),
        ],
        out_specs=pl.BlockSpec((None, 2, n2, rblk, F_WIDTH), lambda b, i: (b, 0, 0, i, 0)),
        out_shape=jax.ShapeDtypeStruct((bsz, 2, n2, n1, F_WIDTH), BF16),
        compiler_params=pltpu.CompilerParams(
            dimension_semantics=("parallel", "parallel"),
            vmem_limit_bytes=VMEM_LIMIT),
        name="fft_stage_a",
    )(proj4, wa)
    yt = pl.pallas_call(
        _fft_b_kernel,
        grid=(bsz, n2),
        in_specs=[
            pl.BlockSpec((None, 2, None, n1, F_WIDTH), lambda b, k: (b, 0, k, 0, 0)),
            pl.BlockSpec(wb.shape, lambda b, k: (0, 0)),
            pl.BlockSpec(wc.shape, lambda b, k: (0, 0)),
        ],
        out_specs=pl.BlockSpec((None, None, n1, F_WIDTH), lambda b, k: (b, k, 0, 0)),
        out_shape=jax.ShapeDtypeStruct((bsz, n2, n1, F_WIDTH), BF16),
        compiler_params=pltpu.CompilerParams(
            dimension_semantics=("parallel", "parallel"),
            vmem_limit_bytes=VMEM_LIMIT),
        name="fft_stage_b",
    )(za, wb, wc)
    return yt


def _log_sigmoid(x):
    return jnp.minimum(x, 0.0) - jnp.log1p(jnp.exp(-jnp.abs(x)))


def _conv_qkv_kernel(prev_ref, cur_ref, next_ref, cw_ref, cb_ref, wq_ref, wk_ref, wkt_ref, wv_ref,
                     wg_ref, bg_ref, q_ref, k_ref, kt_ref, v_ref, xc_ref, g_ref, xs_ref,
                     *, tiles_per_seq):
    tm = cur_ref.shape[0]
    halo = BF16_SUBLANES
    pos = pl.program_id(0) % tiles_per_seq
    keep_prev = (pos != 0).astype(F32)
    keep_next = (pos != tiles_per_seq - 1).astype(F32)
    xs_ref[0:halo, :] = prev_ref[...].astype(F32) * keep_prev
    xs_ref[halo:halo + tm, :] = cur_ref[...].astype(F32)
    xs_ref[halo + tm:, :] = next_ref[...].astype(F32) * keep_next

    gacc = jnp.zeros((tm, GATE_LANES), F32)
    for g in range(M_WIDTH // MXU_DIM):
        cols = slice(MXU_DIM * g, MXU_DIM * (g + 1))
        acc = jnp.broadcast_to(cb_ref[:, cols], (tm, MXU_DIM))
        ext = xs_ref[halo - SUBLANES:halo + tm + SUBLANES, cols]
        for j in range(CONV_K):
            shift = (CONV_K // 2 - j) % ext.shape[0]
            tap = ext if shift == 0 else pltpu.roll(ext, shift, axis=0)
            acc = acc + cw_ref[j:j + 1, cols] * tap[SUBLANES:SUBLANES + tm]
        xcb = (acc * (1.0 + jnp.tanh(acc))).astype(BF16)
        q = _dot(xcb, wq_ref[g])
        k = _dot(xcb, wk_ref[g])
        kt_ref[cols, :] = _dot_nt(wkt_ref[g], xcb).astype(BF16)
        v = _dot(cur_ref[:, cols], wv_ref[g])
        qb, kb, vb = q.astype(BF16), k.astype(BF16), v.astype(BF16)
        gacc = gacc + _dot(qb, wg_ref[0, g]) + _dot(kb, wg_ref[1, g]) + _dot(vb, wg_ref[2, g])
        q_ref[:, cols] = qb
        k_ref[:, cols] = kb
        v_ref[:, cols] = vb
        xc_ref[:, cols] = xcb
    gates = gacc + bg_ref[...]
    lane = lax.broadcasted_iota(jnp.int32, gates.shape, 1)
    is_forget = (lane // M_HEADS) % 2 == 1
    g_ref[...] = jnp.where(is_forget, _log_sigmoid(gates), gates)


def _conv_qkv(proj, seq, cw, cb, wq, wk, wkt, wv, wg, bg, tm=256):
    t = proj.shape[0]
    halo = BF16_SUBLANES
    hb = tm // halo
    nhalo = t // halo
    xm_blk = COL_XM // M_WIDTH
    out_bf = jax.ShapeDtypeStruct((t, M_WIDTH), BF16)
    row_spec = pl.BlockSpec((tm, M_WIDTH), lambda i: (i, 0))

    def full(a):
        nd = a.ndim
        return pl.BlockSpec(a.shape, lambda i: (0,) * nd)

    return pl.pallas_call(
        functools.partial(_conv_qkv_kernel, tiles_per_seq=seq // tm),
        grid=(t // tm,),
        in_specs=[
            pl.BlockSpec((halo, M_WIDTH), lambda i: (jnp.maximum(i * hb - 1, 0), xm_blk)),
            pl.BlockSpec((tm, M_WIDTH), lambda i: (i, xm_blk)),
            pl.BlockSpec((halo, M_WIDTH), lambda i: (jnp.minimum((i + 1) * hb, nhalo - 1), xm_blk)),
            full(cw), full(cb), full(wq), full(wk), full(wkt), full(wv), full(wg), full(bg),
        ],
        out_specs=[row_spec, row_spec, pl.BlockSpec((M_WIDTH, tm), lambda i: (0, i)),
                   row_spec, row_spec,
                   pl.BlockSpec((tm, GATE_LANES), lambda i: (i, 0))],
        out_shape=[out_bf, out_bf, jax.ShapeDtypeStruct((M_WIDTH, t), BF16), out_bf, out_bf,
                   jax.ShapeDtypeStruct((t, GATE_LANES), F32)],
        scratch_shapes=[pltpu.VMEM((tm + 2 * halo, M_WIDTH), F32)],
        compiler_params=pltpu.CompilerParams(
            dimension_semantics=("parallel",),
            vmem_limit_bytes=VMEM_LIMIT),
        name="conv_qkv_gates",
    )(proj, proj, proj, cw, cb, wq, wk, wkt, wv, wg, bg)


GS_C, GS_M, GS_INTER, GS_ENEGM, GS_WS, GS_DECAY, GS_ROWS = 0, 1, 2, 3, 4, 5, 6


def _lane_scan(x, op, fill, reverse):
    n = x.shape[1]
    lane = lax.broadcasted_iota(jnp.int32, x.shape, 1)
    sh = 1
    while sh < n:
        if reverse:
            x = op(x, jnp.where(lane < n - sh, pltpu.roll(x, n - sh, axis=1), fill))
        else:
            x = op(x, jnp.where(lane >= sh, pltpu.roll(x, sh, axis=1), fill))
        sh *= 2
    return x


def _gate_prep(i_pre, log_f, gs_ref, tmp_ref, *, reverse):
    nc, L = i_pre.shape
    b = _lane_scan(log_f, jnp.add, 0.0, reverse)
    c = i_pre - b
    cm = _lane_scan(c, jnp.maximum, -jnp.inf, reverse)
    end = slice(0, 1) if reverse else slice(L - 1, L)
    g = b[:, end]
    cmt = cm[:, end]
    tmp_ref[0] = jnp.broadcast_to(g, (nc, LANES))
    tmp_ref[1] = jnp.broadcast_to(g + cmt, (nc, LANES))

    def body(step, m):
        idx = nc - 1 - step if reverse else step
        tmp_ref[2, pl.ds(idx, 1), :] = m
        return jnp.maximum(m + tmp_ref[0, pl.ds(idx, 1), :], tmp_ref[1, pl.ds(idx, 1), :])

    lax.fori_loop(0, nc, body, jnp.zeros((1, LANES), F32))
    m_prev = tmp_ref[2][:, 0:1]
    big_m = jnp.maximum(m_prev, cm)
    mx = jnp.maximum(m_prev, cmt)
    gs_ref[GS_C] = c
    gs_ref[GS_M] = big_m
    gs_ref[GS_INTER] = jnp.exp(m_prev - big_m)
    gs_ref[GS_ENEGM] = jnp.exp(-b - big_m)
    gs_ref[GS_WS] = jnp.exp(c - mx)
    gs_ref[GS_DECAY] = jnp.broadcast_to(jnp.exp(m_prev - mx), (nc, L))


def _mlstm_chunk(q, k, kt, v, gs_ref, ci, ct_ref, n_ref, *, reverse):
    L = q.shape[0]

    def row(r):
        return gs_ref[r, pl.ds(ci, 1), :]

    c_row = row(GS_C)
    sub = lax.broadcasted_iota(jnp.int32, (SUBLANES, L), 0)
    packed = jnp.where(sub == 0, row(GS_M),
                       jnp.where(sub == 1, row(GS_INTER),
                                 jnp.where(sub == 2, row(GS_ENEGM), row(GS_WS))))
    tr = jnp.concatenate([packed, jnp.zeros((LANES - SUBLANES, L), F32)], axis=0).T
    m_col, inter, enegm, ws = tr[:, 0:1], tr[:, 1:2], tr[:, 2:3], tr[:, 3:4]
    decay = row(GS_DECAY)[:, 0:1]

    r_i = lax.broadcasted_iota(jnp.int32, (L, L), 0)
    c_i = lax.broadcasted_iota(jnp.int32, (L, L), 1)
    mask = (c_i >= r_i) if reverse else (c_i <= r_i)
    p = jnp.exp(jnp.where(mask, c_row - m_col, -jnp.inf))
    s = _dot_nt(q, k) * p
    num = _dot(s.astype(BF16), v) + inter * _dot(q, ct_ref[...].astype(BF16))
    qn = jnp.sum(q.astype(F32) * n_ref[...], axis=1, keepdims=True)
    den = jnp.sum(s, axis=1, keepdims=True) + inter * qn
    h = num / jnp.maximum(jnp.abs(den), enegm)

    vw = (ws * v.astype(F32)).astype(BF16)
    ct_ref[...] = decay * ct_ref[...] + _dot(kt, vw)
    n_ref[...] = decay * n_ref[...] + jnp.sum(ws * k.astype(F32), axis=0, keepdims=True)
    return h


def _mlstm_kernel(gates_ref, qf_ref, kf_ref, ktf_ref, vf_ref, qb_ref, kb_ref, ktb_ref, vb_ref,
                  hf_ref, hb_ref,
                  ct_ref, n_ref, gs_ref, tmp_ref, *, hp, nc):
    c = pl.program_id(2)
    dh = M_HEAD_DIM

    @pl.when(c == 0)
    def _():
        ct_ref[...] = jnp.zeros_like(ct_ref)
        n_ref[...] = jnp.zeros_like(n_ref)
        for hh in range(hp):
            _gate_prep(gates_ref[hh, 0], gates_ref[hh, 1], gs_ref.at[hh, 0], tmp_ref, reverse=False)
            _gate_prep(gates_ref[hh, 2], gates_ref[hh, 3], gs_ref.at[hh, 1], tmp_ref, reverse=True)

    for hh in range(hp):
        cols = slice(dh * hh, dh * (hh + 1))
        hf = _mlstm_chunk(qf_ref[:, cols], kf_ref[:, cols], ktf_ref[cols, :], vf_ref[:, cols],
                          gs_ref.at[hh, 0], c, ct_ref.at[hh, 0], n_ref.at[hh, 0], reverse=False)
        hf_ref[:, cols] = hf.astype(hf_ref.dtype)
        hb = _mlstm_chunk(qb_ref[:, cols], kb_ref[:, cols], ktb_ref[cols, :], vb_ref[:, cols],
                          gs_ref.at[hh, 1], nc - 1 - c, ct_ref.at[hh, 1], n_ref.at[hh, 1],
                          reverse=True)
        hb_ref[:, cols] = hb.astype(hb_ref.dtype)


def _mlstm(q, k, kt, v, gates4, bsz, seq, hp=MLSTM_HEADS_PER_STEP):
    t = q.shape[0]
    L = MCHUNK
    nc = seq // L
    dh = M_HEAD_DIM
    width = hp * dh

    qkv_f = pl.BlockSpec((L, width), lambda b, h, c: (b * nc + c, h))
    qkv_b = pl.BlockSpec((L, width), lambda b, h, c: (b * nc + nc - 1 - c, h))
    kt_f = pl.BlockSpec((width, L), lambda b, h, c: (h, b * nc + c))
    kt_b = pl.BlockSpec((width, L), lambda b, h, c: (h, b * nc + nc - 1 - c))
    g_spec = pl.BlockSpec((None, hp, 4, nc, L), lambda b, h, c: (b, h, 0, 0, 0))
    out = jax.ShapeDtypeStruct((t, M_WIDTH), BF16)
    return pl.pallas_call(
        functools.partial(_mlstm_kernel, hp=hp, nc=nc),
        grid=(bsz, M_HEADS // hp, nc),
        in_specs=[g_spec, qkv_f, qkv_f, kt_f, qkv_f, qkv_b, qkv_b, kt_b, qkv_b],
        out_specs=[qkv_f, qkv_b],
        out_shape=[out, out],
        scratch_shapes=[
            pltpu.VMEM((hp, 2, dh, dh), F32),
            pltpu.VMEM((hp, 2, 1, dh), F32),
            pltpu.VMEM((hp, 2, GS_ROWS, nc, L), F32),
            pltpu.VMEM((3, nc, LANES), F32),
        ],
        compiler_params=pltpu.CompilerParams(
            dimension_semantics=("parallel", "parallel", "arbitrary"),
            vmem_limit_bytes=VMEM_LIMIT),
        name="mlstm",
    )(gates4, q, k, kt, v, q, k, kt, v)


def _merge_kernel(hf_ref, hb_ref, om_ref, zm_ref, xc_ref, yf_ref, zf_ref, gl_ref, x_ref,
                  hn_ref, sk_ref, wf_ref, wm_ref, wo_ref, fn_ref, o_ref, *, final_norm, sub_rows):
    for r in range(o_ref.shape[0] // sub_rows):
        rows = slice(sub_rows * r, sub_rows * (r + 1))
        y_b = None
        for h in range(M_HEADS):
            cols = slice(M_HEAD_DIM * h, M_HEAD_DIM * (h + 1))
            hc2 = (hf_ref[rows, cols].astype(F32) + hb_ref[rows, cols].astype(F32))
            hc2 = hc2 * (1.0 + jnp.tanh(om_ref[rows, cols].astype(F32)))
            hc = hc2 * lax.rsqrt(jnp.mean(hc2 * hc2, axis=-1, keepdims=True) + 4.0 * EPS)
            u = hc * hn_ref[:, cols] + sk_ref[:, cols] * xc_ref[rows, cols].astype(F32)
            hz = zm_ref[rows, cols].astype(F32)
            u = u * (hz * (1.0 + jnp.tanh(hz)))
            part = _dot(u.astype(BF16), wm_ref[cols, :])
            y_b = part if y_b is None else y_b + part
        hzf = zf_ref[rows, :].astype(F32)
        ya_in = yf_ref[rows, :].astype(F32) * (hzf * (1.0 + jnp.tanh(hzf)))
        y_a = _dot(ya_in.astype(BF16), wf_ref[...])
        t_a = jnp.tanh(gl_ref[rows, :D_MODEL].astype(F32))
        t_b = jnp.tanh(gl_ref[rows, D_MODEL:].astype(F32))
        merged2 = (y_a + y_b) + (t_a * y_a + t_b * y_b)
        xn = x_ref[rows, :] + _dot(merged2.astype(BF16), wo_ref[...])
        if final_norm:
            xn = xn * lax.rsqrt(jnp.mean(xn * xn, axis=-1, keepdims=True) + EPS) * fn_ref[...]
        o_ref[rows, :] = xn


def _merge(hf, hb, proj, xc, yf, x2, hn, sk, wf, wm, wo, fn, final_norm, tm=512, sub_rows=256):
    t = x2.shape[0]

    def rows(width, blk):
        return pl.BlockSpec((tm, width), lambda i: (i, blk))

    def full(a):
        nd = a.ndim
        return pl.BlockSpec(a.shape, lambda i: (0,) * nd)

    return pl.pallas_call(
        functools.partial(_merge_kernel, final_norm=final_norm, sub_rows=sub_rows),
        grid=(t // tm,),
        in_specs=[
            rows(M_WIDTH, 0), rows(M_WIDTH, 0),
            rows(M_WIDTH, COL_OM // M_WIDTH), rows(M_WIDTH, COL_ZM // M_WIDTH),
            rows(M_WIDTH, 0),
            rows(F_WIDTH, 0), rows(F_WIDTH, COL_ZF // F_WIDTH),
            rows(2 * D_MODEL, COL_G // (2 * D_MODEL)),
            rows(D_MODEL, 0),
            full(hn), full(sk), full(wf), full(wm), full(wo), full(fn),
        ],
        out_specs=rows(D_MODEL, 0),
        out_shape=jax.ShapeDtypeStruct((t, D_MODEL), F32),
        compiler_params=pltpu.CompilerParams(
            dimension_semantics=("parallel",),
            vmem_limit_bytes=VMEM_LIMIT),
        name="merge_out",
    )(hf, hb, proj, proj, xc, yf, proj, proj, x2, hn, sk, wf, wm, wo, fn)


def _block_diag_tiles(w):
    rows = w.reshape(-1, MXU_DIM, QKV_BLOCK)
    dense = jnp.tile(rows, (1, 1, MXU_DIM // QKV_BLOCK))
    r = lax.broadcasted_iota(jnp.int32, (MXU_DIM, MXU_DIM), 0) // QKV_BLOCK
    c = lax.broadcasted_iota(jnp.int32, (MXU_DIM, MXU_DIM), 1) // QKV_BLOCK
    return jnp.where(r == c, dense, 0.0).astype(BF16)


def kernel(x, norm_w, w_in, w_fourier, conv_w, conv_b, w_q, w_k, w_v,
           w_igate_fwd, b_igate_fwd, w_fgate_fwd, b_fgate_fwd,
           w_igate_bwd, b_igate_bwd, w_fgate_bwd, b_fgate_bwd,
           hnorm_w, skip_w, w_mlstm, w_out, final_norm_w):
    bsz, seq, d = x.shape
    depth = w_in.shape[0]
    assert d == D_MODEL and seq % (FFT_N2 * BF16_SUBLANES * 2) == 0 and seq % MCHUNK == 0
    t = bsz * seq
    nc = seq // MCHUNK
    consts = tuple(jnp.asarray(a, dtype=F32).astype(BF16) for a in _fft_constants(seq))
    fn = final_norm_w.reshape(1, d)

    x2 = x.reshape(t, d)
    for l in range(depth):
        col = lax.broadcasted_iota(jnp.int32, (1, IN_COLS), 1)
        half = ((col >= COL_ZF) & (col < COL_XM)) | (col >= COL_ZM)
        proj = _inproj(x2, norm_w[l], (w_in[l] * jnp.where(half, 0.5, 1.0)).astype(BF16))

        yt = _fourier_mix(proj, bsz, seq, consts)
        yf = jnp.transpose(yt, (0, 2, 1, 3)).reshape(t, F_WIDTH)

        wg = jnp.concatenate([w_igate_fwd[l], w_fgate_fwd[l], w_igate_bwd[l], w_fgate_bwd[l]], axis=1)
        wg = jnp.pad(wg, ((0, 0), (0, GATE_LANES - wg.shape[1])))
        wg = wg.reshape(3, M_WIDTH // MXU_DIM, MXU_DIM, GATE_LANES)
        qscale = M_HEAD_DIM ** -0.5
        wg = (wg * jnp.asarray([1.0 / qscale, 1.0, 1.0], F32).reshape(3, 1, 1, 1)).astype(BF16)
        bg = jnp.concatenate([b_igate_fwd[l], b_fgate_fwd[l], b_igate_bwd[l], b_fgate_bwd[l]])
        bg = jnp.pad(bg, (0, GATE_LANES - bg.shape[0])).reshape(1, GATE_LANES)
        wk_tiles = _block_diag_tiles(w_k[l])
        q, k, kt, v, xc, gates = _conv_qkv(
            proj, seq, 0.5 * conv_w[l], 0.5 * conv_b[l].reshape(1, M_WIDTH),
            _block_diag_tiles(w_q[l] * qscale), wk_tiles, jnp.swapaxes(wk_tiles, 1, 2),
            _block_diag_tiles(w_v[l]), wg, bg)

        g4 = gates[:, :4 * M_HEADS].reshape(bsz, nc, MCHUNK, 4, M_HEADS)
        g4 = jnp.transpose(g4, (0, 4, 3, 1, 2))
        hf, hb = _mlstm(q, k, kt, v, g4, bsz, seq)

        x2 = _merge(hf, hb, proj, xc, yf, x2,
                    hnorm_w[l].reshape(1, M_WIDTH), skip_w[l].reshape(1, M_WIDTH),
                    w_fourier[l].astype(BF16), w_mlstm[l].astype(BF16),
                    (0.5 * w_out[l]).astype(BF16),
                    fn, final_norm=(l == depth - 1))
    return x2.reshape(bsz, seq, d)
```

```python
import functools

import numpy as np
import jax
import jax.numpy as jnp
from jax import lax
from jax.experimental import pallas as pl
from jax.experimental.pallas import tpu as pltpu

F32 = jnp.float32
BF16 = jnp.bfloat16

D_MODEL = 1024
F_WIDTH = D_MODEL
F_GROUPS = 4
F_GROUP_DIM = F_WIDTH // F_GROUPS
M_WIDTH = 2 * D_MODEL
M_HEADS = 4
M_HEAD_DIM = M_WIDTH // M_HEADS
QKV_BLOCK = 4
CONV_K = 5
EPS = 1e-6
IN_COLS = 2 * F_WIDTH + 3 * M_WIDTH + 2 * D_MODEL

MCHUNK = 256
MLSTM_HEADS_PER_STEP = 2

COL_XF = 0
COL_ZF = F_WIDTH
COL_XM = 2 * F_WIDTH
COL_ZM = COL_XM + M_WIDTH
COL_OM = COL_ZM + M_WIDTH
COL_G = COL_OM + M_WIDTH

LANES = 128
SUBLANES = 8
BF16_SUBLANES = 16
MXU_DIM = 256
GATE_LANES = 128
VMEM_LIMIT = 56 * 1024 * 1024

FFT_N2 = 32


def _dot(a, b):
    return jnp.dot(a, b, preferred_element_type=F32)


def _dot_nt(a, b):
    return lax.dot_general(a, b, (((1,), (1,)), ((), ())), preferred_element_type=F32)


def _sigmoid(x):
    return 0.5 * jnp.tanh(0.5 * x) + 0.5


def _silu(x):
    hx = 0.5 * x
    return hx * jnp.tanh(hx) + hx


def _rmsnorm_bf16(x, w):
    ms = jnp.mean(x * x, axis=-1, keepdims=True)
    return (x * lax.rsqrt(ms + EPS) * w).astype(BF16)


def _resident(shape, index_map):
    return pl.BlockSpec(shape, index_map, pipeline_mode=pl.Buffered(1))


def _inproj_kernel(x_ref, nw_ref, wf_ref, wm_ref, xf_ref, xm_ref, *, sub_rows):
    for r in range(x_ref.shape[0] // sub_rows):
        rows = slice(sub_rows * r, sub_rows * (r + 1))
        h = _rmsnorm_bf16(x_ref[rows, :], nw_ref[...])
        xf_ref[rows, :] = _dot(h, wf_ref[...]).astype(BF16)
        xm_ref[rows, :] = _dot(h, wm_ref[...]).astype(BF16)


def _inproj(x2, norm_w, w_in_bf, tm=512, sub_rows=256):
    t, d = x2.shape
    return pl.pallas_call(
        functools.partial(_inproj_kernel, sub_rows=sub_rows),
        grid=(t // tm,),
        in_specs=[
            pl.BlockSpec((tm, d), lambda i: (i, 0)),
            pl.BlockSpec((1, d), lambda i: (0, 0)),
            _resident((d, F_WIDTH), lambda i: (0, COL_XF // F_WIDTH)),
            _resident((d, M_WIDTH), lambda i: (0, COL_XM // M_WIDTH)),
        ],
        out_specs=[pl.BlockSpec((tm, F_WIDTH), lambda i: (i, 0)),
                   pl.BlockSpec((tm, M_WIDTH), lambda i: (i, 0))],
        out_shape=[jax.ShapeDtypeStruct((t, F_WIDTH), BF16),
                   jax.ShapeDtypeStruct((t, M_WIDTH), BF16)],
        compiler_params=pltpu.CompilerParams(
            dimension_semantics=("parallel",),
            vmem_limit_bytes=VMEM_LIMIT),
        name="inproj",
    )(x2, norm_w.reshape(1, d), w_in_bf, w_in_bf)


def _fft_constants(seq):
    n2 = FFT_N2
    n1 = seq // n2
    sub = BF16_SUBLANES
    nblk = n1 // sub
    i = np.arange(nblk)[:, None, None, None]
    k2 = np.arange(n2)[None, :, None, None]
    j = np.arange(sub)[None, None, :, None]
    m2 = np.arange(n2)[None, None, None, :]
    ang = 2.0 * np.pi * (((m2 * k2) % n2) / n2 + (((sub * i + j) * k2) % seq) / seq)
    val = np.stack([np.cos(ang), -np.sin(ang)], axis=1)
    wa = np.zeros((nblk, 2, n2, sub, n2, sub), np.float64)
    for jj in range(sub):
        wa[:, :, :, jj, :, jj] = val[:, :, :, jj, :]
    wa = wa.reshape(nblk, 2 * n2 * sub, n2 * sub)
    a = np.arange(n1)
    angb = 2.0 * np.pi * ((a[:, None] * a[None, :]) % n1) / n1
    wr, wi = np.cos(angb), -np.sin(angb)
    wb = np.block([[wr, -wi], [wi, wr]])
    c = np.arange(F_GROUP_DIM)
    angc = 2.0 * np.pi * ((c[:, None] * c[None, :]) % F_GROUP_DIM) / F_GROUP_DIM
    wc = np.concatenate([np.cos(angc), np.sin(angc)], axis=0) / np.sqrt(seq * F_GROUP_DIM)
    return wa, wb, wc


def _fft_a_kernel(x_ref, wa_ref, o_ref, *, nsub):
    n2 = FFT_N2
    sub = BF16_SUBLANES
    c = x_ref.shape[-1]
    for i in range(nsub):
        rows = slice(sub * i, sub * (i + 1))
        xs = x_ref[:, rows, :].reshape(n2 * sub, c)
        z = _dot(wa_ref[i], xs)
        o_ref[:, :, rows, :] = z.astype(BF16).reshape(2, n2, sub, c)


def _fft_b_kernel(z_ref, wb_ref, wc_ref, o_ref):
    n1 = z_ref.shape[1]
    c = z_ref.shape[2]
    xs = z_ref[...].reshape(2 * n1, c)
    g = _dot(wb_ref[...], xs).astype(BF16)
    for grp in range(c // F_GROUP_DIM):
        cols = slice(F_GROUP_DIM * grp, F_GROUP_DIM * (grp + 1))
        lhs = jnp.concatenate([g[:n1, cols], g[n1:, cols]], axis=1)
        o_ref[:, cols] = _dot(lhs, wc_ref[...]).astype(BF16)


def _fourier_mix(xf, bsz, seq, consts, nsub=2):
    wa, wb, wc = consts
    n2 = FFT_N2
    n1 = seq // n2
    rblk = nsub * BF16_SUBLANES
    proj4 = xf.reshape(bsz, n2, n1, F_WIDTH)
    za = pl.pallas_call(
        functools.partial(_fft_a_kernel, nsub=nsub),
        grid=(bsz, n1 // rblk),
        in_specs=[
            pl.BlockSpec((None, n2, rblk, F_WIDTH), lambda b, i: (b, 0, i, 0)),
            pl.BlockSpec((nsub,) + wa.shape[1:], lambda b, i: (i, 0, 0)),
        ],
        out_specs=pl.BlockSpec((None, 2, n2, rblk, F_WIDTH), lambda b, i: (b, 0, 0, i, 0)),
        out_shape=jax.ShapeDtypeStruct((bsz, 2, n2, n1, F_WIDTH), BF16),
        compiler_params=pltpu.CompilerParams(
            dimension_semantics=("parallel", "parallel"),
            vmem_limit_bytes=VMEM_LIMIT),
        name="fft_stage_a",
    )(proj4, wa)
    yt = pl.pallas_call(
        _fft_b_kernel,
        grid=(bsz, n2),
        in_specs=[
            pl.BlockSpec((None, 2, None, n1, F_WIDTH), lambda b, k: (b, 0, k, 0, 0)),
            pl.BlockSpec(wb.shape, lambda b, k: (0, 0)),
            pl.BlockSpec(wc.shape, lambda b, k: (0, 0)),
        ],
        out_specs=pl.BlockSpec((None, None, n1, F_WIDTH), lambda b, k: (b, k, 0, 0)),
        out_shape=jax.ShapeDtypeStruct((bsz, n2, n1, F_WIDTH), BF16),
        compiler_params=pltpu.CompilerParams(
            dimension_semantics=("parallel", "parallel"),
            vmem_limit_bytes=VMEM_LIMIT),
        name="fft_stage_b",
    )(za, wb, wc)
    return yt


def _log_sigmoid(x):
    return jnp.minimum(x, 0.0) - jnp.log1p(jnp.exp(-jnp.abs(x)))


def _conv_qkv_kernel(prev_ref, cur_ref, next_ref, cw_ref, cb_ref, wq_ref, wk_ref, wkt_ref, wv_ref,
                     wg_ref, bg_ref, q_ref, k_ref, kt_ref, v_ref, xc_ref, g_ref, xs_ref,
                     *, tiles_per_seq):
    tm = cur_ref.shape[0]
    halo = BF16_SUBLANES
    pos = pl.program_id(0) % tiles_per_seq
    keep_prev = (pos != 0).astype(F32)
    keep_next = (pos != tiles_per_seq - 1).astype(F32)
    xs_ref[0:halo, :] = prev_ref[...].astype(F32) * keep_prev
    xs_ref[halo:halo + tm, :] = cur_ref[...].astype(F32)
    xs_ref[halo + tm:, :] = next_ref[...].astype(F32) * keep_next

    gacc = jnp.zeros((tm, GATE_LANES), F32)
    for g in range(M_WIDTH // MXU_DIM):
        cols = slice(MXU_DIM * g, MXU_DIM * (g + 1))
        acc = jnp.broadcast_to(cb_ref[:, cols], (tm, MXU_DIM))
        ext = xs_ref[halo - SUBLANES:halo + tm + SUBLANES, cols]
        for j in range(CONV_K):
            shift = (CONV_K // 2 - j) % ext.shape[0]
            tap = ext if shift == 0 else pltpu.roll(ext, shift, axis=0)
            acc = acc + cw_ref[j:j + 1, cols] * tap[SUBLANES:SUBLANES + tm]
        xcb = (acc * (1.0 + jnp.tanh(acc))).astype(BF16)
        q = _dot(xcb, wq_ref[g])
        k = _dot(xcb, wk_ref[g])
        kt_ref[cols, :] = _dot_nt(wkt_ref[g], xcb).astype(BF16)
        v = _dot(cur_ref[:, cols], wv_ref[g])
        qb, kb, vb = q.astype(BF16), k.astype(BF16), v.astype(BF16)
        gacc = gacc + _dot(qb, wg_ref[0, g]) + _dot(kb, wg_ref[1, g]) + _dot(vb, wg_ref[2, g])
        q_ref[:, cols] = qb
        k_ref[:, cols] = kb
        v_ref[:, cols] = vb
        xc_ref[:, cols] = xcb
    gates = gacc + bg_ref[...]
    lane = lax.broadcasted_iota(jnp.int32, gates.shape, 1)
    is_forget = (lane // M_HEADS) % 2 == 1
    g_ref[...] = jnp.where(is_forget, _log_sigmoid(gates), gates)


def _conv_qkv(proj, seq, cw, cb, wq, wk, wkt, wv, wg, bg, tm=256):
    t = proj.shape[0]
    halo = BF16_SUBLANES
    hb = tm // halo
    nhalo = t // halo
    xm_blk = 0
    out_bf = jax.ShapeDtypeStruct((t, M_WIDTH), BF16)
    row_spec = pl.BlockSpec((tm, M_WIDTH), lambda i: (i, 0))

    def full(a):
        nd = a.ndim
        return pl.BlockSpec(a.shape, lambda i: (0,) * nd)

    return pl.pallas_call(
        functools.partial(_conv_qkv_kernel, tiles_per_seq=seq // tm),
        grid=(t // tm,),
        in_specs=[
            pl.BlockSpec((halo, M_WIDTH), lambda i: (jnp.maximum(i * hb - 1, 0), xm_blk)),
            pl.BlockSpec((tm, M_WIDTH), lambda i: (i, xm_blk)),
            pl.BlockSpec((halo, M_WIDTH), lambda i: (jnp.minimum((i + 1) * hb, nhalo - 1), xm_blk)),
            full(cw), full(cb), full(wq), full(wk), full(wkt), full(wv), full(wg), full(bg),
        ],
        out_specs=[row_spec, row_spec, pl.BlockSpec((M_WIDTH, tm), lambda i: (0, i)),
                   row_spec, row_spec,
                   pl.BlockSpec((tm, GATE_LANES), lambda i: (i, 0))],
        out_shape=[out_bf, out_bf, jax.ShapeDtypeStruct((M_WIDTH, t), BF16), out_bf, out_bf,
                   jax.ShapeDtypeStruct((t, GATE_LANES), F32)],
        scratch_shapes=[pltpu.VMEM((tm + 2 * halo, M_WIDTH), F32)],
        compiler_params=pltpu.CompilerParams(
            dimension_semantics=("parallel",),
            vmem_limit_bytes=VMEM_LIMIT),
        name="conv_qkv_gates",
    )(proj, proj, proj, cw, cb, wq, wk, wkt, wv, wg, bg)


GS_C, GS_M, GS_INTER, GS_ENEGM, GS_WS, GS_DECAY, GS_ROWS = 0, 1, 2, 3, 4, 5, 6


def _lane_scan(x, op, fill, reverse):
    n = x.shape[1]
    lane = lax.broadcasted_iota(jnp.int32, x.shape, 1)
    sh = 1
    while sh < n:
        if reverse:
            x = op(x, jnp.where(lane < n - sh, pltpu.roll(x, n - sh, axis=1), fill))
        else:
            x = op(x, jnp.where(lane >= sh, pltpu.roll(x, sh, axis=1), fill))
        sh *= 2
    return x


def _gate_prep(i_pre, log_f, gs_ref, tmp_ref, *, reverse):
    nc, L = i_pre.shape
    b = _lane_scan(log_f, jnp.add, 0.0, reverse)
    c = i_pre - b
    cm = _lane_scan(c, jnp.maximum, -jnp.inf, reverse)
    end = slice(0, 1) if reverse else slice(L - 1, L)
    g = b[:, end]
    cmt = cm[:, end]
    tmp_ref[0] = jnp.broadcast_to(g, (nc, LANES))
    tmp_ref[1] = jnp.broadcast_to(g + cmt, (nc, LANES))

    def body(step, m):
        idx = nc - 1 - step if reverse else step
        tmp_ref[2, pl.ds(idx, 1), :] = m
        return jnp.maximum(m + tmp_ref[0, pl.ds(idx, 1), :], tmp_ref[1, pl.ds(idx, 1), :])

    lax.fori_loop(0, nc, body, jnp.zeros((1, LANES), F32))
    m_prev = tmp_ref[2][:, 0:1]
    big_m = jnp.maximum(m_prev, cm)
    mx = jnp.maximum(m_prev, cmt)
    gs_ref[GS_C] = c
    gs_ref[GS_M] = big_m
    gs_ref[GS_INTER] = jnp.exp(m_prev - big_m)
    gs_ref[GS_ENEGM] = jnp.exp(-b - big_m)
    gs_ref[GS_WS] = jnp.exp(c - mx)
    gs_ref[GS_DECAY] = jnp.broadcast_to(jnp.exp(m_prev - mx), (nc, L))


def _mlstm_chunk(q, k, kt, v, gs_ref, ci, ct_ref, n_ref, *, reverse):
    L = q.shape[0]

    def row(r):
        return gs_ref[r, pl.ds(ci, 1), :]

    c_row = row(GS_C)
    sub = lax.broadcasted_iota(jnp.int32, (SUBLANES, L), 0)
    packed = jnp.where(sub == 0, row(GS_M),
                       jnp.where(sub == 1, row(GS_INTER),
                                 jnp.where(sub == 2, row(GS_ENEGM), row(GS_WS))))
    tr = jnp.concatenate([packed, jnp.zeros((LANES - SUBLANES, L), F32)], axis=0).T
    m_col, inter, enegm, ws = tr[:, 0:1], tr[:, 1:2], tr[:, 2:3], tr[:, 3:4]
    decay = row(GS_DECAY)[:, 0:1]

    r_i = lax.broadcasted_iota(jnp.int32, (L, L), 0)
    c_i = lax.broadcasted_iota(jnp.int32, (L, L), 1)
    mask = (c_i >= r_i) if reverse else (c_i <= r_i)
    p = jnp.exp(jnp.where(mask, c_row - m_col, -jnp.inf))
    s = _dot_nt(q, k) * p
    num = _dot(s.astype(BF16), v) + inter * _dot(q, ct_ref[...].astype(BF16))
    qn = jnp.sum(q.astype(F32) * n_ref[...], axis=1, keepdims=True)
    den = jnp.sum(s, axis=1, keepdims=True) + inter * qn
    h = num / jnp.maximum(jnp.abs(den), enegm)

    vw = (ws * v.astype(F32)).astype(BF16)
    ct_ref[...] = decay * ct_ref[...] + _dot(kt, vw)
    n_ref[...] = decay * n_ref[...] + jnp.sum(ws * k.astype(F32), axis=0, keepdims=True)
    return h


def _mlstm_kernel(gates_ref, qf_ref, kf_ref, ktf_ref, vf_ref, qb_ref, kb_ref, ktb_ref, vb_ref,
                  hf_ref, hb_ref,
                  ct_ref, n_ref, gs_ref, tmp_ref, *, hp, nc):
    c = pl.program_id(2)
    dh = M_HEAD_DIM

    @pl.when(c == 0)
    def _():
        ct_ref[...] = jnp.zeros_like(ct_ref)
        n_ref[...] = jnp.zeros_like(n_ref)
        for hh in range(hp):
            _gate_prep(gates_ref[hh, 0], gates_ref[hh, 1], gs_ref.at[hh, 0], tmp_ref, reverse=False)
            _gate_prep(gates_ref[hh, 2], gates_ref[hh, 3], gs_ref.at[hh, 1], tmp_ref, reverse=True)

    for hh in range(hp):
        cols = slice(dh * hh, dh * (hh + 1))
        hf = _mlstm_chunk(qf_ref[:, cols], kf_ref[:, cols], ktf_ref[cols, :], vf_ref[:, cols],
                          gs_ref.at[hh, 0], c, ct_ref.at[hh, 0], n_ref.at[hh, 0], reverse=False)
        hf_ref[:, cols] = hf.astype(hf_ref.dtype)
        hb = _mlstm_chunk(qb_ref[:, cols], kb_ref[:, cols], ktb_ref[cols, :], vb_ref[:, cols],
                          gs_ref.at[hh, 1], nc - 1 - c, ct_ref.at[hh, 1], n_ref.at[hh, 1],
                          reverse=True)
        hb_ref[:, cols] = hb.astype(hb_ref.dtype)


def _mlstm(q, k, kt, v, gates4, bsz, seq, hp=MLSTM_HEADS_PER_STEP):
    t = q.shape[0]
    L = MCHUNK
    nc = seq // L
    dh = M_HEAD_DIM
    width = hp * dh

    qkv_f = pl.BlockSpec((L, width), lambda b, h, c: (b * nc + c, h))
    qkv_b = pl.BlockSpec((L, width), lambda b, h, c: (b * nc + nc - 1 - c, h))
    kt_f = pl.BlockSpec((width, L), lambda b, h, c: (h, b * nc + c))
    kt_b = pl.BlockSpec((width, L), lambda b, h, c: (h, b * nc + nc - 1 - c))
    g_spec = pl.BlockSpec((None, hp, 4, nc, L), lambda b, h, c: (b, h, 0, 0, 0))
    out = jax.ShapeDtypeStruct((t, M_WIDTH), BF16)
    return pl.pallas_call(
        functools.partial(_mlstm_kernel, hp=hp, nc=nc),
        grid=(bsz, M_HEADS // hp, nc),
        in_specs=[g_spec, qkv_f, qkv_f, kt_f, qkv_f, qkv_b, qkv_b, kt_b, qkv_b],
        out_specs=[qkv_f, qkv_b],
        out_shape=[out, out],
        scratch_shapes=[
            pltpu.VMEM((hp, 2, dh, dh), F32),
            pltpu.VMEM((hp, 2, 1, dh), F32),
            pltpu.VMEM((hp, 2, GS_ROWS, nc, L), F32),
            pltpu.VMEM((3, nc, LANES), F32),
        ],
        compiler_params=pltpu.CompilerParams(
            dimension_semantics=("parallel", "parallel", "arbitrary"),
            vmem_limit_bytes=VMEM_LIMIT),
        name="mlstm",
    )(gates4, q, k, kt, v, q, k, kt, v)


def _merge_kernel(hf_ref, hb_ref, xc_ref, yf_ref, x_ref, nw_ref, wzf_ref, wzm_ref, wom_ref, wg_ref,
                  hn_ref, sk_ref, wf_ref, wm_ref, wo_ref, fn_ref, o_ref, *, final_norm, sub_rows):
    for r in range(o_ref.shape[0] // sub_rows):
        rows = slice(sub_rows * r, sub_rows * (r + 1))
        hin = _rmsnorm_bf16(x_ref[rows, :], nw_ref[...])
        y_b = None
        for h in range(M_HEADS):
            cols = slice(M_HEAD_DIM * h, M_HEAD_DIM * (h + 1))
            hc2 = (hf_ref[rows, cols].astype(F32) + hb_ref[rows, cols].astype(F32))
            hc2 = hc2 * (1.0 + jnp.tanh(_dot(hin, wom_ref[:, cols])))
            hc = hc2 * lax.rsqrt(jnp.mean(hc2 * hc2, axis=-1, keepdims=True) + 4.0 * EPS)
            u = hc * hn_ref[:, cols] + sk_ref[:, cols] * xc_ref[rows, cols].astype(F32)
            hz = _dot(hin, wzm_ref[:, cols])
            u = u * (hz * (1.0 + jnp.tanh(hz)))
            part = _dot(u.astype(BF16), wm_ref[cols, :])
            y_b = part if y_b is None else y_b + part
        hzf = _dot(hin, wzf_ref[...])
        ya_in = yf_ref[rows, :].astype(F32) * (hzf * (1.0 + jnp.tanh(hzf)))
        y_a = _dot(ya_in.astype(BF16), wf_ref[...])
        t_a = jnp.tanh(_dot(hin, wg_ref[:, :D_MODEL]))
        t_b = jnp.tanh(_dot(hin, wg_ref[:, D_MODEL:]))
        merged2 = (y_a + y_b) + (t_a * y_a + t_b * y_b)
        xn = x_ref[rows, :] + _dot(merged2.astype(BF16), wo_ref[...])
        if final_norm:
            xn = xn * lax.rsqrt(jnp.mean(xn * xn, axis=-1, keepdims=True) + EPS) * fn_ref[...]
        o_ref[rows, :] = xn


def _merge(hf, hb, xc, yf, x2, nw, w_in_bf, hn, sk, wf, wm, wo, fn, final_norm,
           tm=512, sub_rows=256):
    t, d = x2.shape

    def rows(width):
        return pl.BlockSpec((tm, width), lambda i: (i, 0))

    def full(a):
        return _resident(a.shape, lambda i: (0,) * a.ndim)

    def w_in_cols(col, width):
        return _resident((d, width), lambda i: (0, col // width))

    return pl.pallas_call(
        functools.partial(_merge_kernel, final_norm=final_norm, sub_rows=sub_rows),
        grid=(t // tm,),
        in_specs=[
            rows(M_WIDTH), rows(M_WIDTH), rows(M_WIDTH), rows(F_WIDTH), rows(D_MODEL),
            full(nw),
            w_in_cols(COL_ZF, F_WIDTH), w_in_cols(COL_ZM, M_WIDTH), w_in_cols(COL_OM, M_WIDTH),
            w_in_cols(COL_G, 2 * D_MODEL),
            full(hn), full(sk), full(wf), full(wm), full(wo), full(fn),
        ],
        out_specs=rows(D_MODEL),
        out_shape=jax.ShapeDtypeStruct((t, D_MODEL), F32),
        compiler_params=pltpu.CompilerParams(
            dimension_semantics=("parallel",),
            vmem_limit_bytes=VMEM_LIMIT),
        name="merge_out",
    )(hf, hb, xc, yf, x2, nw, w_in_bf, w_in_bf, w_in_bf, w_in_bf, hn, sk, wf, wm, wo, fn)


def _block_diag_tiles(w):
    rows = w.reshape(-1, MXU_DIM, QKV_BLOCK)
    dense = jnp.tile(rows, (1, 1, MXU_DIM // QKV_BLOCK))
    r = lax.broadcasted_iota(jnp.int32, (MXU_DIM, MXU_DIM), 0) // QKV_BLOCK
    c = lax.broadcasted_iota(jnp.int32, (MXU_DIM, MXU_DIM), 1) // QKV_BLOCK
    return jnp.where(r == c, dense, 0.0).astype(BF16)


def kernel(x, norm_w, w_in, w_fourier, conv_w, conv_b, w_q, w_k, w_v,
           w_igate_fwd, b_igate_fwd, w_fgate_fwd, b_fgate_fwd,
           w_igate_bwd, b_igate_bwd, w_fgate_bwd, b_fgate_bwd,
           hnorm_w, skip_w, w_mlstm, w_out, final_norm_w):
    bsz, seq, d = x.shape
    depth = w_in.shape[0]
    assert d == D_MODEL and seq % (FFT_N2 * BF16_SUBLANES * 2) == 0 and seq % MCHUNK == 0
    t = bsz * seq
    nc = seq // MCHUNK
    consts = tuple(jnp.asarray(a, dtype=F32).astype(BF16) for a in _fft_constants(seq))
    fn = final_norm_w.reshape(1, d)

    x2 = x.reshape(t, d)
    for l in range(depth):
        col = lax.broadcasted_iota(jnp.int32, (1, IN_COLS), 1)
        half = ((col >= COL_ZF) & (col < COL_XM)) | (col >= COL_ZM)
        w_in_bf = (w_in[l] * jnp.where(half, 0.5, 1.0)).astype(BF16)
        xf, xm = _inproj(x2, norm_w[l], w_in_bf)

        yt = _fourier_mix(xf, bsz, seq, consts)
        yf = jnp.transpose(yt, (0, 2, 1, 3)).reshape(t, F_WIDTH)

        wg = jnp.concatenate([w_igate_fwd[l], w_fgate_fwd[l], w_igate_bwd[l], w_fgate_bwd[l]], axis=1)
        wg = jnp.pad(wg, ((0, 0), (0, GATE_LANES - wg.shape[1])))
        wg = wg.reshape(3, M_WIDTH // MXU_DIM, MXU_DIM, GATE_LANES)
        qscale = M_HEAD_DIM ** -0.5
        wg = (wg * jnp.asarray([1.0 / qscale, 1.0, 1.0], F32).reshape(3, 1, 1, 1)).astype(BF16)
        bg = jnp.concatenate([b_igate_fwd[l], b_fgate_fwd[l], b_igate_bwd[l], b_fgate_bwd[l]])
        bg = jnp.pad(bg, (0, GATE_LANES - bg.shape[0])).reshape(1, GATE_LANES)
        wk_tiles = _block_diag_tiles(w_k[l])
        q, k, kt, v, xc, gates = _conv_qkv(
            xm, seq, 0.5 * conv_w[l], 0.5 * conv_b[l].reshape(1, M_WIDTH),
            _block_diag_tiles(w_q[l] * qscale), wk_tiles, jnp.swapaxes(wk_tiles, 1, 2),
            _block_diag_tiles(w_v[l]), wg, bg)

        g4 = gates[:, :4 * M_HEADS].reshape(bsz, nc, MCHUNK, 4, M_HEADS)
        g4 = jnp.transpose(g4, (0, 4, 3, 1, 2))
        hf, hb = _mlstm(q, k, kt, v, g4, bsz, seq)

        x2 = _merge(hf, hb, xc, yf, x2, norm_w[l].reshape(1, d), w_in_bf,
                    hnorm_w[l].reshape(1, M_WIDTH), skip_w[l].reshape(1, M_WIDTH),
                    w_fourier[l].astype(BF16), w_mlstm[l].astype(BF16),
                    (0.5 * w_out[l]).astype(BF16),
                    fn, final_norm=(l == depth - 1))
    return x2.reshape(bsz, seq, d)
```

```python
import functools

import numpy as np
import jax
import jax.numpy as jnp
from jax import lax
from jax.experimental import pallas as pl
from jax.experimental.pallas import tpu as pltpu

F32 = jnp.float32
BF16 = jnp.bfloat16

D_MODEL = 1024
F_WIDTH = D_MODEL
F_GROUPS = 4
F_GROUP_DIM = F_WIDTH // F_GROUPS
M_WIDTH = 2 * D_MODEL
M_HEADS = 4
M_HEAD_DIM = M_WIDTH // M_HEADS
QKV_BLOCK = 4
CONV_K = 5
EPS = 1e-6
IN_COLS = 2 * F_WIDTH + 3 * M_WIDTH + 2 * D_MODEL

MCHUNK = 256
MLSTM_HEADS_PER_STEP = 2
MLSTM_CHUNKS_PER_STEP = 2

COL_XF = 0
COL_ZF = F_WIDTH
COL_XM = 2 * F_WIDTH
COL_ZM = COL_XM + M_WIDTH
COL_OM = COL_ZM + M_WIDTH
COL_G = COL_OM + M_WIDTH

LANES = 128
SUBLANES = 8
BF16_SUBLANES = 16
MXU_DIM = 256
GATE_LANES = 128
VMEM_LIMIT = 56 * 1024 * 1024

FFT_N2 = 32


def _dot(a, b):
    return jnp.dot(a, b, preferred_element_type=F32)


def _dot_nt(a, b):
    return lax.dot_general(a, b, (((1,), (1,)), ((), ())), preferred_element_type=F32)


def _sigmoid(x):
    return 0.5 * jnp.tanh(0.5 * x) + 0.5


def _silu(x):
    hx = 0.5 * x
    return hx * jnp.tanh(hx) + hx


def _rmsnorm_bf16(x, w):
    ms = jnp.mean(x * x, axis=-1, keepdims=True)
    return (x * lax.rsqrt(ms + EPS) * w).astype(BF16)


def _resident(shape, index_map):
    return pl.BlockSpec(shape, index_map, pipeline_mode=pl.Buffered(1))


def _inproj_kernel(x_ref, nw_ref, wf_ref, wm_ref, xf_ref, xm_ref, *, sub_rows):
    for r in range(x_ref.shape[0] // sub_rows):
        rows = slice(sub_rows * r, sub_rows * (r + 1))
        h = _rmsnorm_bf16(x_ref[rows, :], nw_ref[...])
        xf_ref[rows, :] = _dot(h, wf_ref[...]).astype(BF16)
        xm_ref[rows, :] = _dot(h, wm_ref[...]).astype(BF16)


def _inproj(x2, norm_w, w_in_bf, tm=512, sub_rows=256):
    t, d = x2.shape
    return pl.pallas_call(
        functools.partial(_inproj_kernel, sub_rows=sub_rows),
        grid=(t // tm,),
        in_specs=[
            pl.BlockSpec((tm, d), lambda i: (i, 0)),
            pl.BlockSpec((1, d), lambda i: (0, 0)),
            _resident((d, F_WIDTH), lambda i: (0, COL_XF // F_WIDTH)),
            _resident((d, M_WIDTH), lambda i: (0, COL_XM // M_WIDTH)),
        ],
        out_specs=[pl.BlockSpec((tm, F_WIDTH), lambda i: (i, 0)),
                   pl.BlockSpec((tm, M_WIDTH), lambda i: (i, 0))],
        out_shape=[jax.ShapeDtypeStruct((t, F_WIDTH), BF16),
                   jax.ShapeDtypeStruct((t, M_WIDTH), BF16)],
        compiler_params=pltpu.CompilerParams(
            dimension_semantics=("parallel",),
            vmem_limit_bytes=VMEM_LIMIT),
        name="inproj",
    )(x2, norm_w, w_in_bf, w_in_bf)


def _fft_constants(seq):
    n2 = FFT_N2
    n1 = seq // n2
    sub = BF16_SUBLANES
    nblk = n1 // sub
    i = np.arange(nblk)[:, None, None, None]
    k2 = np.arange(n2)[None, :, None, None]
    j = np.arange(sub)[None, None, :, None]
    m2 = np.arange(n2)[None, None, None, :]
    ang = 2.0 * np.pi * (((m2 * k2) % n2) / n2 + (((sub * i + j) * k2) % seq) / seq)
    val = np.stack([np.cos(ang), -np.sin(ang)], axis=1)
    wa = np.zeros((nblk, 2, n2, sub, n2, sub), np.float64)
    for jj in range(sub):
        wa[:, :, :, jj, :, jj] = val[:, :, :, jj, :]
    wa = wa.reshape(nblk, 2 * n2 * sub, n2 * sub)
    a = np.arange(n1)
    angb = 2.0 * np.pi * ((a[:, None] * a[None, :]) % n1) / n1
    wr, wi = np.cos(angb), -np.sin(angb)
    wb = np.block([[wr, -wi], [wi, wr]])
    c = np.arange(F_GROUP_DIM)
    angc = 2.0 * np.pi * ((c[:, None] * c[None, :]) % F_GROUP_DIM) / F_GROUP_DIM
    wc = np.concatenate([np.cos(angc), np.sin(angc)], axis=0) / np.sqrt(seq * F_GROUP_DIM)
    return wa, wb, wc


def _fft_a_kernel(x_ref, wa_ref, o_ref, *, nsub):
    n2 = FFT_N2
    sub = BF16_SUBLANES
    c = x_ref.shape[-1]
    for i in range(nsub):
        rows = slice(sub * i, sub * (i + 1))
        xs = x_ref[:, rows, :].reshape(n2 * sub, c)
        z = _dot(wa_ref[i], xs)
        o_ref[:, :, rows, :] = z.astype(BF16).reshape(2, n2, sub, c)


def _fft_b_kernel(z_ref, wb_ref, wc_ref, o_ref):
    nk, n1, c = z_ref.shape[1:]
    for kk in range(nk):
        xs = z_ref[:, kk].reshape(2 * n1, c)
        g = _dot(wb_ref[...], xs).astype(BF16)
        for grp in range(c // F_GROUP_DIM):
            cols = slice(F_GROUP_DIM * grp, F_GROUP_DIM * (grp + 1))
            lhs = jnp.concatenate([g[:n1, cols], g[n1:, cols]], axis=1)
            o_ref[kk, :, cols] = _dot(lhs, wc_ref[...]).astype(BF16)


def _fourier_mix(xf, bsz, seq, consts, nsub=2, kb=2):
    wa, wb, wc = consts
    n2 = FFT_N2
    n1 = seq // n2
    rblk = nsub * BF16_SUBLANES
    proj4 = xf.reshape(bsz, n2, n1, F_WIDTH)
    za = pl.pallas_call(
        functools.partial(_fft_a_kernel, nsub=nsub),
        grid=(n1 // rblk, bsz),
        in_specs=[
            pl.BlockSpec((None, n2, rblk, F_WIDTH), lambda i, b: (b, 0, i, 0)),
            pl.BlockSpec((nsub,) + wa.shape[1:], lambda i, b: (i, 0, 0)),
        ],
        out_specs=pl.BlockSpec((None, 2, n2, rblk, F_WIDTH), lambda i, b: (b, 0, 0, i, 0)),
        out_shape=jax.ShapeDtypeStruct((bsz, 2, n2, n1, F_WIDTH), BF16),
        compiler_params=pltpu.CompilerParams(
            dimension_semantics=("parallel", "parallel"),
            vmem_limit_bytes=VMEM_LIMIT),
        name="fft_stage_a",
    )(proj4, wa)
    yt = pl.pallas_call(
        _fft_b_kernel,
        grid=(bsz, n2 // kb),
        in_specs=[
            pl.BlockSpec((None, 2, kb, n1, F_WIDTH), lambda b, k: (b, 0, k, 0, 0)),
            _resident(wb.shape, lambda b, k: (0, 0)),
            _resident(wc.shape, lambda b, k: (0, 0)),
        ],
        out_specs=pl.BlockSpec((None, kb, n1, F_WIDTH), lambda b, k: (b, k, 0, 0)),
        out_shape=jax.ShapeDtypeStruct((bsz, n2, n1, F_WIDTH), BF16),
        compiler_params=pltpu.CompilerParams(
            dimension_semantics=("parallel", "parallel"),
            vmem_limit_bytes=VMEM_LIMIT),
        name="fft_stage_b",
    )(za, wb, wc)
    return yt


def _log_sigmoid(x):
    return jnp.minimum(x, 0.0) - jnp.log1p(jnp.exp(-jnp.abs(x)))


def _conv_qkv_kernel(prev_ref, cur_ref, next_ref, cw_ref, cb_ref, wq_ref, wk_ref, wkt_ref, wv_ref,
                     wg_ref, bg_ref, q_ref, k_ref, kt_ref, v_ref, xc_ref, g_ref, xs_ref,
                     *, tiles_per_seq):
    tm = cur_ref.shape[0]
    halo = BF16_SUBLANES
    pos = pl.program_id(0) % tiles_per_seq
    keep_prev = (pos != 0).astype(F32)
    keep_next = (pos != tiles_per_seq - 1).astype(F32)
    xs_ref[0:halo, :] = prev_ref[...].astype(F32) * keep_prev
    xs_ref[halo:halo + tm, :] = cur_ref[...].astype(F32)
    xs_ref[halo + tm:, :] = next_ref[...].astype(F32) * keep_next

    gacc = jnp.zeros((tm, GATE_LANES), F32)
    for g in range(M_WIDTH // MXU_DIM):
        cols = slice(MXU_DIM * g, MXU_DIM * (g + 1))
        acc = jnp.broadcast_to(cb_ref[:, cols], (tm, MXU_DIM))
        ext = xs_ref[halo - SUBLANES:halo + tm + SUBLANES, cols]
        for j in range(CONV_K):
            shift = (CONV_K // 2 - j) % ext.shape[0]
            tap = ext if shift == 0 else pltpu.roll(ext, shift, axis=0)
            acc = acc + cw_ref[j:j + 1, cols] * tap[SUBLANES:SUBLANES + tm]
        xcb = (acc * (1.0 + jnp.tanh(acc))).astype(BF16)
        q = _dot(xcb, wq_ref[g])
        k = _dot(xcb, wk_ref[g])
        kt_ref[cols, :] = _dot_nt(wkt_ref[g], xcb).astype(BF16)
        v = _dot(cur_ref[:, cols], wv_ref[g])
        qb, kb, vb = q.astype(BF16), k.astype(BF16), v.astype(BF16)
        gacc = gacc + _dot(qb, wg_ref[0, g]) + _dot(kb, wg_ref[1, g]) + _dot(vb, wg_ref[2, g])
        q_ref[:, cols] = qb
        k_ref[:, cols] = kb
        v_ref[:, cols] = vb
        xc_ref[:, cols] = xcb
    gates = gacc + bg_ref[...]
    lane = lax.broadcasted_iota(jnp.int32, gates.shape, 1)
    is_forget = (lane // M_HEADS) % 2 == 1
    g_ref[...] = jnp.where(is_forget, _log_sigmoid(gates), gates)


def _conv_qkv(xm, seq, cw, cb, wq, wk, wkt, wv, wg, bg, tm=256):
    t = xm.shape[0]
    halo = BF16_SUBLANES
    hb = tm // halo
    nhalo = t // halo
    out_bf = jax.ShapeDtypeStruct((t, M_WIDTH), BF16)
    row_spec = pl.BlockSpec((tm, M_WIDTH), lambda i: (i, 0))

    def full(a):
        return _resident(a.shape, lambda i: (0,) * a.ndim)

    return pl.pallas_call(
        functools.partial(_conv_qkv_kernel, tiles_per_seq=seq // tm),
        grid=(t // tm,),
        in_specs=[
            pl.BlockSpec((halo, M_WIDTH), lambda i: (jnp.maximum(i * hb - 1, 0), 0)),
            row_spec,
            pl.BlockSpec((halo, M_WIDTH), lambda i: (jnp.minimum((i + 1) * hb, nhalo - 1), 0)),
            full(cw), full(cb), full(wq), full(wk), full(wkt), full(wv), full(wg), full(bg),
        ],
        out_specs=[row_spec, row_spec, pl.BlockSpec((M_WIDTH, tm), lambda i: (0, i)),
                   row_spec, row_spec,
                   pl.BlockSpec((tm, GATE_LANES), lambda i: (i, 0))],
        out_shape=[out_bf, out_bf, jax.ShapeDtypeStruct((M_WIDTH, t), BF16), out_bf, out_bf,
                   jax.ShapeDtypeStruct((t, GATE_LANES), F32)],
        scratch_shapes=[pltpu.VMEM((tm + 2 * halo, M_WIDTH), F32)],
        compiler_params=pltpu.CompilerParams(
            dimension_semantics=("parallel",),
            vmem_limit_bytes=VMEM_LIMIT),
        name="conv_qkv_gates",
    )(xm, xm, xm, cw, cb, wq, wk, wkt, wv, wg, bg)


GS_C, GS_M, GS_INTER, GS_ENEGM, GS_WS, GS_DECAY, GS_ROWS = 0, 1, 2, 3, 4, 5, 6


def _lane_scan(x, op, fill, reverse):
    n = x.shape[1]
    lane = lax.broadcasted_iota(jnp.int32, x.shape, 1)
    sh = 1
    while sh < n:
        if reverse:
            x = op(x, jnp.where(lane < n - sh, pltpu.roll(x, n - sh, axis=1), fill))
        else:
            x = op(x, jnp.where(lane >= sh, pltpu.roll(x, sh, axis=1), fill))
        sh *= 2
    return x


def _gate_prep(i_pre, log_f, gs_ref, tmp_ref, *, reverse):
    nc, L = i_pre.shape
    b = _lane_scan(log_f, jnp.add, 0.0, reverse)
    c = i_pre - b
    cm = _lane_scan(c, jnp.maximum, -jnp.inf, reverse)
    end = slice(0, 1) if reverse else slice(L - 1, L)
    g = b[:, end]
    cmt = cm[:, end]
    tmp_ref[0] = jnp.broadcast_to(g, (nc, LANES))
    tmp_ref[1] = jnp.broadcast_to(g + cmt, (nc, LANES))

    def body(step, m):
        idx = nc - 1 - step if reverse else step
        tmp_ref[2, pl.ds(idx, 1), :] = m
        return jnp.maximum(m + tmp_ref[0, pl.ds(idx, 1), :], tmp_ref[1, pl.ds(idx, 1), :])

    lax.fori_loop(0, nc, body, jnp.zeros((1, LANES), F32))
    m_prev = tmp_ref[2][:, 0:1]
    big_m = jnp.maximum(m_prev, cm)
    mx = jnp.maximum(m_prev, cmt)
    gs_ref[GS_C] = c
    gs_ref[GS_M] = big_m
    gs_ref[GS_INTER] = jnp.exp(m_prev - big_m)
    gs_ref[GS_ENEGM] = jnp.exp(-b - big_m)
    gs_ref[GS_WS] = jnp.exp(c - mx)
    gs_ref[GS_DECAY] = jnp.broadcast_to(jnp.exp(m_prev - mx), (nc, L))


def _mlstm_chunk(q, k, kt, v, gs_ref, ci, ct_ref, n_ref, *, reverse):
    L = q.shape[0]

    def row(r):
        return gs_ref[r, pl.ds(ci, 1), :]

    c_row = row(GS_C)
    sub = lax.broadcasted_iota(jnp.int32, (SUBLANES, L), 0)
    packed = jnp.where(sub == 0, row(GS_M),
                       jnp.where(sub == 1, row(GS_INTER),
                                 jnp.where(sub == 2, row(GS_ENEGM), row(GS_WS))))
    tr = jnp.concatenate([packed, jnp.zeros((LANES - SUBLANES, L), F32)], axis=0).T
    m_col, inter, enegm, ws = tr[:, 0:1], tr[:, 1:2], tr[:, 2:3], tr[:, 3:4]
    decay = row(GS_DECAY)[:, 0:1]

    r_i = lax.broadcasted_iota(jnp.int32, (L, L), 0)
    c_i = lax.broadcasted_iota(jnp.int32, (L, L), 1)
    mask = (c_i >= r_i) if reverse else (c_i <= r_i)
    p = jnp.exp(jnp.where(mask, c_row - m_col, -jnp.inf))
    s = _dot_nt(q, k) * p
    qi = inter * q.astype(F32)
    num = _dot(s.astype(BF16), v) + _dot(qi.astype(BF16), ct_ref[...].astype(BF16))
    den = jnp.sum(s, axis=1, keepdims=True) + jnp.sum(qi * n_ref[...], axis=1, keepdims=True)
    h = num * (1.0 / jnp.maximum(jnp.abs(den), enegm))

    vw = ws.astype(BF16) * v
    ct_ref[...] = decay * ct_ref[...] + _dot(kt, vw)
    n_ref[...] = decay * n_ref[...] + jnp.sum(ws * k.astype(F32), axis=0, keepdims=True)
    return h


def _mlstm_kernel(gates_ref, qf_ref, kf_ref, ktf_ref, vf_ref, qb_ref, kb_ref, ktb_ref, vb_ref,
                  hf_ref, hb_ref,
                  ct_ref, n_ref, gs_ref, tmp_ref, *, hp, nc, cps):
    c = pl.program_id(2)
    dh = M_HEAD_DIM
    L = MCHUNK

    @pl.when(c == 0)
    def _():
        ct_ref[...] = jnp.zeros_like(ct_ref)
        n_ref[...] = jnp.zeros_like(n_ref)
        for hh in range(hp):
            _gate_prep(gates_ref[hh, 0], gates_ref[hh, 1], gs_ref.at[hh, 0], tmp_ref, reverse=False)
            _gate_prep(gates_ref[hh, 2], gates_ref[hh, 3], gs_ref.at[hh, 1], tmp_ref, reverse=True)

    for sub in range(cps):
        rf = slice(L * sub, L * (sub + 1))
        rb = slice(L * (cps - 1 - sub), L * (cps - sub))
        for hh in range(hp):
            cols = slice(dh * hh, dh * (hh + 1))
            hf = _mlstm_chunk(qf_ref[rf, cols], kf_ref[rf, cols], ktf_ref[cols, rf],
                              vf_ref[rf, cols], gs_ref.at[hh, 0], cps * c + sub,
                              ct_ref.at[hh, 0], n_ref.at[hh, 0], reverse=False)
            hf_ref[rf, cols] = hf.astype(hf_ref.dtype)
            hb = _mlstm_chunk(qb_ref[rb, cols], kb_ref[rb, cols], ktb_ref[cols, rb],
                              vb_ref[rb, cols], gs_ref.at[hh, 1], nc - 1 - (cps * c + sub),
                              ct_ref.at[hh, 1], n_ref.at[hh, 1], reverse=True)
            hb_ref[rb, cols] = hb.astype(hb_ref.dtype)


def _mlstm(q, k, kt, v, gates4, bsz, seq, hp=MLSTM_HEADS_PER_STEP, cps=MLSTM_CHUNKS_PER_STEP):
    t = q.shape[0]
    L = MCHUNK
    nc = seq // L
    dh = M_HEAD_DIM
    width = hp * dh
    rows = cps * L
    ns = nc // cps

    qkv_f = pl.BlockSpec((rows, width), lambda b, h, c: (b * ns + c, h))
    qkv_b = pl.BlockSpec((rows, width), lambda b, h, c: (b * ns + ns - 1 - c, h))
    kt_f = pl.BlockSpec((width, rows), lambda b, h, c: (h, b * ns + c))
    kt_b = pl.BlockSpec((width, rows), lambda b, h, c: (h, b * ns + ns - 1 - c))
    g_spec = pl.BlockSpec((None, hp, 4, nc, L), lambda b, h, c: (b, h, 0, 0, 0))
    out = jax.ShapeDtypeStruct((t, M_WIDTH), BF16)
    return pl.pallas_call(
        functools.partial(_mlstm_kernel, hp=hp, nc=nc, cps=cps),
        grid=(bsz, M_HEADS // hp, ns),
        in_specs=[g_spec, qkv_f, qkv_f, kt_f, qkv_f, qkv_b, qkv_b, kt_b, qkv_b],
        out_specs=[qkv_f, qkv_b],
        out_shape=[out, out],
        scratch_shapes=[
            pltpu.VMEM((hp, 2, dh, dh), F32),
            pltpu.VMEM((hp, 2, 1, dh), F32),
            pltpu.VMEM((hp, 2, GS_ROWS, nc, L), F32),
            pltpu.VMEM((3, nc, LANES), F32),
        ],
        compiler_params=pltpu.CompilerParams(
            dimension_semantics=("parallel", "parallel", "arbitrary"),
            vmem_limit_bytes=VMEM_LIMIT),
        name="mlstm",
    )(gates4, q, k, kt, v, q, k, kt, v)


def _merge_kernel(hf_ref, hb_ref, xc_ref, yf_ref, x_ref, nw_ref, wzf_ref, wzm_ref, wom_ref, wg_ref,
                  hn_ref, sk_ref, wf_ref, wm_ref, wo_ref, fn_ref, o_ref, *, final_norm, sub_rows):
    for r in range(o_ref.shape[0] // sub_rows):
        rows = slice(sub_rows * r, sub_rows * (r + 1))
        hin = _rmsnorm_bf16(x_ref[rows, :], nw_ref[...])
        y_b = None
        for h in range(M_HEADS):
            cols = slice(M_HEAD_DIM * h, M_HEAD_DIM * (h + 1))
            hc2 = (hf_ref[rows, cols].astype(F32) + hb_ref[rows, cols].astype(F32))
            hc2 = hc2 * (1.0 + jnp.tanh(_dot(hin, wom_ref[:, cols])))
            hc = hc2 * lax.rsqrt(jnp.mean(hc2 * hc2, axis=-1, keepdims=True) + 4.0 * EPS)
            u = hc * hn_ref[:, cols] + sk_ref[:, cols] * xc_ref[rows, cols].astype(F32)
            hz = _dot(hin, wzm_ref[:, cols])
            u = u * (hz * (1.0 + jnp.tanh(hz)))
            part = _dot(u.astype(BF16), wm_ref[cols, :])
            y_b = part if y_b is None else y_b + part
        hzf = _dot(hin, wzf_ref[...])
        ya_in = yf_ref[rows, :].astype(F32) * (hzf * (1.0 + jnp.tanh(hzf)))
        y_a = _dot(ya_in.astype(BF16), wf_ref[...])
        t_a = jnp.tanh(_dot(hin, wg_ref[:, :D_MODEL]))
        t_b = jnp.tanh(_dot(hin, wg_ref[:, D_MODEL:]))
        merged2 = (y_a + y_b) + (t_a * y_a + t_b * y_b)
        xn = x_ref[rows, :] + _dot(merged2.astype(BF16), wo_ref[...])
        if final_norm:
            xn = xn * lax.rsqrt(jnp.mean(xn * xn, axis=-1, keepdims=True) + EPS) * fn_ref[...]
        o_ref[rows, :] = xn


def _merge(hf, hb, xc, yf, x2, nw, w_in_bf, hn, sk, wf, wm, wo, fn, final_norm,
           tm=512, sub_rows=256):
    t, d = x2.shape

    def rows(width):
        return pl.BlockSpec((tm, width), lambda i: (i, 0))

    def full(a):
        return _resident(a.shape, lambda i: (0,) * a.ndim)

    def w_in_cols(col, width):
        return _resident((d, width), lambda i: (0, col // width))

    return pl.pallas_call(
        functools.partial(_merge_kernel, final_norm=final_norm, sub_rows=sub_rows),
        grid=(t // tm,),
        in_specs=[
            rows(M_WIDTH), rows(M_WIDTH), rows(M_WIDTH), rows(F_WIDTH), rows(D_MODEL),
            full(nw),
            w_in_cols(COL_ZF, F_WIDTH), w_in_cols(COL_ZM, M_WIDTH), w_in_cols(COL_OM, M_WIDTH),
            w_in_cols(COL_G, 2 * D_MODEL),
            full(hn), full(sk), full(wf), full(wm), full(wo), full(fn),
        ],
        out_specs=rows(D_MODEL),
        out_shape=jax.ShapeDtypeStruct((t, D_MODEL), F32),
        compiler_params=pltpu.CompilerParams(
            dimension_semantics=("parallel",),
            vmem_limit_bytes=VMEM_LIMIT),
        name="merge_out",
    )(hf, hb, xc, yf, x2, nw, w_in_bf, w_in_bf, w_in_bf, w_in_bf, hn, sk, wf, wm, wo, fn)


def _block_diag_tiles(w):
    rows = w.reshape(-1, MXU_DIM, QKV_BLOCK)
    dense = jnp.tile(rows, (1, 1, MXU_DIM // QKV_BLOCK))
    r = lax.broadcasted_iota(jnp.int32, (MXU_DIM, MXU_DIM), 0) // QKV_BLOCK
    c = lax.broadcasted_iota(jnp.int32, (MXU_DIM, MXU_DIM), 1) // QKV_BLOCK
    return jnp.where(r == c, dense, 0.0).astype(BF16)


def kernel(x, norm_w, w_in, w_fourier, conv_w, conv_b, w_q, w_k, w_v,
           w_igate_fwd, b_igate_fwd, w_fgate_fwd, b_fgate_fwd,
           w_igate_bwd, b_igate_bwd, w_fgate_bwd, b_fgate_bwd,
           hnorm_w, skip_w, w_mlstm, w_out, final_norm_w):
    bsz, seq, d = x.shape
    depth = w_in.shape[0]
    assert d == D_MODEL and seq % (FFT_N2 * BF16_SUBLANES * 2) == 0 and seq % MCHUNK == 0
    t = bsz * seq
    nc = seq // MCHUNK
    consts = tuple(jnp.asarray(a, dtype=F32).astype(BF16) for a in _fft_constants(seq))
    fn = final_norm_w.reshape(1, d)

    x2 = x.reshape(t, d)
    for l in range(depth):
        col = lax.broadcasted_iota(jnp.int32, (1, IN_COLS), 1)
        half = ((col >= COL_ZF) & (col < COL_XM)) | (col >= COL_ZM)
        w_in_bf = (w_in[l] * jnp.where(half, 0.5, 1.0)).astype(BF16)
        nw = norm_w[l].reshape(1, d)

        wg = jnp.concatenate([w_igate_fwd[l], w_fgate_fwd[l], w_igate_bwd[l], w_fgate_bwd[l]], axis=1)
        wg = jnp.pad(wg, ((0, 0), (0, GATE_LANES - wg.shape[1])))
        wg = wg.reshape(3, M_WIDTH // MXU_DIM, MXU_DIM, GATE_LANES)
        qscale = M_HEAD_DIM ** -0.5
        wg = (wg * jnp.asarray([1.0 / qscale, 1.0, 1.0], F32).reshape(3, 1, 1, 1)).astype(BF16)
        bg = jnp.concatenate([b_igate_fwd[l], b_fgate_fwd[l], b_igate_bwd[l], b_fgate_bwd[l]])
        bg = jnp.pad(bg, (0, GATE_LANES - bg.shape[0])).reshape(1, GATE_LANES)
        wk_tiles = _block_diag_tiles(w_k[l])
        xf, xm = _inproj(x2, nw, w_in_bf)
        q, k, kt, v, xc, gates = _conv_qkv(
            xm, seq, 0.5 * conv_w[l], 0.5 * conv_b[l].reshape(1, M_WIDTH),
            _block_diag_tiles(w_q[l] * qscale), wk_tiles, jnp.swapaxes(wk_tiles, 1, 2),
            _block_diag_tiles(w_v[l]), wg, bg)

        yt = _fourier_mix(xf, bsz, seq, consts)
        yf = jnp.transpose(yt, (0, 2, 1, 3)).reshape(t, F_WIDTH)

        g4 = gates[:, :4 * M_HEADS].reshape(bsz, nc, MCHUNK, 4, M_HEADS)
        g4 = jnp.transpose(g4, (0, 4, 3, 1, 2))
        hf, hb = _mlstm(q, k, kt, v, g4, bsz, seq)

        x2 = _merge(hf, hb, xc, yf, x2, nw, w_in_bf,
                    hnorm_w[l].reshape(1, M_WIDTH), skip_w[l].reshape(1, M_WIDTH),
                    w_fourier[l].astype(BF16), w_mlstm[l].astype(BF16),
                    (0.5 * w_out[l]).astype(BF16),
                    fn, final_norm=(l == depth - 1))
    return x2.reshape(bsz, seq, d)
```

```python
import functools

import numpy as np
import jax
import jax.numpy as jnp
from jax import lax
from jax.experimental import pallas as pl
from jax.experimental.pallas import tpu as pltpu

F32 = jnp.float32
BF16 = jnp.bfloat16

D_MODEL = 1024
F_WIDTH = D_MODEL
F_GROUPS = 4
F_GROUP_DIM = F_WIDTH // F_GROUPS
M_WIDTH = 2 * D_MODEL
M_HEADS = 4
M_HEAD_DIM = M_WIDTH // M_HEADS
QKV_BLOCK = 4
CONV_K = 5
EPS = 1e-6
IN_COLS = 2 * F_WIDTH + 3 * M_WIDTH + 2 * D_MODEL

MCHUNK = 256
MLSTM_HEADS_PER_STEP = 2
MLSTM_CHUNKS_PER_STEP = 2

COL_XF = 0
COL_ZF = F_WIDTH
COL_XM = 2 * F_WIDTH
COL_ZM = COL_XM + M_WIDTH
COL_OM = COL_ZM + M_WIDTH
COL_G = COL_OM + M_WIDTH

LANES = 128
SUBLANES = 8
BF16_SUBLANES = 16
MXU_DIM = 256
GATE_LANES = 128
VMEM_LIMIT = 56 * 1024 * 1024

FFT_N2 = 32


def _dot(a, b):
    return jnp.dot(a, b, preferred_element_type=F32)


def _dot_nt(a, b):
    return lax.dot_general(a, b, (((1,), (1,)), ((), ())), preferred_element_type=F32)


def _sigmoid(x):
    return 0.5 * jnp.tanh(0.5 * x) + 0.5


def _silu(x):
    hx = 0.5 * x
    return hx * jnp.tanh(hx) + hx


def _rmsnorm_bf16(x, w):
    ms = jnp.mean(x * x, axis=-1, keepdims=True)
    return (x * lax.rsqrt(ms + EPS) * w).astype(BF16)


def _resident(shape, index_map):
    return pl.BlockSpec(shape, index_map, pipeline_mode=pl.Buffered(1))


def _inproj_kernel(x_ref, nw_ref, wf_ref, wm_ref, xf_ref, xm_ref, *, sub_rows):
    for r in range(x_ref.shape[0] // sub_rows):
        rows = slice(sub_rows * r, sub_rows * (r + 1))
        h = _rmsnorm_bf16(x_ref[rows, :], nw_ref[...])
        xf_ref[rows, :] = _dot(h, wf_ref[...]).astype(BF16)
        xm_ref[rows, :] = _dot(h, wm_ref[...]).astype(BF16)


def _inproj(x2, norm_w, w_in_bf, tm=512, sub_rows=256):
    t, d = x2.shape
    return pl.pallas_call(
        functools.partial(_inproj_kernel, sub_rows=sub_rows),
        grid=(t // tm,),
        in_specs=[
            pl.BlockSpec((tm, d), lambda i: (i, 0)),
            pl.BlockSpec((1, d), lambda i: (0, 0)),
            _resident((d, F_WIDTH), lambda i: (0, COL_XF // F_WIDTH)),
            _resident((d, M_WIDTH), lambda i: (0, COL_XM // M_WIDTH)),
        ],
        out_specs=[pl.BlockSpec((tm, F_WIDTH), lambda i: (i, 0)),
                   pl.BlockSpec((tm, M_WIDTH), lambda i: (i, 0))],
        out_shape=[jax.ShapeDtypeStruct((t, F_WIDTH), BF16),
                   jax.ShapeDtypeStruct((t, M_WIDTH), BF16)],
        compiler_params=pltpu.CompilerParams(
            dimension_semantics=("parallel",),
            vmem_limit_bytes=VMEM_LIMIT),
        name="inproj",
    )(x2, norm_w, w_in_bf, w_in_bf)


def _fft_constants(seq):
    n2 = FFT_N2
    n1 = seq // n2
    sub = BF16_SUBLANES
    nblk = n1 // sub
    i = np.arange(nblk)[:, None, None, None]
    k2 = np.arange(n2)[None, :, None, None]
    j = np.arange(sub)[None, None, :, None]
    m2 = np.arange(n2)[None, None, None, :]
    ang = 2.0 * np.pi * (((m2 * k2) % n2) / n2 + (((sub * i + j) * k2) % seq) / seq)
    val = np.stack([np.cos(ang), -np.sin(ang)], axis=1)
    wa = np.zeros((nblk, 2, n2, sub, n2, sub), np.float64)
    for jj in range(sub):
        wa[:, :, :, jj, :, jj] = val[:, :, :, jj, :]
    wa = wa.reshape(nblk, 2 * n2 * sub, n2 * sub)
    a = np.arange(n1)
    angb = 2.0 * np.pi * ((a[:, None] * a[None, :]) % n1) / n1
    wr, wi = np.cos(angb), -np.sin(angb)
    wb = np.block([[wr, -wi], [wi, wr]])
    c = np.arange(F_GROUP_DIM)
    angc = 2.0 * np.pi * ((c[:, None] * c[None, :]) % F_GROUP_DIM) / F_GROUP_DIM
    wc = np.concatenate([np.cos(angc), np.sin(angc)], axis=0) / np.sqrt(seq * F_GROUP_DIM)
    return wa, wb, wc


def _fft_a_kernel(x_ref, wa_ref, o_ref, *, nsub):
    n2 = FFT_N2
    sub = BF16_SUBLANES
    c = x_ref.shape[-1]
    for i in range(nsub):
        rows = slice(sub * i, sub * (i + 1))
        xs = x_ref[:, rows, :].reshape(n2 * sub, c)
        z = _dot(wa_ref[i], xs)
        o_ref[:, :, rows, :] = z.astype(BF16).reshape(2, n2, sub, c)


def _fft_b_kernel(z_ref, wb_ref, wc_ref, o_ref):
    nk, n1, c = z_ref.shape[1:]
    for kk in range(nk):
        xs = z_ref[:, kk].reshape(2 * n1, c)
        g = _dot(wb_ref[...], xs).astype(BF16)
        for grp in range(c // F_GROUP_DIM):
            cols = slice(F_GROUP_DIM * grp, F_GROUP_DIM * (grp + 1))
            lhs = jnp.concatenate([g[:n1, cols], g[n1:, cols]], axis=1)
            o_ref[kk, :, cols] = _dot(lhs, wc_ref[...]).astype(BF16)


def _fourier_mix(xf, bsz, seq, consts, nsub=4, kb=4):
    wa, wb, wc = consts
    n2 = FFT_N2
    n1 = seq // n2
    rblk = nsub * BF16_SUBLANES
    proj4 = xf.reshape(bsz, n2, n1, F_WIDTH)
    za = pl.pallas_call(
        functools.partial(_fft_a_kernel, nsub=nsub),
        grid=(n1 // rblk, bsz),
        in_specs=[
            pl.BlockSpec((None, n2, rblk, F_WIDTH), lambda i, b: (b, 0, i, 0)),
            pl.BlockSpec((nsub,) + wa.shape[1:], lambda i, b: (i, 0, 0)),
        ],
        out_specs=pl.BlockSpec((None, 2, n2, rblk, F_WIDTH), lambda i, b: (b, 0, 0, i, 0)),
        out_shape=jax.ShapeDtypeStruct((bsz, 2, n2, n1, F_WIDTH), BF16),
        compiler_params=pltpu.CompilerParams(
            dimension_semantics=("parallel", "parallel"),
            vmem_limit_bytes=VMEM_LIMIT),
        name="fft_stage_a",
    )(proj4, wa)
    yt = pl.pallas_call(
        _fft_b_kernel,
        grid=(bsz, n2 // kb),
        in_specs=[
            pl.BlockSpec((None, 2, kb, n1, F_WIDTH), lambda b, k: (b, 0, k, 0, 0)),
            _resident(wb.shape, lambda b, k: (0, 0)),
            _resident(wc.shape, lambda b, k: (0, 0)),
        ],
        out_specs=pl.BlockSpec((None, kb, n1, F_WIDTH), lambda b, k: (b, k, 0, 0)),
        out_shape=jax.ShapeDtypeStruct((bsz, n2, n1, F_WIDTH), BF16),
        compiler_params=pltpu.CompilerParams(
            dimension_semantics=("parallel", "parallel"),
            vmem_limit_bytes=VMEM_LIMIT),
        name="fft_stage_b",
    )(za, wb, wc)
    return yt


def _log_sigmoid(x):
    return jnp.minimum(x, 0.0) - jnp.log1p(jnp.exp(-jnp.abs(x)))


def _conv_qkv_kernel(prev_ref, cur_ref, next_ref, cw_ref, cb_ref, wq_ref, wk_ref, wkt_ref, wv_ref,
                     wg_ref, bg_ref, q_ref, k_ref, kt_ref, v_ref, xc_ref, g_ref, xs_ref,
                     *, tiles_per_seq):
    tm = cur_ref.shape[0]
    halo = BF16_SUBLANES
    pos = pl.program_id(0) % tiles_per_seq
    keep_prev = (pos != 0).astype(F32)
    keep_next = (pos != tiles_per_seq - 1).astype(F32)
    xs_ref[0:halo, :] = prev_ref[...].astype(F32) * keep_prev
    xs_ref[halo:halo + tm, :] = cur_ref[...].astype(F32)
    xs_ref[halo + tm:, :] = next_ref[...].astype(F32) * keep_next

    gacc = jnp.zeros((tm, GATE_LANES), F32)
    for g in range(M_WIDTH // MXU_DIM):
        cols = slice(MXU_DIM * g, MXU_DIM * (g + 1))
        acc = jnp.broadcast_to(cb_ref[:, cols], (tm, MXU_DIM))
        ext = xs_ref[halo - SUBLANES:halo + tm + SUBLANES, cols]
        for j in range(CONV_K):
            shift = (CONV_K // 2 - j) % ext.shape[0]
            tap = ext if shift == 0 else pltpu.roll(ext, shift, axis=0)
            acc = acc + cw_ref[j:j + 1, cols] * tap[SUBLANES:SUBLANES + tm]
        xcb = (acc * (1.0 + jnp.tanh(acc))).astype(BF16)
        q = _dot(xcb, wq_ref[g])
        k = _dot(xcb, wk_ref[g])
        kt_ref[cols, :] = _dot_nt(wkt_ref[g], xcb).astype(BF16)
        v = _dot(cur_ref[:, cols], wv_ref[g])
        qb, kb, vb = q.astype(BF16), k.astype(BF16), v.astype(BF16)
        gacc = gacc + _dot(qb, wg_ref[0, g]) + _dot(kb, wg_ref[1, g]) + _dot(vb, wg_ref[2, g])
        q_ref[:, cols] = qb
        k_ref[:, cols] = kb
        v_ref[:, cols] = vb
        xc_ref[:, cols] = xcb
    gates = gacc + bg_ref[...]
    lane = lax.broadcasted_iota(jnp.int32, gates.shape, 1)
    is_forget = (lane // M_HEADS) % 2 == 1
    g_ref[...] = jnp.where(is_forget, _log_sigmoid(gates), gates)


def _conv_qkv(xm, seq, cw, cb, wq, wk, wkt, wv, wg, bg, tm=512):
    t = xm.shape[0]
    halo = BF16_SUBLANES
    hb = tm // halo
    nhalo = t // halo
    out_bf = jax.ShapeDtypeStruct((t, M_WIDTH), BF16)
    row_spec = pl.BlockSpec((tm, M_WIDTH), lambda i: (i, 0))

    def full(a):
        return _resident(a.shape, lambda i: (0,) * a.ndim)

    return pl.pallas_call(
        functools.partial(_conv_qkv_kernel, tiles_per_seq=seq // tm),
        grid=(t // tm,),
        in_specs=[
            pl.BlockSpec((halo, M_WIDTH), lambda i: (jnp.maximum(i * hb - 1, 0), 0)),
            row_spec,
            pl.BlockSpec((halo, M_WIDTH), lambda i: (jnp.minimum((i + 1) * hb, nhalo - 1), 0)),
            full(cw), full(cb), full(wq), full(wk), full(wkt), full(wv), full(wg), full(bg),
        ],
        out_specs=[row_spec, row_spec, pl.BlockSpec((M_WIDTH, tm), lambda i: (0, i)),
                   row_spec, row_spec,
                   pl.BlockSpec((tm, GATE_LANES), lambda i: (i, 0))],
        out_shape=[out_bf, out_bf, jax.ShapeDtypeStruct((M_WIDTH, t), BF16), out_bf, out_bf,
                   jax.ShapeDtypeStruct((t, GATE_LANES), F32)],
        scratch_shapes=[pltpu.VMEM((tm + 2 * halo, M_WIDTH), F32)],
        compiler_params=pltpu.CompilerParams(
            dimension_semantics=("parallel",),
            vmem_limit_bytes=VMEM_LIMIT),
        name="conv_qkv_gates",
    )(xm, xm, xm, cw, cb, wq, wk, wkt, wv, wg, bg)


GS_C, GS_M, GS_INTER, GS_ENEGM, GS_WS, GS_DECAY, GS_ROWS = 0, 1, 2, 3, 4, 5, 6


def _lane_scan(x, op, fill, reverse):
    n = x.shape[1]
    lane = lax.broadcasted_iota(jnp.int32, x.shape, 1)
    sh = 1
    while sh < n:
        if reverse:
            x = op(x, jnp.where(lane < n - sh, pltpu.roll(x, n - sh, axis=1), fill))
        else:
            x = op(x, jnp.where(lane >= sh, pltpu.roll(x, sh, axis=1), fill))
        sh *= 2
    return x


def _gate_prep(i_pre, log_f, gs_ref, tmp_ref, *, reverse):
    nc, L = i_pre.shape
    b = _lane_scan(log_f, jnp.add, 0.0, reverse)
    c = i_pre - b
    cm = _lane_scan(c, jnp.maximum, -jnp.inf, reverse)
    end = slice(0, 1) if reverse else slice(L - 1, L)
    g = b[:, end]
    cmt = cm[:, end]
    tmp_ref[0] = jnp.broadcast_to(g, (nc, LANES))
    tmp_ref[1] = jnp.broadcast_to(g + cmt, (nc, LANES))

    def body(step, m):
        idx = nc - 1 - step if reverse else step
        tmp_ref[2, pl.ds(idx, 1), :] = m
        return jnp.maximum(m + tmp_ref[0, pl.ds(idx, 1), :], tmp_ref[1, pl.ds(idx, 1), :])

    lax.fori_loop(0, nc, body, jnp.zeros((1, LANES), F32))
    m_prev = tmp_ref[2][:, 0:1]
    big_m = jnp.maximum(m_prev, cm)
    mx = jnp.maximum(m_prev, cmt)
    gs_ref[GS_C] = c
    gs_ref[GS_M] = big_m
    gs_ref[GS_INTER] = jnp.exp(m_prev - big_m)
    gs_ref[GS_ENEGM] = jnp.exp(-b - big_m)
    gs_ref[GS_WS] = jnp.exp(c - mx)
    gs_ref[GS_DECAY] = jnp.broadcast_to(jnp.exp(m_prev - mx), (nc, L))


def _mlstm_chunk(q, k, kt, v, gs_ref, ci, ct_ref, n_ref, *, reverse):
    L = q.shape[0]

    def row(r):
        return gs_ref[r, pl.ds(ci, 1), :]

    c_row = row(GS_C)
    sub = lax.broadcasted_iota(jnp.int32, (SUBLANES, L), 0)
    packed = jnp.where(sub == 0, row(GS_M),
                       jnp.where(sub == 1, row(GS_INTER),
                                 jnp.where(sub == 2, row(GS_ENEGM), row(GS_WS))))
    tr = jnp.concatenate([packed, jnp.zeros((LANES - SUBLANES, L), F32)], axis=0).T
    m_col, inter, enegm, ws = tr[:, 0:1], tr[:, 1:2], tr[:, 2:3], tr[:, 3:4]
    decay = row(GS_DECAY)[:, 0:1]

    r_i = lax.broadcasted_iota(jnp.int32, (L, L), 0)
    c_i = lax.broadcasted_iota(jnp.int32, (L, L), 1)
    mask = (c_i >= r_i) if reverse else (c_i <= r_i)
    p = jnp.exp(jnp.where(mask, c_row - m_col, -jnp.inf))
    s = _dot_nt(q, k) * p
    qi = inter * q.astype(F32)
    num = _dot(s.astype(BF16), v) + _dot(qi.astype(BF16), ct_ref[...].astype(BF16))
    den = jnp.sum(s, axis=1, keepdims=True) + jnp.sum(qi * n_ref[...], axis=1, keepdims=True)
    h = num * (1.0 / jnp.maximum(jnp.abs(den), enegm))

    vw = ws.astype(BF16) * v
    ct_ref[...] = decay * ct_ref[...] + _dot(kt, vw)
    n_ref[...] = decay * n_ref[...] + jnp.sum(ws * k.astype(F32), axis=0, keepdims=True)
    return h


def _mlstm_kernel(gates_ref, qf_ref, kf_ref, ktf_ref, vf_ref, qb_ref, kb_ref, ktb_ref, vb_ref,
                  hf_ref, hb_ref,
                  ct_ref, n_ref, gs_ref, tmp_ref, *, hp, nc, cps):
    c = pl.program_id(2)
    dh = M_HEAD_DIM
    L = MCHUNK

    @pl.when(c == 0)
    def _():
        ct_ref[...] = jnp.zeros_like(ct_ref)
        n_ref[...] = jnp.zeros_like(n_ref)
        for hh in range(hp):
            _gate_prep(gates_ref[hh, 0], gates_ref[hh, 1], gs_ref.at[hh, 0], tmp_ref, reverse=False)
            _gate_prep(gates_ref[hh, 2], gates_ref[hh, 3], gs_ref.at[hh, 1], tmp_ref, reverse=True)

    for sub in range(cps):
        rf = slice(L * sub, L * (sub + 1))
        rb = slice(L * (cps - 1 - sub), L * (cps - sub))
        for hh in range(hp):
            cols = slice(dh * hh, dh * (hh + 1))
            hf = _mlstm_chunk(qf_ref[rf, cols], kf_ref[rf, cols], ktf_ref[cols, rf],
                              vf_ref[rf, cols], gs_ref.at[hh, 0], cps * c + sub,
                              ct_ref.at[hh, 0], n_ref.at[hh, 0], reverse=False)
            hf_ref[rf, cols] = hf.astype(hf_ref.dtype)
            hb = _mlstm_chunk(qb_ref[rb, cols], kb_ref[rb, cols], ktb_ref[cols, rb],
                              vb_ref[rb, cols], gs_ref.at[hh, 1], nc - 1 - (cps * c + sub),
                              ct_ref.at[hh, 1], n_ref.at[hh, 1], reverse=True)
            hb_ref[rb, cols] = hb.astype(hb_ref.dtype)


def _mlstm(q, k, kt, v, gates4, bsz, seq, hp=MLSTM_HEADS_PER_STEP, cps=MLSTM_CHUNKS_PER_STEP):
    t = q.shape[0]
    L = MCHUNK
    nc = seq // L
    dh = M_HEAD_DIM
    width = hp * dh
    rows = cps * L
    ns = nc // cps

    qkv_f = pl.BlockSpec((rows, width), lambda b, h, c: (b * ns + c, h))
    qkv_b = pl.BlockSpec((rows, width), lambda b, h, c: (b * ns + ns - 1 - c, h))
    kt_f = pl.BlockSpec((width, rows), lambda b, h, c: (h, b * ns + c))
    kt_b = pl.BlockSpec((width, rows), lambda b, h, c: (h, b * ns + ns - 1 - c))
    g_spec = pl.BlockSpec((None, hp, 4, nc, L), lambda b, h, c: (b, h, 0, 0, 0))
    out = jax.ShapeDtypeStruct((t, M_WIDTH), BF16)
    return pl.pallas_call(
        functools.partial(_mlstm_kernel, hp=hp, nc=nc, cps=cps),
        grid=(bsz, M_HEADS // hp, ns),
        in_specs=[g_spec, qkv_f, qkv_f, kt_f, qkv_f, qkv_b, qkv_b, kt_b, qkv_b],
        out_specs=[qkv_f, qkv_b],
        out_shape=[out, out],
        scratch_shapes=[
            pltpu.VMEM((hp, 2, dh, dh), F32),
            pltpu.VMEM((hp, 2, 1, dh), F32),
            pltpu.VMEM((hp, 2, GS_ROWS, nc, L), F32),
            pltpu.VMEM((3, nc, LANES), F32),
        ],
        compiler_params=pltpu.CompilerParams(
            dimension_semantics=("parallel", "parallel", "arbitrary"),
            vmem_limit_bytes=VMEM_LIMIT),
        name="mlstm",
    )(gates4, q, k, kt, v, q, k, kt, v)


def _merge_kernel(hf_ref, hb_ref, xc_ref, yf_ref, x_ref, nw_ref, wzf_ref, wzm_ref, wom_ref, wg_ref,
                  hn_ref, sk_ref, wf_ref, wm_ref, wo_ref, fn_ref, o_ref, *, final_norm, sub_rows):
    for r in range(o_ref.shape[0] // sub_rows):
        rows = slice(sub_rows * r, sub_rows * (r + 1))
        hin = _rmsnorm_bf16(x_ref[rows, :], nw_ref[...])
        y_b = None
        for h in range(M_HEADS):
            cols = slice(M_HEAD_DIM * h, M_HEAD_DIM * (h + 1))
            hc2 = (hf_ref[rows, cols].astype(F32) + hb_ref[rows, cols].astype(F32))
            hc2 = hc2 * (1.0 + jnp.tanh(_dot(hin, wom_ref[:, cols])))
            hc = hc2 * lax.rsqrt(jnp.mean(hc2 * hc2, axis=-1, keepdims=True) + 4.0 * EPS)
            u = hc * hn_ref[:, cols] + sk_ref[:, cols] * xc_ref[rows, cols].astype(F32)
            hz = _dot(hin, wzm_ref[:, cols])
            u = u * (hz * (1.0 + jnp.tanh(hz)))
            part = _dot(u.astype(BF16), wm_ref[cols, :])
            y_b = part if y_b is None else y_b + part
        hzf = _dot(hin, wzf_ref[...])
        ya_in = yf_ref[rows, :].astype(F32) * (hzf * (1.0 + jnp.tanh(hzf)))
        y_a = _dot(ya_in.astype(BF16), wf_ref[...])
        t_a = jnp.tanh(_dot(hin, wg_ref[:, :D_MODEL]))
        t_b = jnp.tanh(_dot(hin, wg_ref[:, D_MODEL:]))
        merged2 = (y_a + y_b) + (t_a * y_a + t_b * y_b)
        xn = x_ref[rows, :] + _dot(merged2.astype(BF16), wo_ref[...])
        if final_norm:
            xn = xn * lax.rsqrt(jnp.mean(xn * xn, axis=-1, keepdims=True) + EPS) * fn_ref[...]
        o_ref[rows, :] = xn


def _merge(hf, hb, xc, yf, x2, nw, w_in_bf, hn, sk, wf, wm, wo, fn, final_norm,
           tm=512, sub_rows=256):
    t, d = x2.shape

    def rows(width):
        return pl.BlockSpec((tm, width), lambda i: (i, 0))

    def full(a):
        return _resident(a.shape, lambda i: (0,) * a.ndim)

    def w_in_cols(col, width):
        return _resident((d, width), lambda i: (0, col // width))

    return pl.pallas_call(
        functools.partial(_merge_kernel, final_norm=final_norm, sub_rows=sub_rows),
        grid=(t // tm,),
        in_specs=[
            rows(M_WIDTH), rows(M_WIDTH), rows(M_WIDTH), rows(F_WIDTH), rows(D_MODEL),
            full(nw),
            w_in_cols(COL_ZF, F_WIDTH), w_in_cols(COL_ZM, M_WIDTH), w_in_cols(COL_OM, M_WIDTH),
            w_in_cols(COL_G, 2 * D_MODEL),
            full(hn), full(sk), full(wf), full(wm), full(wo), full(fn),
        ],
        out_specs=rows(D_MODEL),
        out_shape=jax.ShapeDtypeStruct((t, D_MODEL), F32),
        compiler_params=pltpu.CompilerParams(
            dimension_semantics=("parallel",),
            vmem_limit_bytes=VMEM_LIMIT),
        name="merge_out",
    )(hf, hb, xc, yf, x2, nw, w_in_bf, w_in_bf, w_in_bf, w_in_bf, hn, sk, wf, wm, wo, fn)


def _block_diag_tiles(w):
    rows = w.reshape(-1, MXU_DIM, QKV_BLOCK)
    dense = jnp.tile(rows, (1, 1, MXU_DIM // QKV_BLOCK))
    r = lax.broadcasted_iota(jnp.int32, (MXU_DIM, MXU_DIM), 0) // QKV_BLOCK
    c = lax.broadcasted_iota(jnp.int32, (MXU_DIM, MXU_DIM), 1) // QKV_BLOCK
    return jnp.where(r == c, dense, 0.0).astype(BF16)


def kernel(x, norm_w, w_in, w_fourier, conv_w, conv_b, w_q, w_k, w_v,
           w_igate_fwd, b_igate_fwd, w_fgate_fwd, b_fgate_fwd,
           w_igate_bwd, b_igate_bwd, w_fgate_bwd, b_fgate_bwd,
           hnorm_w, skip_w, w_mlstm, w_out, final_norm_w):
    bsz, seq, d = x.shape
    depth = w_in.shape[0]
    assert d == D_MODEL and seq % (FFT_N2 * BF16_SUBLANES * 2) == 0 and seq % MCHUNK == 0
    t = bsz * seq
    nc = seq // MCHUNK
    consts = tuple(jnp.asarray(a, dtype=F32).astype(BF16) for a in _fft_constants(seq))
    fn = final_norm_w.reshape(1, d)

    x2 = x.reshape(t, d)
    for l in range(depth):
        col = lax.broadcasted_iota(jnp.int32, (1, IN_COLS), 1)
        half = ((col >= COL_ZF) & (col < COL_XM)) | (col >= COL_ZM)
        w_in_bf = (w_in[l] * jnp.where(half, 0.5, 1.0)).astype(BF16)
        nw = norm_w[l].reshape(1, d)

        wg = jnp.concatenate([w_igate_fwd[l], w_fgate_fwd[l], w_igate_bwd[l], w_fgate_bwd[l]], axis=1)
        wg = jnp.pad(wg, ((0, 0), (0, GATE_LANES - wg.shape[1])))
        wg = wg.reshape(3, M_WIDTH // MXU_DIM, MXU_DIM, GATE_LANES)
        qscale = M_HEAD_DIM ** -0.5
        wg = (wg * jnp.asarray([1.0 / qscale, 1.0, 1.0], F32).reshape(3, 1, 1, 1)).astype(BF16)
        bg = jnp.concatenate([b_igate_fwd[l], b_fgate_fwd[l], b_igate_bwd[l], b_fgate_bwd[l]])
        bg = jnp.pad(bg, (0, GATE_LANES - bg.shape[0])).reshape(1, GATE_LANES)
        wk_tiles = _block_diag_tiles(w_k[l])
        wkt_tiles = _block_diag_tiles(jnp.swapaxes(w_k[l], -1, -2))
        xf, xm = _inproj(x2, nw, w_in_bf)
        q, k, kt, v, xc, gates = _conv_qkv(
            xm, seq, 0.5 * conv_w[l], 0.5 * conv_b[l].reshape(1, M_WIDTH),
            _block_diag_tiles(w_q[l] * qscale), wk_tiles, wkt_tiles,
            _block_diag_tiles(w_v[l]), wg, bg)

        yt = _fourier_mix(xf, bsz, seq, consts)
        yf = jnp.transpose(yt, (0, 2, 1, 3)).reshape(t, F_WIDTH)

        g4 = gates[:, :4 * M_HEADS].reshape(bsz, nc, MCHUNK, 4, M_HEADS)
        g4 = jnp.transpose(g4, (0, 4, 3, 1, 2))
        hf, hb = _mlstm(q, k, kt, v, g4, bsz, seq)

        x2 = _merge(hf, hb, xc, yf, x2, nw, w_in_bf,
                    hnorm_w[l].reshape(1, M_WIDTH), skip_w[l].reshape(1, M_WIDTH),
                    w_fourier[l].astype(BF16), w_mlstm[l].astype(BF16),
                    (0.5 * w_out[l]).astype(BF16),
                    fn, final_norm=(l == depth - 1))
    return x2.reshape(bsz, seq, d)
```

```python
import functools

import numpy as np
import jax
import jax.numpy as jnp
from jax import lax
from jax.experimental import pallas as pl
from jax.experimental.pallas import tpu as pltpu

F32 = jnp.float32
BF16 = jnp.bfloat16

D_MODEL = 1024
F_WIDTH = D_MODEL
F_GROUPS = 4
F_GROUP_DIM = F_WIDTH // F_GROUPS
M_WIDTH = 2 * D_MODEL
M_HEADS = 4
M_HEAD_DIM = M_WIDTH // M_HEADS
QKV_BLOCK = 4
CONV_K = 5
EPS = 1e-6
IN_COLS = 2 * F_WIDTH + 3 * M_WIDTH + 2 * D_MODEL

MCHUNK = 256
MLSTM_HEADS_PER_STEP = 2
MLSTM_CHUNKS_PER_STEP = 2

COL_XF = 0
COL_ZF = F_WIDTH
COL_XM = 2 * F_WIDTH
COL_ZM = COL_XM + M_WIDTH
COL_OM = COL_ZM + M_WIDTH
COL_G = COL_OM + M_WIDTH
GCOL_ZM = 0
GCOL_OM = M_WIDTH
GCOL_G = 2 * M_WIDTH
GCOL_ZF = 2 * M_WIDTH + 2 * D_MODEL

LANES = 128
SUBLANES = 8
BF16_SUBLANES = 16
MXU_DIM = 256
GATE_LANES = 128
VMEM_LIMIT = 56 * 1024 * 1024

FFT_N2 = 32


def _dot(a, b):
    return jnp.dot(a, b, preferred_element_type=F32)


def _dot_nt(a, b):
    return lax.dot_general(a, b, (((1,), (1,)), ((), ())), preferred_element_type=F32)


def _sigmoid(x):
    return 0.5 * jnp.tanh(0.5 * x) + 0.5


def _silu(x):
    hx = 0.5 * x
    return hx * jnp.tanh(hx) + hx


def _rmsnorm_bf16(x, w):
    ms = jnp.mean(x * x, axis=-1, keepdims=True)
    return (x * lax.rsqrt(ms + EPS) * w).astype(BF16)


def _resident(shape, index_map):
    return pl.BlockSpec(shape, index_map, pipeline_mode=pl.Buffered(1))


def _inproj_kernel(x_ref, nw_ref, wf_ref, wm_ref, xf_ref, xm_ref, *, sub_rows):
    wf = wf_ref[...].astype(BF16)
    wm = wm_ref[...].astype(BF16)
    for r in range(x_ref.shape[0] // sub_rows):
        rows = slice(sub_rows * r, sub_rows * (r + 1))
        h = _rmsnorm_bf16(x_ref[rows, :], nw_ref[...])
        xf_ref[rows, :] = _dot(h, wf).astype(BF16)
        xm_ref[rows, :] = _dot(h, wm).astype(BF16)


def _inproj(x2, norm_w, w_in, tm=512, sub_rows=256):
    t, d = x2.shape
    return pl.pallas_call(
        functools.partial(_inproj_kernel, sub_rows=sub_rows),
        grid=(t // tm,),
        in_specs=[
            pl.BlockSpec((tm, d), lambda i: (i, 0)),
            pl.BlockSpec((1, d), lambda i: (0, 0)),
            _resident((d, F_WIDTH), lambda i: (0, COL_XF // F_WIDTH)),
            _resident((d, M_WIDTH), lambda i: (0, COL_XM // M_WIDTH)),
        ],
        out_specs=[pl.BlockSpec((tm, F_WIDTH), lambda i: (i, 0)),
                   pl.BlockSpec((tm, M_WIDTH), lambda i: (i, 0))],
        out_shape=[jax.ShapeDtypeStruct((t, F_WIDTH), BF16),
                   jax.ShapeDtypeStruct((t, M_WIDTH), BF16)],
        compiler_params=pltpu.CompilerParams(
            dimension_semantics=("parallel",),
            vmem_limit_bytes=VMEM_LIMIT),
        name="inproj",
    )(x2, norm_w, w_in, w_in)


def _fft_constants(seq):
    n2 = FFT_N2
    n1 = seq // n2
    sub = BF16_SUBLANES
    nblk = n1 // sub
    i = np.arange(nblk)[:, None, None, None]
    k2 = np.arange(n2)[None, :, None, None]
    j = np.arange(sub)[None, None, :, None]
    m2 = np.arange(n2)[None, None, None, :]
    ang = 2.0 * np.pi * (((m2 * k2) % n2) / n2 + (((sub * i + j) * k2) % seq) / seq)
    val = np.stack([np.cos(ang), -np.sin(ang)], axis=1)
    wa = np.zeros((nblk, 2, n2, sub, n2, sub), np.float64)
    for jj in range(sub):
        wa[:, :, :, jj, :, jj] = val[:, :, :, jj, :]
    wa = wa.reshape(nblk, 2 * n2 * sub, n2 * sub)
    a = np.arange(n1)
    angb = 2.0 * np.pi * ((a[:, None] * a[None, :]) % n1) / n1
    wr, wi = np.cos(angb), -np.sin(angb)
    wb = np.block([[wr, -wi], [wi, wr]])
    c = np.arange(F_GROUP_DIM)
    angc = 2.0 * np.pi * ((c[:, None] * c[None, :]) % F_GROUP_DIM) / F_GROUP_DIM
    wc = np.concatenate([np.cos(angc), np.sin(angc)], axis=0) / np.sqrt(seq * F_GROUP_DIM)
    return wa, wb, wc


def _fft_a_kernel(x_ref, wa_ref, o_ref, *, nsub):
    n2 = FFT_N2
    sub = BF16_SUBLANES
    c = x_ref.shape[-1]
    for i in range(nsub):
        rows = slice(sub * i, sub * (i + 1))
        xs = x_ref[:, rows, :].reshape(n2 * sub, c)
        z = _dot(wa_ref[i], xs)
        o_ref[:, :, rows, :] = z.astype(BF16).reshape(2, n2, sub, c)


def _fft_b_kernel(z_ref, wb_ref, wc_ref, o_ref):
    nk, n1, c = z_ref.shape[1:]
    for kk in range(nk):
        xs = z_ref[:, kk].reshape(2 * n1, c)
        g = _dot(wb_ref[...], xs).astype(BF16)
        for grp in range(c // F_GROUP_DIM):
            cols = slice(F_GROUP_DIM * grp, F_GROUP_DIM * (grp + 1))
            lhs = jnp.concatenate([g[:n1, cols], g[n1:, cols]], axis=1)
            o_ref[kk, :, cols] = _dot(lhs, wc_ref[...]).astype(BF16)


def _fourier_mix(xf, bsz, seq, consts, nsub=4, kb=4):
    wa, wb, wc = consts
    n2 = FFT_N2
    n1 = seq // n2
    rblk = nsub * BF16_SUBLANES
    proj4 = xf.reshape(bsz, n2, n1, F_WIDTH)
    za = pl.pallas_call(
        functools.partial(_fft_a_kernel, nsub=nsub),
        grid=(n1 // rblk, bsz),
        in_specs=[
            pl.BlockSpec((None, n2, rblk, F_WIDTH), lambda i, b: (b, 0, i, 0)),
            pl.BlockSpec((nsub,) + wa.shape[1:], lambda i, b: (i, 0, 0)),
        ],
        out_specs=pl.BlockSpec((None, 2, n2, rblk, F_WIDTH), lambda i, b: (b, 0, 0, i, 0)),
        out_shape=jax.ShapeDtypeStruct((bsz, 2, n2, n1, F_WIDTH), BF16),
        compiler_params=pltpu.CompilerParams(
            dimension_semantics=("parallel", "parallel"),
            vmem_limit_bytes=VMEM_LIMIT),
        name="fft_stage_a",
    )(proj4, wa)
    yt = pl.pallas_call(
        _fft_b_kernel,
        grid=(bsz, n2 // kb),
        in_specs=[
            pl.BlockSpec((None, 2, kb, n1, F_WIDTH), lambda b, k: (b, 0, k, 0, 0)),
            _resident(wb.shape, lambda b, k: (0, 0)),
            _resident(wc.shape, lambda b, k: (0, 0)),
        ],
        out_specs=pl.BlockSpec((None, kb, n1, F_WIDTH), lambda b, k: (b, k, 0, 0)),
        out_shape=jax.ShapeDtypeStruct((bsz, n2, n1, F_WIDTH), BF16),
        compiler_params=pltpu.CompilerParams(
            dimension_semantics=("parallel", "parallel"),
            vmem_limit_bytes=VMEM_LIMIT),
        name="fft_stage_b",
    )(za, wb, wc)
    return yt


def _log_sigmoid(x):
    return jnp.minimum(x, 0.0) - jnp.log1p(jnp.exp(-jnp.abs(x)))


def _conv_qkv_kernel(prev_ref, cur_ref, next_ref, cw_ref, cb_ref, wq_ref, wk_ref, wkt_ref, wv_ref,
                     wg_ref, bg_ref, q_ref, k_ref, kt_ref, v_ref, xc_ref, g_ref, xs_ref,
                     *, tiles_per_seq):
    tm = cur_ref.shape[0]
    halo = BF16_SUBLANES
    pos = pl.program_id(0) % tiles_per_seq
    keep_prev = (pos != 0).astype(F32)
    keep_next = (pos != tiles_per_seq - 1).astype(F32)
    xs_ref[0:halo, :] = prev_ref[...].astype(F32) * keep_prev
    xs_ref[halo:halo + tm, :] = cur_ref[...].astype(F32)
    xs_ref[halo + tm:, :] = next_ref[...].astype(F32) * keep_next

    gacc = jnp.zeros((tm, GATE_LANES), F32)
    for g in range(M_WIDTH // MXU_DIM):
        cols = slice(MXU_DIM * g, MXU_DIM * (g + 1))
        acc = jnp.broadcast_to(cb_ref[:, cols], (tm, MXU_DIM))
        ext = xs_ref[halo - SUBLANES:halo + tm + SUBLANES, cols]
        for j in range(CONV_K):
            shift = (CONV_K // 2 - j) % ext.shape[0]
            tap = ext if shift == 0 else pltpu.roll(ext, shift, axis=0)
            acc = acc + cw_ref[j:j + 1, cols] * tap[SUBLANES:SUBLANES + tm]
        xcb = (acc * (1.0 + jnp.tanh(acc))).astype(BF16)
        q = _dot(xcb, wq_ref[g])
        k = _dot(xcb, wk_ref[g])
        kt_ref[cols, :] = _dot_nt(wkt_ref[g], xcb).astype(BF16)
        v = _dot(cur_ref[:, cols], wv_ref[g])
        qb, kb, vb = q.astype(BF16), k.astype(BF16), v.astype(BF16)
        gacc = gacc + _dot(qb, wg_ref[0, g]) + _dot(kb, wg_ref[1, g]) + _dot(vb, wg_ref[2, g])
        q_ref[:, cols] = qb
        k_ref[:, cols] = kb
        v_ref[:, cols] = vb
        xc_ref[:, cols] = xcb
    gates = gacc + bg_ref[...]
    lane = lax.broadcasted_iota(jnp.int32, gates.shape, 1)
    is_forget = (lane // M_HEADS) % 2 == 1
    g_ref[...] = jnp.where(is_forget, _log_sigmoid(gates), gates)


def _conv_qkv(xm, seq, cw, cb, wq, wk, wkt, wv, wg, bg, tm=256):
    t = xm.shape[0]
    halo = BF16_SUBLANES
    hb = tm // halo
    nhalo = t // halo
    out_bf = jax.ShapeDtypeStruct((t, M_WIDTH), BF16)
    row_spec = pl.BlockSpec((tm, M_WIDTH), lambda i: (i, 0))

    def full(a):
        return _resident(a.shape, lambda i: (0,) * a.ndim)

    return pl.pallas_call(
        functools.partial(_conv_qkv_kernel, tiles_per_seq=seq // tm),
        grid=(t // tm,),
        in_specs=[
            pl.BlockSpec((halo, M_WIDTH), lambda i: (jnp.maximum(i * hb - 1, 0), 0)),
            row_spec,
            pl.BlockSpec((halo, M_WIDTH), lambda i: (jnp.minimum((i + 1) * hb, nhalo - 1), 0)),
            full(cw), full(cb), full(wq), full(wk), full(wkt), full(wv), full(wg), full(bg),
        ],
        out_specs=[row_spec, row_spec, pl.BlockSpec((M_WIDTH, tm), lambda i: (0, i)),
                   row_spec, row_spec,
                   pl.BlockSpec((tm, GATE_LANES), lambda i: (i, 0))],
        out_shape=[out_bf, out_bf, jax.ShapeDtypeStruct((M_WIDTH, t), BF16), out_bf, out_bf,
                   jax.ShapeDtypeStruct((t, GATE_LANES), F32)],
        scratch_shapes=[pltpu.VMEM((tm + 2 * halo, M_WIDTH), F32)],
        compiler_params=pltpu.CompilerParams(
            dimension_semantics=("parallel",),
            vmem_limit_bytes=VMEM_LIMIT),
        name="conv_qkv_gates",
    )(xm, xm, xm, cw, cb, wq, wk, wkt, wv, wg, bg)


GS_C, GS_M, GS_INTER, GS_ENEGM, GS_WS, GS_DECAY, GS_ROWS = 0, 1, 2, 3, 4, 5, 6


def _lane_scan(x, op, fill, reverse):
    n = x.shape[1]
    lane = lax.broadcasted_iota(jnp.int32, x.shape, 1)
    sh = 1
    while sh < n:
        if reverse:
            x = op(x, jnp.where(lane < n - sh, pltpu.roll(x, n - sh, axis=1), fill))
        else:
            x = op(x, jnp.where(lane >= sh, pltpu.roll(x, sh, axis=1), fill))
        sh *= 2
    return x


def _gate_prep(i_pre, log_f, gs_ref, tmp_ref, *, reverse):
    nc, L = i_pre.shape
    b = _lane_scan(log_f, jnp.add, 0.0, reverse)
    c = i_pre - b
    cm = _lane_scan(c, jnp.maximum, -jnp.inf, reverse)
    end = slice(0, 1) if reverse else slice(L - 1, L)
    g = b[:, end]
    cmt = cm[:, end]
    tmp_ref[0] = jnp.broadcast_to(g, (nc, LANES))
    tmp_ref[1] = jnp.broadcast_to(g + cmt, (nc, LANES))

    m = jnp.zeros((1, LANES), F32)
    for step in range(nc):
        idx = nc - 1 - step if reverse else step
        tmp_ref[2, idx:idx + 1, :] = m
        m = jnp.maximum(m + tmp_ref[0, idx:idx + 1, :], tmp_ref[1, idx:idx + 1, :])
    m_prev = tmp_ref[2][:, 0:1]
    big_m = jnp.maximum(m_prev, cm)
    mx = jnp.maximum(m_prev, cmt)
    gs_ref[GS_C] = c
    gs_ref[GS_M] = big_m
    gs_ref[GS_INTER] = jnp.exp(m_prev - big_m)
    gs_ref[GS_ENEGM] = jnp.exp(-b - big_m)
    gs_ref[GS_WS] = jnp.exp(c - mx)
    gs_ref[GS_DECAY] = jnp.broadcast_to(jnp.exp(m_prev - mx), (nc, L))


def _mlstm_chunk(q, k, kt, v, gs_ref, ci, ct_ref, n_ref, *, reverse):
    L = q.shape[0]

    def row(r):
        return gs_ref[r, pl.ds(ci, 1), :]

    c_row = row(GS_C)
    sub = lax.broadcasted_iota(jnp.int32, (SUBLANES, L), 0)
    packed = jnp.where(sub == 0, row(GS_M),
                       jnp.where(sub == 1, row(GS_INTER),
                                 jnp.where(sub == 2, row(GS_ENEGM), row(GS_WS))))
    tr = jnp.concatenate([packed, jnp.zeros((LANES - SUBLANES, L), F32)], axis=0).T
    m_col, inter, enegm, ws = tr[:, 0:1], tr[:, 1:2], tr[:, 2:3], tr[:, 3:4]
    decay = row(GS_DECAY)[:, 0:1]

    r_i = lax.broadcasted_iota(jnp.int32, (L, L), 0)
    c_i = lax.broadcasted_iota(jnp.int32, (L, L), 1)
    mask = (c_i >= r_i) if reverse else (c_i <= r_i)
    p = jnp.exp(jnp.where(mask, c_row - m_col, -jnp.inf))
    s = _dot_nt(q, k) * p
    qi = inter * q.astype(F32)
    num = _dot(s.astype(BF16), v) + _dot(qi.astype(BF16), ct_ref[...].astype(BF16))
    den = jnp.sum(s, axis=1, keepdims=True) + jnp.sum(qi * n_ref[...], axis=1, keepdims=True)
    h = num * (1.0 / jnp.maximum(jnp.abs(den), enegm))

    vw = ws.astype(BF16) * v
    ct_ref[...] = decay * ct_ref[...] + _dot(kt, vw)
    n_ref[...] = decay * n_ref[...] + jnp.sum(ws * k.astype(F32), axis=0, keepdims=True)
    return h


def _mlstm_kernel(gates_ref, qf_ref, kf_ref, ktf_ref, vf_ref, qb_ref, kb_ref, ktb_ref, vb_ref,
                  hf_ref, hb_ref,
                  ct_ref, n_ref, gs_ref, tmp_ref, *, hp, nc, cps):
    c = pl.program_id(2)
    dh = M_HEAD_DIM
    L = MCHUNK

    @pl.when(c == 0)
    def _():
        ct_ref[...] = jnp.zeros_like(ct_ref)
        n_ref[...] = jnp.zeros_like(n_ref)
        for hh in range(hp):
            _gate_prep(gates_ref[hh, 0], gates_ref[hh, 1], gs_ref.at[hh, 0], tmp_ref.at[hh, 0],
                       reverse=False)
            _gate_prep(gates_ref[hh, 2], gates_ref[hh, 3], gs_ref.at[hh, 1], tmp_ref.at[hh, 1],
                       reverse=True)

    for sub in range(cps):
        rf = slice(L * sub, L * (sub + 1))
        rb = slice(L * (cps - 1 - sub), L * (cps - sub))
        for hh in range(hp):
            cols = slice(dh * hh, dh * (hh + 1))
            hf = _mlstm_chunk(qf_ref[rf, cols], kf_ref[rf, cols], ktf_ref[cols, rf],
                              vf_ref[rf, cols], gs_ref.at[hh, 0], cps * c + sub,
                              ct_ref.at[hh, 0], n_ref.at[hh, 0], reverse=False)
            hf_ref[rf, cols] = hf.astype(hf_ref.dtype)
            hb = _mlstm_chunk(qb_ref[rb, cols], kb_ref[rb, cols], ktb_ref[cols, rb],
                              vb_ref[rb, cols], gs_ref.at[hh, 1], nc - 1 - (cps * c + sub),
                              ct_ref.at[hh, 1], n_ref.at[hh, 1], reverse=True)
            hb_ref[rb, cols] = hb.astype(hb_ref.dtype)


def _mlstm(q, k, kt, v, gates4, bsz, seq, hp=MLSTM_HEADS_PER_STEP, cps=MLSTM_CHUNKS_PER_STEP):
    t = q.shape[0]
    L = MCHUNK
    nc = seq // L
    dh = M_HEAD_DIM
    width = hp * dh
    rows = cps * L
    ns = nc // cps

    qkv_f = pl.BlockSpec((rows, width), lambda b, h, c: (b * ns + c, h))
    qkv_b = pl.BlockSpec((rows, width), lambda b, h, c: (b * ns + ns - 1 - c, h))
    kt_f = pl.BlockSpec((width, rows), lambda b, h, c: (h, b * ns + c))
    kt_b = pl.BlockSpec((width, rows), lambda b, h, c: (h, b * ns + ns - 1 - c))
    g_spec = pl.BlockSpec((None, hp, 4, nc, L), lambda b, h, c: (b, h, 0, 0, 0))
    out = jax.ShapeDtypeStruct((t, M_WIDTH), BF16)
    return pl.pallas_call(
        functools.partial(_mlstm_kernel, hp=hp, nc=nc, cps=cps),
        grid=(bsz, M_HEADS // hp, ns),
        in_specs=[g_spec, qkv_f, qkv_f, kt_f, qkv_f, qkv_b, qkv_b, kt_b, qkv_b],
        out_specs=[qkv_f, qkv_b],
        out_shape=[out, out],
        scratch_shapes=[
            pltpu.VMEM((hp, 2, dh, dh), F32),
            pltpu.VMEM((hp, 2, 1, dh), F32),
            pltpu.VMEM((hp, 2, GS_ROWS, nc, L), F32),
            pltpu.VMEM((hp, 2, 3, nc, LANES), F32),
        ],
        compiler_params=pltpu.CompilerParams(
            dimension_semantics=("parallel", "parallel", "arbitrary"),
            vmem_limit_bytes=VMEM_LIMIT),
        name="mlstm",
    )(gates4, q, k, kt, v, q, k, kt, v)


def _merge_kernel(hf_ref, hb_ref, xc_ref, yf_ref, x_ref, nw_ref, wzf_ref, wzm_ref, wom_ref, wg_ref,
                  hn_ref, sk_ref, wf_ref, wm_ref, wo_ref, fn_ref, o_ref, *, final_norm, sub_rows):
    for r in range(o_ref.shape[0] // sub_rows):
        rows = slice(sub_rows * r, sub_rows * (r + 1))
        hin = _rmsnorm_bf16(x_ref[rows, :], nw_ref[...])
        y_b = None
        for h in range(M_HEADS):
            cols = slice(M_HEAD_DIM * h, M_HEAD_DIM * (h + 1))
            hc2 = (hf_ref[rows, cols].astype(F32) + hb_ref[rows, cols].astype(F32))
            hc2 = hc2 * (1.0 + jnp.tanh(_dot(hin, wom_ref[:, cols])))
            hc = hc2 * lax.rsqrt(jnp.mean(hc2 * hc2, axis=-1, keepdims=True) + 4.0 * EPS)
            u = hc * hn_ref[:, cols] + sk_ref[:, cols] * xc_ref[rows, cols].astype(F32)
            hz = _dot(hin, wzm_ref[:, cols])
            u = u * (hz * (1.0 + jnp.tanh(hz)))
            part = _dot(u.astype(BF16), wm_ref[cols, :])
            y_b = part if y_b is None else y_b + part
        hzf = _dot(hin, wzf_ref[...])
        ya_in = yf_ref[rows, :].astype(F32) * (hzf * (1.0 + jnp.tanh(hzf)))
        y_a = _dot(ya_in.astype(BF16), wf_ref[...])
        t_a = jnp.tanh(_dot(hin, wg_ref[:, :D_MODEL]))
        t_b = jnp.tanh(_dot(hin, wg_ref[:, D_MODEL:]))
        merged2 = (y_a + y_b) + (t_a * y_a + t_b * y_b)
        xn = x_ref[rows, :] + _dot(merged2.astype(BF16), wo_ref[...])
        if final_norm:
            xn = xn * lax.rsqrt(jnp.mean(xn * xn, axis=-1, keepdims=True) + EPS) * fn_ref[...]
        o_ref[rows, :] = xn


def _merge(hf, hb, xc, yf, x2, nw, w_gate, hn, sk, wf, wm, wo, fn, final_norm,
           tm=512, sub_rows=512):
    t, d = x2.shape

    def rows(width):
        return pl.BlockSpec((tm, width), lambda i: (i, 0))

    def full(a):
        return _resident(a.shape, lambda i: (0,) * a.ndim)

    def w_gate_cols(col, width):
        return _resident((d, width), lambda i: (0, col // width))

    return pl.pallas_call(
        functools.partial(_merge_kernel, final_norm=final_norm, sub_rows=sub_rows),
        grid=(t // tm,),
        in_specs=[
            rows(M_WIDTH), rows(M_WIDTH), rows(M_WIDTH), rows(F_WIDTH), rows(D_MODEL),
            full(nw),
            w_gate_cols(GCOL_ZF, F_WIDTH), w_gate_cols(GCOL_ZM, M_WIDTH),
            w_gate_cols(GCOL_OM, M_WIDTH), w_gate_cols(GCOL_G, 2 * D_MODEL),
            full(hn), full(sk), full(wf), full(wm), full(wo), full(fn),
        ],
        out_specs=rows(D_MODEL),
        out_shape=jax.ShapeDtypeStruct((t, D_MODEL), F32),
        compiler_params=pltpu.CompilerParams(
            dimension_semantics=("parallel",),
            vmem_limit_bytes=VMEM_LIMIT),
        name="merge_out",
    )(hf, hb, xc, yf, x2, nw, w_gate, w_gate, w_gate, w_gate, hn, sk, wf, wm, wo, fn)


def _block_diag_tiles(w):
    rows = w.reshape(-1, MXU_DIM, QKV_BLOCK)
    dense = jnp.tile(rows, (1, 1, MXU_DIM // QKV_BLOCK))
    r = lax.broadcasted_iota(jnp.int32, (MXU_DIM, MXU_DIM), 0) // QKV_BLOCK
    c = lax.broadcasted_iota(jnp.int32, (MXU_DIM, MXU_DIM), 1) // QKV_BLOCK
    return jnp.where(r == c, dense, 0.0).astype(BF16)


def kernel(x, norm_w, w_in, w_fourier, conv_w, conv_b, w_q, w_k, w_v,
           w_igate_fwd, b_igate_fwd, w_fgate_fwd, b_fgate_fwd,
           w_igate_bwd, b_igate_bwd, w_fgate_bwd, b_fgate_bwd,
           hnorm_w, skip_w, w_mlstm, w_out, final_norm_w):
    bsz, seq, d = x.shape
    depth = w_in.shape[0]
    assert d == D_MODEL and seq % (FFT_N2 * BF16_SUBLANES * 2) == 0 and seq % MCHUNK == 0
    t = bsz * seq
    nc = seq // MCHUNK
    consts = tuple(jnp.asarray(a, dtype=F32).astype(BF16) for a in _fft_constants(seq))
    fn = final_norm_w.reshape(1, d)

    x2 = x.reshape(t, d)
    for l in range(depth):
        w_gate = jnp.concatenate([w_in[l][:, COL_ZM:], w_in[l][:, COL_ZF:COL_XM]], axis=1)
        w_gate = (0.5 * w_gate).astype(BF16)
        nw = norm_w[l].reshape(1, d)

        wg = jnp.concatenate([w_igate_fwd[l], w_fgate_fwd[l], w_igate_bwd[l], w_fgate_bwd[l]], axis=1)
        wg = jnp.pad(wg, ((0, 0), (0, GATE_LANES - wg.shape[1])))
        wg = wg.reshape(3, M_WIDTH // MXU_DIM, MXU_DIM, GATE_LANES)
        qscale = M_HEAD_DIM ** -0.5
        wg = (wg * jnp.asarray([1.0 / qscale, 1.0, 1.0], F32).reshape(3, 1, 1, 1)).astype(BF16)
        bg = jnp.concatenate([b_igate_fwd[l], b_fgate_fwd[l], b_igate_bwd[l], b_fgate_bwd[l]])
        bg = jnp.pad(bg, (0, GATE_LANES - bg.shape[0])).reshape(1, GATE_LANES)
        wk_tiles = _block_diag_tiles(w_k[l])
        wkt_tiles = _block_diag_tiles(jnp.swapaxes(w_k[l], -1, -2))
        xf, xm = _inproj(x2, nw, w_in[l])
        q, k, kt, v, xc, gates = _conv_qkv(
            xm, seq, 0.5 * conv_w[l], 0.5 * conv_b[l].reshape(1, M_WIDTH),
            _block_diag_tiles(w_q[l] * qscale), wk_tiles, wkt_tiles,
            _block_diag_tiles(w_v[l]), wg, bg)

        yt = _fourier_mix(xf, bsz, seq, consts)
        yf = jnp.transpose(yt, (0, 2, 1, 3)).reshape(t, F_WIDTH)

        g4 = gates[:, :4 * M_HEADS].reshape(bsz, nc, MCHUNK, 4, M_HEADS)
        g4 = jnp.transpose(g4, (0, 4, 3, 1, 2))
        hf, hb = _mlstm(q, k, kt, v, g4, bsz, seq)

        x2 = _merge(hf, hb, xc, yf, x2, nw, w_gate,
                    hnorm_w[l].reshape(1, M_WIDTH), skip_w[l].reshape(1, M_WIDTH),
                    w_fourier[l].astype(BF16), w_mlstm[l].astype(BF16),
                    (0.5 * w_out[l]).astype(BF16),
                    fn, final_norm=(l == depth - 1))
    return x2.reshape(bsz, seq, d)
```

```python
import functools

import numpy as np
import jax
import jax.numpy as jnp
from jax import lax
from jax.experimental import pallas as pl
from jax.experimental.pallas import tpu as pltpu

F32 = jnp.float32
BF16 = jnp.bfloat16

D_MODEL = 1024
F_WIDTH = D_MODEL
F_GROUPS = 4
F_GROUP_DIM = F_WIDTH // F_GROUPS
M_WIDTH = 2 * D_MODEL
M_HEADS = 4
M_HEAD_DIM = M_WIDTH // M_HEADS
QKV_BLOCK = 4
CONV_K = 5
EPS = 1e-6
IN_COLS = 2 * F_WIDTH + 3 * M_WIDTH + 2 * D_MODEL

MCHUNK = 256
MLSTM_HEADS_PER_STEP = 2
MLSTM_CHUNKS_PER_STEP = 2

COL_XF = 0
COL_ZF = F_WIDTH
COL_XM = 2 * F_WIDTH
COL_ZM = COL_XM + M_WIDTH
COL_OM = COL_ZM + M_WIDTH
COL_G = COL_OM + M_WIDTH
GCOL_ZM = 0
GCOL_OM = M_WIDTH
GCOL_G = 2 * M_WIDTH
GCOL_ZF = 2 * M_WIDTH + 2 * D_MODEL

LANES = 128
SUBLANES = 8
BF16_SUBLANES = 16
MXU_DIM = 256
GATE_LANES = 128
VMEM_LIMIT = 56 * 1024 * 1024

FFT_N2 = 32


def _dot(a, b):
    return jnp.dot(a, b, preferred_element_type=F32)


def _dot_nt(a, b):
    return lax.dot_general(a, b, (((1,), (1,)), ((), ())), preferred_element_type=F32)


def _sigmoid(x):
    return 0.5 * jnp.tanh(0.5 * x) + 0.5


def _silu(x):
    hx = 0.5 * x
    return hx * jnp.tanh(hx) + hx


def _rmsnorm_bf16(x, w):
    ms = jnp.mean(x * x, axis=-1, keepdims=True)
    return (x * lax.rsqrt(ms + EPS) * w).astype(BF16)


def _resident(shape, index_map):
    return pl.BlockSpec(shape, index_map, pipeline_mode=pl.Buffered(1))


def _inproj_kernel(x_ref, nw_ref, wf_ref, wm_ref, xf_ref, xm_ref, *, sub_rows):
    wf = wf_ref[...].astype(BF16)
    wm = wm_ref[...].astype(BF16)
    for r in range(x_ref.shape[0] // sub_rows):
        rows = slice(sub_rows * r, sub_rows * (r + 1))
        h = _rmsnorm_bf16(x_ref[rows, :], nw_ref[...])
        xf_ref[rows, :] = _dot(h, wf).astype(BF16)
        xm_ref[rows, :] = _dot(h, wm).astype(BF16)


def _inproj(x2, norm_w, w_in, layer, tm=512, sub_rows=256):
    t, d = x2.shape
    return pl.pallas_call(
        functools.partial(_inproj_kernel, sub_rows=sub_rows),
        grid=(t // tm,),
        in_specs=[
            pl.BlockSpec((tm, d), lambda i: (i, 0)),
            pl.BlockSpec((1, d), lambda i: (0, 0)),
            _resident((None, d, F_WIDTH), lambda i: (layer, 0, COL_XF // F_WIDTH)),
            _resident((None, d, M_WIDTH), lambda i: (layer, 0, COL_XM // M_WIDTH)),
        ],
        out_specs=[pl.BlockSpec((tm, F_WIDTH), lambda i: (i, 0)),
                   pl.BlockSpec((tm, M_WIDTH), lambda i: (i, 0))],
        out_shape=[jax.ShapeDtypeStruct((t, F_WIDTH), BF16),
                   jax.ShapeDtypeStruct((t, M_WIDTH), BF16)],
        compiler_params=pltpu.CompilerParams(
            dimension_semantics=("parallel",),
            vmem_limit_bytes=VMEM_LIMIT),
        name="inproj",
    )(x2, norm_w, w_in, w_in)


def _fft_constants(seq):
    n2 = FFT_N2
    n1 = seq // n2
    sub = BF16_SUBLANES
    nblk = n1 // sub
    i = np.arange(nblk)[:, None, None, None]
    k2 = np.arange(n2)[None, :, None, None]
    j = np.arange(sub)[None, None, :, None]
    m2 = np.arange(n2)[None, None, None, :]
    ang = 2.0 * np.pi * (((m2 * k2) % n2) / n2 + (((sub * i + j) * k2) % seq) / seq)
    val = np.stack([np.cos(ang), -np.sin(ang)], axis=1)
    wa = np.zeros((nblk, 2, n2, sub, n2, sub), np.float64)
    for jj in range(sub):
        wa[:, :, :, jj, :, jj] = val[:, :, :, jj, :]
    wa = wa.reshape(nblk, 2 * n2 * sub, n2 * sub)
    a = np.arange(n1)
    angb = 2.0 * np.pi * ((a[:, None] * a[None, :]) % n1) / n1
    wr, wi = np.cos(angb), -np.sin(angb)
    wb = np.block([[wr, -wi], [wi, wr]])
    c = np.arange(F_GROUP_DIM)
    angc = 2.0 * np.pi * ((c[:, None] * c[None, :]) % F_GROUP_DIM) / F_GROUP_DIM
    wc = np.concatenate([np.cos(angc), np.sin(angc)], axis=0) / np.sqrt(seq * F_GROUP_DIM)
    return wa, wb, wc


def _fft_a_kernel(x_ref, wa_ref, o_ref, *, nsub):
    n2 = FFT_N2
    sub = BF16_SUBLANES
    c = x_ref.shape[-1]
    for i in range(nsub):
        rows = slice(sub * i, sub * (i + 1))
        xs = x_ref[:, rows, :].reshape(n2 * sub, c)
        z = _dot(wa_ref[i], xs)
        o_ref[:, :, rows, :] = z.astype(BF16).reshape(2, n2, sub, c)


def _fft_b_kernel(z_ref, wb_ref, wc_ref, o_ref):
    nk, n1, c = z_ref.shape[1:]
    for kk in range(nk):
        xs = z_ref[:, kk].reshape(2 * n1, c)
        g = _dot(wb_ref[...], xs).astype(BF16)
        for grp in range(c // F_GROUP_DIM):
            cols = slice(F_GROUP_DIM * grp, F_GROUP_DIM * (grp + 1))
            lhs = jnp.concatenate([g[:n1, cols], g[n1:, cols]], axis=1)
            o_ref[kk, :, cols] = _dot(lhs, wc_ref[...]).astype(BF16)


def _fourier_mix(xf, bsz, seq, consts, nsub=4, kb=4):
    wa, wb, wc = consts
    n2 = FFT_N2
    n1 = seq // n2
    rblk = nsub * BF16_SUBLANES
    proj4 = xf.reshape(bsz, n2, n1, F_WIDTH)
    za = pl.pallas_call(
        functools.partial(_fft_a_kernel, nsub=nsub),
        grid=(n1 // rblk, bsz),
        in_specs=[
            pl.BlockSpec((None, n2, rblk, F_WIDTH), lambda i, b: (b, 0, i, 0)),
            pl.BlockSpec((nsub,) + wa.shape[1:], lambda i, b: (i, 0, 0)),
        ],
        out_specs=pl.BlockSpec((None, 2, n2, rblk, F_WIDTH), lambda i, b: (b, 0, 0, i, 0)),
        out_shape=jax.ShapeDtypeStruct((bsz, 2, n2, n1, F_WIDTH), BF16),
        compiler_params=pltpu.CompilerParams(
            dimension_semantics=("parallel", "parallel"),
            vmem_limit_bytes=VMEM_LIMIT),
        name="fft_stage_a",
    )(proj4, wa)
    yt = pl.pallas_call(
        _fft_b_kernel,
        grid=(bsz, n2 // kb),
        in_specs=[
            pl.BlockSpec((None, 2, kb, n1, F_WIDTH), lambda b, k: (b, 0, k, 0, 0)),
            _resident(wb.shape, lambda b, k: (0, 0)),
            _resident(wc.shape, lambda b, k: (0, 0)),
        ],
        out_specs=pl.BlockSpec((None, kb, n1, F_WIDTH), lambda b, k: (b, k, 0, 0)),
        out_shape=jax.ShapeDtypeStruct((bsz, n2, n1, F_WIDTH), BF16),
        compiler_params=pltpu.CompilerParams(
            dimension_semantics=("parallel", "parallel"),
            vmem_limit_bytes=VMEM_LIMIT),
        name="fft_stage_b",
    )(za, wb, wc)
    return yt


def _log_sigmoid(x):
    return jnp.minimum(x, 0.0) - jnp.log1p(jnp.exp(-jnp.abs(x)))


def _conv_qkv_kernel(prev_ref, cur_ref, next_ref, cw_ref, cb_ref, wq_ref, wk_ref, wkt_ref, wv_ref,
                     wg_ref, bg_ref, q_ref, k_ref, kt_ref, v_ref, xc_ref, g_ref, xs_ref,
                     *, tiles_per_seq):
    tm = cur_ref.shape[0]
    halo = BF16_SUBLANES
    pos = pl.program_id(0) % tiles_per_seq
    keep_prev = (pos != 0).astype(F32)
    keep_next = (pos != tiles_per_seq - 1).astype(F32)
    xs_ref[0:halo, :] = prev_ref[...].astype(F32) * keep_prev
    xs_ref[halo:halo + tm, :] = cur_ref[...].astype(F32)
    xs_ref[halo + tm:, :] = next_ref[...].astype(F32) * keep_next

    gacc = jnp.zeros((tm, GATE_LANES), F32)
    for g in range(M_WIDTH // MXU_DIM):
        cols = slice(MXU_DIM * g, MXU_DIM * (g + 1))
        acc = jnp.broadcast_to(cb_ref[:, cols], (tm, MXU_DIM))
        ext = xs_ref[halo - SUBLANES:halo + tm + SUBLANES, cols]
        for j in range(CONV_K):
            shift = (CONV_K // 2 - j) % ext.shape[0]
            tap = ext if shift == 0 else pltpu.roll(ext, shift, axis=0)
            acc = acc + cw_ref[j:j + 1, cols] * tap[SUBLANES:SUBLANES + tm]
        xcb = (acc * (1.0 + jnp.tanh(acc))).astype(BF16)
        q = _dot(xcb, wq_ref[g])
        k = _dot(xcb, wk_ref[g])
        kt_ref[cols, :] = _dot_nt(wkt_ref[g], xcb).astype(BF16)
        v = _dot(cur_ref[:, cols], wv_ref[g])
        qb, kb, vb = q.astype(BF16), k.astype(BF16), v.astype(BF16)
        gacc = gacc + _dot(qb, wg_ref[0, g]) + _dot(kb, wg_ref[1, g]) + _dot(vb, wg_ref[2, g])
        q_ref[:, cols] = qb
        k_ref[:, cols] = kb
        v_ref[:, cols] = vb
        xc_ref[:, cols] = xcb
    gates = gacc + bg_ref[...]
    lane = lax.broadcasted_iota(jnp.int32, gates.shape, 1)
    is_forget = (lane // M_HEADS) % 2 == 1
    g_ref[...] = jnp.where(is_forget, _log_sigmoid(gates), gates)


def _conv_qkv(xm, seq, cw, cb, wq, wk, wkt, wv, wg, bg, tm=256):
    t = xm.shape[0]
    halo = BF16_SUBLANES
    hb = tm // halo
    nhalo = t // halo
    out_bf = jax.ShapeDtypeStruct((t, M_WIDTH), BF16)
    row_spec = pl.BlockSpec((tm, M_WIDTH), lambda i: (i, 0))

    def full(a):
        return _resident(a.shape, lambda i: (0,) * a.ndim)

    return pl.pallas_call(
        functools.partial(_conv_qkv_kernel, tiles_per_seq=seq // tm),
        grid=(t // tm,),
        in_specs=[
            pl.BlockSpec((halo, M_WIDTH), lambda i: (jnp.maximum(i * hb - 1, 0), 0)),
            row_spec,
            pl.BlockSpec((halo, M_WIDTH), lambda i: (jnp.minimum((i + 1) * hb, nhalo - 1), 0)),
            full(cw), full(cb), full(wq), full(wk), full(wkt), full(wv), full(wg), full(bg),
        ],
        out_specs=[row_spec, row_spec, pl.BlockSpec((M_WIDTH, tm), lambda i: (0, i)),
                   row_spec, row_spec,
                   pl.BlockSpec((tm, GATE_LANES), lambda i: (i, 0))],
        out_shape=[out_bf, out_bf, jax.ShapeDtypeStruct((M_WIDTH, t), BF16), out_bf, out_bf,
                   jax.ShapeDtypeStruct((t, GATE_LANES), F32)],
        scratch_shapes=[pltpu.VMEM((tm + 2 * halo, M_WIDTH), F32)],
        compiler_params=pltpu.CompilerParams(
            dimension_semantics=("parallel",),
            vmem_limit_bytes=VMEM_LIMIT),
        name="conv_qkv_gates",
    )(xm, xm, xm, cw, cb, wq, wk, wkt, wv, wg, bg)


GS_C, GS_M, GS_INTER, GS_ENEGM, GS_WS, GS_DECAY, GS_ROWS = 0, 1, 2, 3, 4, 5, 6


def _lane_scan(x, op, fill, reverse):
    n = x.shape[1]
    lane = lax.broadcasted_iota(jnp.int32, x.shape, 1)
    sh = 1
    while sh < n:
        if reverse:
            x = op(x, jnp.where(lane < n - sh, pltpu.roll(x, n - sh, axis=1), fill))
        else:
            x = op(x, jnp.where(lane >= sh, pltpu.roll(x, sh, axis=1), fill))
        sh *= 2
    return x


def _gate_prep(i_pre, log_f, gs_ref, tmp_ref, *, reverse):
    nc, L = i_pre.shape
    b = _lane_scan(log_f, jnp.add, 0.0, reverse)
    c = i_pre - b
    cm = _lane_scan(c, jnp.maximum, -jnp.inf, reverse)
    end = slice(0, 1) if reverse else slice(L - 1, L)
    g = b[:, end]
    cmt = cm[:, end]
    tmp_ref[0] = jnp.broadcast_to(g, (nc, LANES))
    tmp_ref[1] = jnp.broadcast_to(g + cmt, (nc, LANES))

    m = jnp.zeros((1, LANES), F32)
    for step in range(nc):
        idx = nc - 1 - step if reverse else step
        tmp_ref[2, idx:idx + 1, :] = m
        m = jnp.maximum(m + tmp_ref[0, idx:idx + 1, :], tmp_ref[1, idx:idx + 1, :])
    m_prev = tmp_ref[2][:, 0:1]
    big_m = jnp.maximum(m_prev, cm)
    mx = jnp.maximum(m_prev, cmt)
    gs_ref[GS_C] = c
    gs_ref[GS_M] = big_m
    gs_ref[GS_INTER] = jnp.exp(m_prev - big_m)
    gs_ref[GS_ENEGM] = jnp.exp(-b - big_m)
    gs_ref[GS_WS] = jnp.exp(c - mx)
    gs_ref[GS_DECAY] = jnp.broadcast_to(jnp.exp(m_prev - mx), (nc, L))


def _mlstm_chunk(q, k, kt, v, gs_ref, ci, ct_ref, n_ref, *, reverse):
    L = q.shape[0]

    def row(r):
        return gs_ref[r, pl.ds(ci, 1), :]

    c_row = row(GS_C)
    sub = lax.broadcasted_iota(jnp.int32, (SUBLANES, L), 0)
    packed = jnp.where(sub == 0, row(GS_M),
                       jnp.where(sub == 1, row(GS_INTER),
                                 jnp.where(sub == 2, row(GS_ENEGM), row(GS_WS))))
    tr = jnp.concatenate([packed, jnp.zeros((LANES - SUBLANES, L), F32)], axis=0).T
    m_col, inter, enegm, ws = tr[:, 0:1], tr[:, 1:2], tr[:, 2:3], tr[:, 3:4]
    decay = row(GS_DECAY)[:, 0:1]

    r_i = lax.broadcasted_iota(jnp.int32, (L, L), 0)
    c_i = lax.broadcasted_iota(jnp.int32, (L, L), 1)
    mask = (c_i >= r_i) if reverse else (c_i <= r_i)
    p = jnp.exp(jnp.where(mask, c_row - m_col, -jnp.inf))
    s = _dot_nt(q, k) * p
    qi = inter * q.astype(F32)
    num = _dot(s.astype(BF16), v) + _dot(qi.astype(BF16), ct_ref[...].astype(BF16))
    den = jnp.sum(s, axis=1, keepdims=True) + jnp.sum(qi * n_ref[...], axis=1, keepdims=True)
    h = num * (1.0 / jnp.maximum(jnp.abs(den), enegm))

    vw = ws.astype(BF16) * v
    ct_ref[...] = decay * ct_ref[...] + _dot(kt, vw)
    n_ref[...] = decay * n_ref[...] + jnp.sum(ws * k.astype(F32), axis=0, keepdims=True)
    return h


def _mlstm_kernel(gates_ref, qf_ref, kf_ref, ktf_ref, vf_ref, qb_ref, kb_ref, ktb_ref, vb_ref,
                  hf_ref, hb_ref,
                  ct_ref, n_ref, gs_ref, tmp_ref, *, hp, nc, cps):
    c = pl.program_id(2)
    dh = M_HEAD_DIM
    L = MCHUNK

    @pl.when(c == 0)
    def _():
        ct_ref[...] = jnp.zeros_like(ct_ref)
        n_ref[...] = jnp.zeros_like(n_ref)
        for hh in range(hp):
            _gate_prep(gates_ref[hh, 0], gates_ref[hh, 1], gs_ref.at[hh, 0], tmp_ref.at[hh, 0],
                       reverse=False)
            _gate_prep(gates_ref[hh, 2], gates_ref[hh, 3], gs_ref.at[hh, 1], tmp_ref.at[hh, 1],
                       reverse=True)

    for sub in range(cps):
        rf = slice(L * sub, L * (sub + 1))
        rb = slice(L * (cps - 1 - sub), L * (cps - sub))
        for hh in range(hp):
            cols = slice(dh * hh, dh * (hh + 1))
            hf = _mlstm_chunk(qf_ref[rf, cols], kf_ref[rf, cols], ktf_ref[cols, rf],
                              vf_ref[rf, cols], gs_ref.at[hh, 0], cps * c + sub,
                              ct_ref.at[hh, 0], n_ref.at[hh, 0], reverse=False)
            hf_ref[rf, cols] = hf.astype(hf_ref.dtype)
            hb = _mlstm_chunk(qb_ref[rb, cols], kb_ref[rb, cols], ktb_ref[cols, rb],
                              vb_ref[rb, cols], gs_ref.at[hh, 1], nc - 1 - (cps * c + sub),
                              ct_ref.at[hh, 1], n_ref.at[hh, 1], reverse=True)
            hb_ref[rb, cols] = hb.astype(hb_ref.dtype)


def _mlstm(q, k, kt, v, gates4, bsz, seq, hp=MLSTM_HEADS_PER_STEP, cps=MLSTM_CHUNKS_PER_STEP):
    t = q.shape[0]
    L = MCHUNK
    nc = seq // L
    dh = M_HEAD_DIM
    width = hp * dh
    rows = cps * L
    ns = nc // cps

    qkv_f = pl.BlockSpec((rows, width), lambda b, h, c: (b * ns + c, h))
    qkv_b = pl.BlockSpec((rows, width), lambda b, h, c: (b * ns + ns - 1 - c, h))
    kt_f = pl.BlockSpec((width, rows), lambda b, h, c: (h, b * ns + c))
    kt_b = pl.BlockSpec((width, rows), lambda b, h, c: (h, b * ns + ns - 1 - c))
    g_spec = pl.BlockSpec((None, hp, 4, nc, L), lambda b, h, c: (b, h, 0, 0, 0))
    out = jax.ShapeDtypeStruct((t, M_WIDTH), BF16)
    return pl.pallas_call(
        functools.partial(_mlstm_kernel, hp=hp, nc=nc, cps=cps),
        grid=(bsz, M_HEADS // hp, ns),
        in_specs=[g_spec, qkv_f, qkv_f, kt_f, qkv_f, qkv_b, qkv_b, kt_b, qkv_b],
        out_specs=[qkv_f, qkv_b],
        out_shape=[out, out],
        scratch_shapes=[
            pltpu.VMEM((hp, 2, dh, dh), F32),
            pltpu.VMEM((hp, 2, 1, dh), F32),
            pltpu.VMEM((hp, 2, GS_ROWS, nc, L), F32),
            pltpu.VMEM((hp, 2, 3, nc, LANES), F32),
        ],
        compiler_params=pltpu.CompilerParams(
            dimension_semantics=("parallel", "parallel", "arbitrary"),
            vmem_limit_bytes=VMEM_LIMIT),
        name="mlstm",
    )(gates4, q, k, kt, v, q, k, kt, v)


def _merge_kernel(hf_ref, hb_ref, xc_ref, yf_ref, x_ref, nw_ref, wzf_ref, wzm_ref, wom_ref, wg_ref,
                  hn_ref, sk_ref, wf_ref, wm_ref, wo_ref, fn_ref, o_ref, *, final_norm, sub_rows):
    for r in range(o_ref.shape[0] // sub_rows):
        rows = slice(sub_rows * r, sub_rows * (r + 1))
        hin = _rmsnorm_bf16(x_ref[rows, :], nw_ref[...])
        y_b = None
        for h in range(M_HEADS):
            cols = slice(M_HEAD_DIM * h, M_HEAD_DIM * (h + 1))
            hc2 = (hf_ref[rows, cols].astype(F32) + hb_ref[rows, cols].astype(F32))
            hc2 = hc2 * (1.0 + jnp.tanh(_dot(hin, wom_ref[:, cols])))
            hc = hc2 * lax.rsqrt(jnp.mean(hc2 * hc2, axis=-1, keepdims=True) + 4.0 * EPS)
            u = hc * hn_ref[:, cols] + sk_ref[:, cols] * xc_ref[rows, cols].astype(F32)
            hz = _dot(hin, wzm_ref[:, cols])
            u = u * (hz * (1.0 + jnp.tanh(hz)))
            part = _dot(u.astype(BF16), wm_ref[cols, :])
            y_b = part if y_b is None else y_b + part
        hzf = _dot(hin, wzf_ref[...])
        ya_in = yf_ref[rows, :].astype(F32) * (hzf * (1.0 + jnp.tanh(hzf)))
        y_a = _dot(ya_in.astype(BF16), wf_ref[...])
        t_a = jnp.tanh(_dot(hin, wg_ref[:, :D_MODEL]))
        t_b = jnp.tanh(_dot(hin, wg_ref[:, D_MODEL:]))
        merged2 = (y_a + y_b) + (t_a * y_a + t_b * y_b)
        xn = x_ref[rows, :] + _dot(merged2.astype(BF16), wo_ref[...])
        if final_norm:
            xn = xn * lax.rsqrt(jnp.mean(xn * xn, axis=-1, keepdims=True) + EPS) * fn_ref[...]
        o_ref[rows, :] = xn


def _merge(hf, hb, xc, yf, x2, nw, w_gate, hn, sk, wf, wm, wo, fn, final_norm,
           tm=512, sub_rows=512):
    t, d = x2.shape

    def rows(width):
        return pl.BlockSpec((tm, width), lambda i: (i, 0))

    def full(a):
        return _resident(a.shape, lambda i: (0,) * a.ndim)

    def w_gate_cols(col, width):
        return _resident((d, width), lambda i: (0, col // width))

    return pl.pallas_call(
        functools.partial(_merge_kernel, final_norm=final_norm, sub_rows=sub_rows),
        grid=(t // tm,),
        in_specs=[
            rows(M_WIDTH), rows(M_WIDTH), rows(M_WIDTH), rows(F_WIDTH), rows(D_MODEL),
            full(nw),
            w_gate_cols(GCOL_ZF, F_WIDTH), w_gate_cols(GCOL_ZM, M_WIDTH),
            w_gate_cols(GCOL_OM, M_WIDTH), w_gate_cols(GCOL_G, 2 * D_MODEL),
            full(hn), full(sk), full(wf), full(wm), full(wo), full(fn),
        ],
        out_specs=rows(D_MODEL),
        out_shape=jax.ShapeDtypeStruct((t, D_MODEL), F32),
        compiler_params=pltpu.CompilerParams(
            dimension_semantics=("parallel",),
            vmem_limit_bytes=VMEM_LIMIT),
        name="merge_out",
    )(hf, hb, xc, yf, x2, nw, w_gate, w_gate, w_gate, w_gate, hn, sk, wf, wm, wo, fn)


def _block_diag_tiles(w):
    rows = w.reshape(-1, MXU_DIM, QKV_BLOCK)
    dense = jnp.tile(rows, (1, 1, MXU_DIM // QKV_BLOCK))
    r = lax.broadcasted_iota(jnp.int32, (MXU_DIM, MXU_DIM), 0) // QKV_BLOCK
    c = lax.broadcasted_iota(jnp.int32, (MXU_DIM, MXU_DIM), 1) // QKV_BLOCK
    return jnp.where(r == c, dense, 0.0).astype(BF16)


def kernel(x, norm_w, w_in, w_fourier, conv_w, conv_b, w_q, w_k, w_v,
           w_igate_fwd, b_igate_fwd, w_fgate_fwd, b_fgate_fwd,
           w_igate_bwd, b_igate_bwd, w_fgate_bwd, b_fgate_bwd,
           hnorm_w, skip_w, w_mlstm, w_out, final_norm_w):
    bsz, seq, d = x.shape
    depth = w_in.shape[0]
    assert d == D_MODEL and seq % (FFT_N2 * BF16_SUBLANES * 2) == 0 and seq % MCHUNK == 0
    t = bsz * seq
    nc = seq // MCHUNK
    consts = tuple(jnp.asarray(a, dtype=F32).astype(BF16) for a in _fft_constants(seq))
    fn = final_norm_w.reshape(1, d)

    x2 = x.reshape(t, d)
    for l in range(depth):
        w_gate = jnp.concatenate([w_in[l, :, COL_ZM:], w_in[l, :, COL_ZF:COL_XM]], axis=1)
        w_gate = (0.5 * w_gate).astype(BF16)
        nw = norm_w[l].reshape(1, d)

        wg = jnp.concatenate([w_igate_fwd[l], w_fgate_fwd[l], w_igate_bwd[l], w_fgate_bwd[l]], axis=1)
        wg = jnp.pad(wg, ((0, 0), (0, GATE_LANES - wg.shape[1])))
        wg = wg.reshape(3, M_WIDTH // MXU_DIM, MXU_DIM, GATE_LANES)
        qscale = M_HEAD_DIM ** -0.5
        wg = (wg * jnp.asarray([1.0 / qscale, 1.0, 1.0], F32).reshape(3, 1, 1, 1)).astype(BF16)
        bg = jnp.concatenate([b_igate_fwd[l], b_fgate_fwd[l], b_igate_bwd[l], b_fgate_bwd[l]])
        bg = jnp.pad(bg, (0, GATE_LANES - bg.shape[0])).reshape(1, GATE_LANES)
        wk_tiles = _block_diag_tiles(w_k[l])
        wkt_tiles = _block_diag_tiles(jnp.swapaxes(w_k[l], -1, -2))
        xf, xm = _inproj(x2, nw, w_in, l)
        q, k, kt, v, xc, gates = _conv_qkv(
            xm, seq, 0.5 * conv_w[l], 0.5 * conv_b[l].reshape(1, M_WIDTH),
            _block_diag_tiles(w_q[l] * qscale), wk_tiles, wkt_tiles,
            _block_diag_tiles(w_v[l]), wg, bg)

        yt = _fourier_mix(xf, bsz, seq, consts)
        yf = jnp.transpose(yt, (0, 2, 1, 3)).reshape(t, F_WIDTH)

        g4 = gates[:, :4 * M_HEADS].reshape(bsz, nc, MCHUNK, 4, M_HEADS)
        g4 = jnp.transpose(g4, (0, 4, 3, 1, 2))
        hf, hb = _mlstm(q, k, kt, v, g4, bsz, seq)

        x2 = _merge(hf, hb, xc, yf, x2, nw, w_gate,
                    hnorm_w[l].reshape(1, M_WIDTH), skip_w[l].reshape(1, M_WIDTH),
                    w_fourier[l].astype(BF16), w_mlstm[l].astype(BF16),
                    (0.5 * w_out[l]).astype(BF16),
                    fn, final_norm=(l == depth - 1))
    return x2.reshape(bsz, seq, d)
```

```python
import functools

import numpy as np
import jax
import jax.numpy as jnp
from jax import lax
from jax.experimental import pallas as pl
from jax.experimental.pallas import tpu as pltpu

F32 = jnp.float32
BF16 = jnp.bfloat16

D_MODEL = 1024
F_WIDTH = D_MODEL
F_GROUPS = 4
F_GROUP_DIM = F_WIDTH // F_GROUPS
M_WIDTH = 2 * D_MODEL
M_HEADS = 4
M_HEAD_DIM = M_WIDTH // M_HEADS
QKV_BLOCK = 4
CONV_K = 5
EPS = 1e-6
IN_COLS = 2 * F_WIDTH + 3 * M_WIDTH + 2 * D_MODEL

MCHUNK = 256
MLSTM_HEADS_PER_STEP = 2
MLSTM_CHUNKS_PER_STEP = 2

COL_XF = 0
COL_ZF = F_WIDTH
COL_XM = 2 * F_WIDTH
COL_ZM = COL_XM + M_WIDTH
COL_OM = COL_ZM + M_WIDTH
COL_G = COL_OM + M_WIDTH
GCOL_ZM = 0
GCOL_OM = M_WIDTH
GCOL_G = 2 * M_WIDTH

LANES = 128
SUBLANES = 8
BF16_SUBLANES = 16
MXU_DIM = 256
GATE_LANES = 128
VMEM_LIMIT = 56 * 1024 * 1024

FFT_N2 = 32


def _dot(a, b):
    return jnp.dot(a, b, preferred_element_type=F32)


def _dot_nt(a, b):
    return lax.dot_general(a, b, (((1,), (1,)), ((), ())), preferred_element_type=F32)


def _rmsnorm_bf16(x, w):
    ms = jnp.mean(x * x, axis=-1, keepdims=True)
    return (x * lax.rsqrt(ms + EPS) * w).astype(BF16)


def _resident(shape, index_map):
    return pl.BlockSpec(shape, index_map, pipeline_mode=pl.Buffered(1))


def _inproj_kernel(x_ref, nw_ref, wf_ref, wm_ref, xf_ref, xm_ref, *, sub_rows):
    wf = wf_ref[...].astype(BF16)
    wm = wm_ref[...].astype(BF16)
    for r in range(x_ref.shape[0] // sub_rows):
        rows = slice(sub_rows * r, sub_rows * (r + 1))
        h = _rmsnorm_bf16(x_ref[rows, :], nw_ref[...])
        xf_ref[rows, :] = _dot(h, wf).astype(BF16)
        xm_ref[rows, :] = _dot(h, wm).astype(BF16)


def _inproj(x2, norm_w, w_in, layer, tm=512, sub_rows=256):
    t, d = x2.shape
    return pl.pallas_call(
        functools.partial(_inproj_kernel, sub_rows=sub_rows),
        grid=(t // tm,),
        in_specs=[
            pl.BlockSpec((tm, d), lambda i: (i, 0)),
            pl.BlockSpec((1, d), lambda i: (0, 0)),
            _resident((None, d, F_WIDTH), lambda i: (layer, 0, COL_XF // F_WIDTH)),
            _resident((None, d, M_WIDTH), lambda i: (layer, 0, COL_XM // M_WIDTH)),
        ],
        out_specs=[pl.BlockSpec((tm, F_WIDTH), lambda i: (i, 0)),
                   pl.BlockSpec((tm, M_WIDTH), lambda i: (i, 0))],
        out_shape=[jax.ShapeDtypeStruct((t, F_WIDTH), BF16),
                   jax.ShapeDtypeStruct((t, M_WIDTH), BF16)],
        compiler_params=pltpu.CompilerParams(
            dimension_semantics=("parallel",),
            vmem_limit_bytes=VMEM_LIMIT),
        name="inproj",
    )(x2, norm_w, w_in, w_in)


def _fft_constants(seq):
    n2 = FFT_N2
    n1 = seq // n2
    sub = BF16_SUBLANES
    nblk = n1 // sub
    i = np.arange(nblk)[:, None, None, None]
    k2 = np.arange(n2)[None, :, None, None]
    j = np.arange(sub)[None, None, :, None]
    m2 = np.arange(n2)[None, None, None, :]
    ang = 2.0 * np.pi * (((m2 * k2) % n2) / n2 + (((sub * i + j) * k2) % seq) / seq)
    val = np.stack([np.cos(ang), -np.sin(ang)], axis=1)
    wa = np.zeros((nblk, 2, n2, sub, n2, sub), np.float64)
    for jj in range(sub):
        wa[:, :, :, jj, :, jj] = val[:, :, :, jj, :]
    wa = wa.reshape(nblk, 2 * n2 * sub, n2 * sub)
    a = np.arange(n1)
    angb = 2.0 * np.pi * ((a[:, None] * a[None, :]) % n1) / n1
    wr, wi = np.cos(angb), -np.sin(angb)
    wb = np.block([[wr, -wi], [wi, wr]])
    c = np.arange(F_GROUP_DIM)
    angc = 2.0 * np.pi * ((c[:, None] * c[None, :]) % F_GROUP_DIM) / F_GROUP_DIM
    wc = np.concatenate([np.cos(angc), np.sin(angc)], axis=0) / np.sqrt(seq * F_GROUP_DIM)
    return wa, wb, wc


def _fft_a_kernel(x_ref, wa_ref, o_ref, *, nsub):
    n2 = FFT_N2
    sub = BF16_SUBLANES
    c = x_ref.shape[-1]
    for i in range(nsub):
        rows = slice(sub * i, sub * (i + 1))
        xs = x_ref[:, rows, :].reshape(n2 * sub, c)
        z = _dot(wa_ref[i], xs)
        o_ref[:, :, rows, :] = z.astype(BF16).reshape(2, n2, sub, c)


def _fft_b_kernel(z_ref, wb_ref, wc_ref, o_ref):
    nk, n1, c = z_ref.shape[1:]
    for kk in range(nk):
        xs = z_ref[:, kk].reshape(2 * n1, c)
        g = _dot(wb_ref[...], xs).astype(BF16)
        for grp in range(c // F_GROUP_DIM):
            cols = slice(F_GROUP_DIM * grp, F_GROUP_DIM * (grp + 1))
            lhs = jnp.concatenate([g[:n1, cols], g[n1:, cols]], axis=1)
            o_ref[kk, :, cols] = _dot(lhs, wc_ref[...]).astype(BF16)


def _fourier_mix(xf, bsz, seq, consts, nsub=4, kb=4):
    wa, wb, wc = consts
    n2 = FFT_N2
    n1 = seq // n2
    rblk = nsub * BF16_SUBLANES
    proj4 = xf.reshape(bsz, n2, n1, F_WIDTH)
    za = pl.pallas_call(
        functools.partial(_fft_a_kernel, nsub=nsub),
        grid=(n1 // rblk, bsz),
        in_specs=[
            pl.BlockSpec((None, n2, rblk, F_WIDTH), lambda i, b: (b, 0, i, 0)),
            pl.BlockSpec((nsub,) + wa.shape[1:], lambda i, b: (i, 0, 0)),
        ],
        out_specs=pl.BlockSpec((None, 2, n2, rblk, F_WIDTH), lambda i, b: (b, 0, 0, i, 0)),
        out_shape=jax.ShapeDtypeStruct((bsz, 2, n2, n1, F_WIDTH), BF16),
        compiler_params=pltpu.CompilerParams(
            dimension_semantics=("parallel", "parallel"),
            vmem_limit_bytes=VMEM_LIMIT),
        name="fft_stage_a",
    )(proj4, wa)
    yt = pl.pallas_call(
        _fft_b_kernel,
        grid=(bsz, n2 // kb),
        in_specs=[
            pl.BlockSpec((None, 2, kb, n1, F_WIDTH), lambda b, k: (b, 0, k, 0, 0)),
            _resident(wb.shape, lambda b, k: (0, 0)),
            _resident(wc.shape, lambda b, k: (0, 0)),
        ],
        out_specs=pl.BlockSpec((None, kb, n1, F_WIDTH), lambda b, k: (b, k, 0, 0)),
        out_shape=jax.ShapeDtypeStruct((bsz, n2, n1, F_WIDTH), BF16),
        compiler_params=pltpu.CompilerParams(
            dimension_semantics=("parallel", "parallel"),
            vmem_limit_bytes=VMEM_LIMIT),
        name="fft_stage_b",
    )(za, wb, wc)
    return yt


def _log_sigmoid(x):
    return jnp.minimum(x, 0.0) - jnp.log1p(jnp.exp(-jnp.abs(x)))


def _conv_qkv_kernel(prev_ref, cur_ref, next_ref, cw_ref, cb_ref, wq_ref, wk_ref, wkt_ref, wv_ref,
                     wg_ref, bg_ref, q_ref, k_ref, kt_ref, v_ref, xc_ref, g_ref, xs_ref,
                     *, tiles_per_seq):
    tm = cur_ref.shape[0]
    halo = BF16_SUBLANES
    pos = pl.program_id(0) % tiles_per_seq
    keep_prev = (pos != 0).astype(F32)
    keep_next = (pos != tiles_per_seq - 1).astype(F32)
    xs_ref[0:halo, :] = prev_ref[...].astype(F32) * keep_prev
    xs_ref[halo:halo + tm, :] = cur_ref[...].astype(F32)
    xs_ref[halo + tm:, :] = next_ref[...].astype(F32) * keep_next

    gacc = jnp.zeros((tm, GATE_LANES), F32)
    for g in range(M_WIDTH // MXU_DIM):
        cols = slice(MXU_DIM * g, MXU_DIM * (g + 1))
        acc = jnp.broadcast_to(cb_ref[:, cols], (tm, MXU_DIM))
        ext = xs_ref[halo - SUBLANES:halo + tm + SUBLANES, cols]
        for j in range(CONV_K):
            shift = (CONV_K // 2 - j) % ext.shape[0]
            tap = ext if shift == 0 else pltpu.roll(ext, shift, axis=0)
            acc = acc + cw_ref[j:j + 1, cols] * tap[SUBLANES:SUBLANES + tm]
        xcb = (acc * (1.0 + jnp.tanh(acc))).astype(BF16)
        q = _dot(xcb, wq_ref[g])
        k = _dot(xcb, wk_ref[g])
        kt_ref[cols, :] = _dot_nt(wkt_ref[g], xcb).astype(BF16)
        v = _dot(cur_ref[:, cols], wv_ref[g])
        qb, kb, vb = q.astype(BF16), k.astype(BF16), v.astype(BF16)
        gacc = gacc + _dot(qb, wg_ref[0, g]) + _dot(kb, wg_ref[1, g]) + _dot(vb, wg_ref[2, g])
        q_ref[:, cols] = qb
        k_ref[:, cols] = kb
        v_ref[:, cols] = vb
        xc_ref[:, cols] = xcb
    gates = gacc + bg_ref[...]
    lane = lax.broadcasted_iota(jnp.int32, gates.shape, 1)
    is_forget = (lane // M_HEADS) % 2 == 1
    g_ref[...] = jnp.where(is_forget, _log_sigmoid(gates), gates)


def _conv_qkv(xm, seq, cw, cb, wq, wk, wkt, wv, wg, bg, tm=256):
    t = xm.shape[0]
    halo = BF16_SUBLANES
    hb = tm // halo
    nhalo = t // halo
    out_bf = jax.ShapeDtypeStruct((t, M_WIDTH), BF16)
    row_spec = pl.BlockSpec((tm, M_WIDTH), lambda i: (i, 0))

    def full(a):
        return _resident(a.shape, lambda i: (0,) * a.ndim)

    return pl.pallas_call(
        functools.partial(_conv_qkv_kernel, tiles_per_seq=seq // tm),
        grid=(t // tm,),
        in_specs=[
            pl.BlockSpec((halo, M_WIDTH), lambda i: (jnp.maximum(i * hb - 1, 0), 0)),
            row_spec,
            pl.BlockSpec((halo, M_WIDTH), lambda i: (jnp.minimum((i + 1) * hb, nhalo - 1), 0)),
            full(cw), full(cb), full(wq), full(wk), full(wkt), full(wv), full(wg), full(bg),
        ],
        out_specs=[row_spec, row_spec, pl.BlockSpec((M_WIDTH, tm), lambda i: (0, i)),
                   row_spec, row_spec,
                   pl.BlockSpec((tm, GATE_LANES), lambda i: (i, 0))],
        out_shape=[out_bf, out_bf, jax.ShapeDtypeStruct((M_WIDTH, t), BF16), out_bf, out_bf,
                   jax.ShapeDtypeStruct((t, GATE_LANES), F32)],
        scratch_shapes=[pltpu.VMEM((tm + 2 * halo, M_WIDTH), F32)],
        compiler_params=pltpu.CompilerParams(
            dimension_semantics=("parallel",),
            vmem_limit_bytes=VMEM_LIMIT),
        name="conv_qkv_gates",
    )(xm, xm, xm, cw, cb, wq, wk, wkt, wv, wg, bg)


GS_C, GS_M, GS_INTER, GS_ENEGM, GS_WS, GS_DECAY, GS_ROWS = 0, 1, 2, 3, 4, 5, 6


def _lane_scan(x, op, fill, reverse):
    n = x.shape[1]
    lane = lax.broadcasted_iota(jnp.int32, x.shape, 1)
    sh = 1
    while sh < n:
        if reverse:
            x = op(x, jnp.where(lane < n - sh, pltpu.roll(x, n - sh, axis=1), fill))
        else:
            x = op(x, jnp.where(lane >= sh, pltpu.roll(x, sh, axis=1), fill))
        sh *= 2
    return x


def _gate_prep(i_pre, log_f, gs_ref, tmp_ref, *, reverse):
    nc, L = i_pre.shape
    b = _lane_scan(log_f, jnp.add, 0.0, reverse)
    c = i_pre - b
    cm = _lane_scan(c, jnp.maximum, -jnp.inf, reverse)
    end = slice(0, 1) if reverse else slice(L - 1, L)
    g = b[:, end]
    cmt = cm[:, end]
    tmp_ref[0] = jnp.broadcast_to(g, (nc, LANES))
    tmp_ref[1] = jnp.broadcast_to(g + cmt, (nc, LANES))

    m = jnp.zeros((1, LANES), F32)
    for step in range(nc):
        idx = nc - 1 - step if reverse else step
        tmp_ref[2, idx:idx + 1, :] = m
        m = jnp.maximum(m + tmp_ref[0, idx:idx + 1, :], tmp_ref[1, idx:idx + 1, :])
    m_prev = tmp_ref[2][:, 0:1]
    big_m = jnp.maximum(m_prev, cm)
    mx = jnp.maximum(m_prev, cmt)
    gs_ref[GS_C] = c
    gs_ref[GS_M] = big_m
    gs_ref[GS_INTER] = jnp.exp(m_prev - big_m)
    gs_ref[GS_ENEGM] = jnp.exp(-b - big_m)
    gs_ref[GS_WS] = jnp.exp(c - mx)
    gs_ref[GS_DECAY] = jnp.broadcast_to(jnp.exp(m_prev - mx), (nc, L))


def _mlstm_chunk(q, k, kt, v, gs_ref, ci, ct_ref, n_ref, *, reverse):
    L = q.shape[0]

    def row(r):
        return gs_ref[r, pl.ds(ci, 1), :]

    c_row = row(GS_C)
    sub = lax.broadcasted_iota(jnp.int32, (SUBLANES, L), 0)
    packed = jnp.where(sub == 0, row(GS_M),
                       jnp.where(sub == 1, row(GS_INTER),
                                 jnp.where(sub == 2, row(GS_ENEGM), row(GS_WS))))
    tr = jnp.concatenate([packed, jnp.zeros((LANES - SUBLANES, L), F32)], axis=0).T
    m_col, inter, enegm, ws = tr[:, 0:1], tr[:, 1:2], tr[:, 2:3], tr[:, 3:4]
    decay = row(GS_DECAY)[:, 0:1]

    r_i = lax.broadcasted_iota(jnp.int32, (L, L), 0)
    c_i = lax.broadcasted_iota(jnp.int32, (L, L), 1)
    mask = (c_i >= r_i) if reverse else (c_i <= r_i)
    p = jnp.exp(jnp.where(mask, c_row - m_col, -jnp.inf))
    s = _dot_nt(q, k) * p
    qi = inter * q.astype(F32)
    num = _dot(s.astype(BF16), v) + _dot(qi.astype(BF16), ct_ref[...].astype(BF16))
    den = jnp.sum(s, axis=1, keepdims=True) + jnp.sum(qi * n_ref[...], axis=1, keepdims=True)
    h = num * (1.0 / jnp.maximum(jnp.abs(den), enegm))

    vw = ws.astype(BF16) * v
    ct_ref[...] = decay * ct_ref[...] + _dot(kt, vw)
    n_ref[...] = decay * n_ref[...] + jnp.sum(ws * k.astype(F32), axis=0, keepdims=True)
    return h


def _mlstm_kernel(gates_ref, qf_ref, kf_ref, ktf_ref, vf_ref, qb_ref, kb_ref, ktb_ref, vb_ref,
                  hf_ref, hb_ref,
                  ct_ref, n_ref, gs_ref, tmp_ref, *, hp, nc, cps):
    c = pl.program_id(2)
    dh = M_HEAD_DIM
    L = MCHUNK

    @pl.when(c == 0)
    def _():
        ct_ref[...] = jnp.zeros_like(ct_ref)
        n_ref[...] = jnp.zeros_like(n_ref)
        for hh in range(hp):
            _gate_prep(gates_ref[hh, 0], gates_ref[hh, 1], gs_ref.at[hh, 0], tmp_ref.at[hh, 0],
                       reverse=False)
            _gate_prep(gates_ref[hh, 2], gates_ref[hh, 3], gs_ref.at[hh, 1], tmp_ref.at[hh, 1],
                       reverse=True)

    for sub in range(cps):
        rf = slice(L * sub, L * (sub + 1))
        rb = slice(L * (cps - 1 - sub), L * (cps - sub))
        for hh in range(hp):
            cols = slice(dh * hh, dh * (hh + 1))
            hf = _mlstm_chunk(qf_ref[rf, cols], kf_ref[rf, cols], ktf_ref[cols, rf],
                              vf_ref[rf, cols], gs_ref.at[hh, 0], cps * c + sub,
                              ct_ref.at[hh, 0], n_ref.at[hh, 0], reverse=False)
            hf_ref[rf, cols] = hf.astype(hf_ref.dtype)
            hb = _mlstm_chunk(qb_ref[rb, cols], kb_ref[rb, cols], ktb_ref[cols, rb],
                              vb_ref[rb, cols], gs_ref.at[hh, 1], nc - 1 - (cps * c + sub),
                              ct_ref.at[hh, 1], n_ref.at[hh, 1], reverse=True)
            hb_ref[rb, cols] = hb.astype(hb_ref.dtype)


def _mlstm(q, k, kt, v, gates4, bsz, seq, hp=MLSTM_HEADS_PER_STEP, cps=MLSTM_CHUNKS_PER_STEP):
    t = q.shape[0]
    L = MCHUNK
    nc = seq // L
    dh = M_HEAD_DIM
    width = hp * dh
    rows = cps * L
    ns = nc // cps

    qkv_f = pl.BlockSpec((rows, width), lambda b, h, c: (b * ns + c, h))
    qkv_b = pl.BlockSpec((rows, width), lambda b, h, c: (b * ns + ns - 1 - c, h))
    kt_f = pl.BlockSpec((width, rows), lambda b, h, c: (h, b * ns + c))
    kt_b = pl.BlockSpec((width, rows), lambda b, h, c: (h, b * ns + ns - 1 - c))
    g_spec = pl.BlockSpec((None, hp, 4, nc, L), lambda b, h, c: (b, h, 0, 0, 0))
    out = jax.ShapeDtypeStruct((t, M_WIDTH), BF16)
    return pl.pallas_call(
        functools.partial(_mlstm_kernel, hp=hp, nc=nc, cps=cps),
        grid=(bsz, M_HEADS // hp, ns),
        in_specs=[g_spec, qkv_f, qkv_f, kt_f, qkv_f, qkv_b, qkv_b, kt_b, qkv_b],
        out_specs=[qkv_f, qkv_b],
        out_shape=[out, out],
        scratch_shapes=[
            pltpu.VMEM((hp, 2, dh, dh), F32),
            pltpu.VMEM((hp, 2, 1, dh), F32),
            pltpu.VMEM((hp, 2, GS_ROWS, nc, L), F32),
            pltpu.VMEM((hp, 2, 3, nc, LANES), F32),
        ],
        compiler_params=pltpu.CompilerParams(
            dimension_semantics=("parallel", "parallel", "arbitrary"),
            vmem_limit_bytes=VMEM_LIMIT),
        name="mlstm",
    )(gates4, q, k, kt, v, q, k, kt, v)


def _merge_kernel(hf_ref, hb_ref, xc_ref, yf_ref, x_ref, nw_ref, wzf_ref, wzm_ref, wom_ref, wg_ref,
                  hn_ref, sk_ref, wf_ref, wm_ref, wo_ref, fn_ref, o_ref, *, final_norm, sub_rows):
    for r in range(o_ref.shape[0] // sub_rows):
        rows = slice(sub_rows * r, sub_rows * (r + 1))
        hin = _rmsnorm_bf16(x_ref[rows, :], nw_ref[...])
        y_b = None
        for h in range(M_HEADS):
            cols = slice(M_HEAD_DIM * h, M_HEAD_DIM * (h + 1))
            hc2 = (hf_ref[rows, cols].astype(F32) + hb_ref[rows, cols].astype(F32))
            hc2 = hc2 * (1.0 + jnp.tanh(_dot(hin, wom_ref[:, cols])))
            hc = hc2 * lax.rsqrt(jnp.mean(hc2 * hc2, axis=-1, keepdims=True) + 4.0 * EPS)
            u = hc * hn_ref[:, cols] + sk_ref[:, cols] * xc_ref[rows, cols].astype(F32)
            hz = _dot(hin, wzm_ref[:, cols])
            u = u * (hz * (1.0 + jnp.tanh(hz)))
            part = _dot(u.astype(BF16), wm_ref[cols, :])
            y_b = part if y_b is None else y_b + part
        hzf = _dot(hin, wzf_ref[...])
        ya_in = yf_ref[rows, :].astype(F32) * (hzf * (1.0 + jnp.tanh(hzf)))
        y_a = _dot(ya_in.astype(BF16), wf_ref[...])
        t_a = jnp.tanh(_dot(hin, wg_ref[:, :D_MODEL]))
        t_b = jnp.tanh(_dot(hin, wg_ref[:, D_MODEL:]))
        merged2 = (y_a + y_b) + (t_a * y_a + t_b * y_b)
        xn = x_ref[rows, :] + _dot(merged2.astype(BF16), wo_ref[...])
        if final_norm:
            xn = xn * lax.rsqrt(jnp.mean(xn * xn, axis=-1, keepdims=True) + EPS) * fn_ref[...]
        o_ref[rows, :] = xn


def _merge(hf, hb, xc, yf, x2, nw, w_zf, w_gate, hn, sk, wf, wm, wo, fn, final_norm,
           tm=512, sub_rows=512):
    t, d = x2.shape

    def rows(width):
        return pl.BlockSpec((tm, width), lambda i: (i, 0))

    def full(a):
        return _resident(a.shape, lambda i: (0,) * a.ndim)

    def w_gate_cols(col, width):
        return _resident((d, width), lambda i: (0, col // width))

    return pl.pallas_call(
        functools.partial(_merge_kernel, final_norm=final_norm, sub_rows=sub_rows),
        grid=(t // tm,),
        in_specs=[
            rows(M_WIDTH), rows(M_WIDTH), rows(M_WIDTH), rows(F_WIDTH), rows(D_MODEL),
            full(nw),
            full(w_zf), w_gate_cols(GCOL_ZM, M_WIDTH),
            w_gate_cols(GCOL_OM, M_WIDTH), w_gate_cols(GCOL_G, 2 * D_MODEL),
            full(hn), full(sk), full(wf), full(wm), full(wo), full(fn),
        ],
        out_specs=rows(D_MODEL),
        out_shape=jax.ShapeDtypeStruct((t, D_MODEL), F32),
        compiler_params=pltpu.CompilerParams(
            dimension_semantics=("parallel",),
            vmem_limit_bytes=VMEM_LIMIT),
        name="merge_out",
    )(hf, hb, xc, yf, x2, nw, w_zf, w_gate, w_gate, w_gate, hn, sk, wf, wm, wo, fn)


def _block_diag_tiles(w):
    rows = w.reshape(-1, MXU_DIM, QKV_BLOCK)
    dense = jnp.tile(rows, (1, 1, MXU_DIM // QKV_BLOCK))
    r = lax.broadcasted_iota(jnp.int32, (MXU_DIM, MXU_DIM), 0) // QKV_BLOCK
    c = lax.broadcasted_iota(jnp.int32, (MXU_DIM, MXU_DIM), 1) // QKV_BLOCK
    return jnp.where(r == c, dense, 0.0).astype(BF16)


def kernel(x, norm_w, w_in, w_fourier, conv_w, conv_b, w_q, w_k, w_v,
           w_igate_fwd, b_igate_fwd, w_fgate_fwd, b_fgate_fwd,
           w_igate_bwd, b_igate_bwd, w_fgate_bwd, b_fgate_bwd,
           hnorm_w, skip_w, w_mlstm, w_out, final_norm_w):
    bsz, seq, d = x.shape
    depth = w_in.shape[0]
    assert d == D_MODEL and seq % (FFT_N2 * BF16_SUBLANES * 2) == 0 and seq % MCHUNK == 0
    t = bsz * seq
    nc = seq // MCHUNK
    consts = tuple(jnp.asarray(a, dtype=F32).astype(BF16) for a in _fft_constants(seq))
    fn = final_norm_w.reshape(1, d)

    x2 = x.reshape(t, d)
    for l in range(depth):
        w_gate = (0.5 * w_in[l, :, COL_ZM:]).astype(BF16)
        w_zf = (0.5 * w_in[l, :, COL_ZF:COL_XM]).astype(BF16)
        nw = norm_w[l].reshape(1, d)

        wg = jnp.concatenate([w_igate_fwd[l], w_fgate_fwd[l], w_igate_bwd[l], w_fgate_bwd[l]], axis=1)
        wg = jnp.pad(wg, ((0, 0), (0, GATE_LANES - wg.shape[1])))
        wg = wg.reshape(3, M_WIDTH // MXU_DIM, MXU_DIM, GATE_LANES)
        qscale = M_HEAD_DIM ** -0.5
        wg = (wg * jnp.asarray([1.0 / qscale, 1.0, 1.0], F32).reshape(3, 1, 1, 1)).astype(BF16)
        bg = jnp.concatenate([b_igate_fwd[l], b_fgate_fwd[l], b_igate_bwd[l], b_fgate_bwd[l]])
        bg = jnp.pad(bg, (0, GATE_LANES - bg.shape[0])).reshape(1, GATE_LANES)
        wk_tiles = _block_diag_tiles(w_k[l])
        wkt_tiles = _block_diag_tiles(jnp.swapaxes(w_k[l], -1, -2))
        xf, xm = _inproj(x2, nw, w_in, l)
        q, k, kt, v, xc, gates = _conv_qkv(
            xm, seq, 0.5 * conv_w[l], 0.5 * conv_b[l].reshape(1, M_WIDTH),
            _block_diag_tiles(w_q[l] * qscale), wk_tiles, wkt_tiles,
            _block_diag_tiles(w_v[l]), wg, bg)

        yt = _fourier_mix(xf, bsz, seq, consts)
        yf = jnp.transpose(yt, (0, 2, 1, 3)).reshape(t, F_WIDTH)

        g4 = gates[:, :4 * M_HEADS].reshape(bsz, nc, MCHUNK, 4, M_HEADS)
        g4 = jnp.transpose(g4, (0, 4, 3, 1, 2))
        hf, hb = _mlstm(q, k, kt, v, g4, bsz, seq)

        x2 = _merge(hf, hb, xc, yf, x2, nw, w_zf, w_gate,
                    hnorm_w[l].reshape(1, M_WIDTH), skip_w[l].reshape(1, M_WIDTH),
                    w_fourier[l].astype(BF16), w_mlstm[l].astype(BF16),
                    (0.5 * w_out[l]).astype(BF16),
                    fn, final_norm=(l == depth - 1))
    return x2.reshape(bsz, seq, d)
```

```python
import functools

import numpy as np
import jax
import jax.numpy as jnp
from jax import lax
from jax.experimental import pallas as pl
from jax.experimental.pallas import tpu as pltpu

F32 = jnp.float32
BF16 = jnp.bfloat16

D_MODEL = 1024
F_WIDTH = D_MODEL
F_GROUPS = 4
F_GROUP_DIM = F_WIDTH // F_GROUPS
M_WIDTH = 2 * D_MODEL
M_HEADS = 4
M_HEAD_DIM = M_WIDTH // M_HEADS
QKV_BLOCK = 4
CONV_K = 5
EPS = 1e-6
IN_COLS = 2 * F_WIDTH + 3 * M_WIDTH + 2 * D_MODEL

MCHUNK = 256
MLSTM_HEADS_PER_STEP = 2
MLSTM_CHUNKS_PER_STEP = 2

COL_XF = 0
COL_ZF = F_WIDTH
COL_XM = 2 * F_WIDTH
COL_ZM = COL_XM + M_WIDTH
COL_OM = COL_ZM + M_WIDTH
COL_G = COL_OM + M_WIDTH
GCOL_ZM = 0
GCOL_OM = M_WIDTH
GCOL_G = 2 * M_WIDTH

LANES = 128
SUBLANES = 8
BF16_SUBLANES = 16
MXU_DIM = 256
GATE_LANES = 128
VMEM_LIMIT = 56 * 1024 * 1024

FFT_N2 = 32


def _dot(a, b):
    return jnp.dot(a, b, preferred_element_type=F32)


def _dot_nt(a, b):
    return lax.dot_general(a, b, (((1,), (1,)), ((), ())), preferred_element_type=F32)


def _rmsnorm_bf16(x, w):
    ms = jnp.mean(x * x, axis=-1, keepdims=True)
    return (x * lax.rsqrt(ms + EPS) * w).astype(BF16)


def _resident(shape, index_map):
    return pl.BlockSpec(shape, index_map, pipeline_mode=pl.Buffered(1))


def _inproj_kernel(x_ref, nw_ref, wf_ref, wm_ref, xf_ref, xm_ref, *, sub_rows):
    wf = wf_ref[...].astype(BF16)
    wm = wm_ref[...].astype(BF16)
    for r in range(x_ref.shape[0] // sub_rows):
        rows = slice(sub_rows * r, sub_rows * (r + 1))
        h = _rmsnorm_bf16(x_ref[rows, :], nw_ref[...])
        xf_ref[rows, :] = _dot(h, wf).astype(BF16)
        xm_ref[rows, :] = _dot(h, wm).astype(BF16)


def _inproj(x2, norm_w, w_in, layer, tm=1024, sub_rows=256):
    t, d = x2.shape
    return pl.pallas_call(
        functools.partial(_inproj_kernel, sub_rows=sub_rows),
        grid=(t // tm,),
        in_specs=[
            pl.BlockSpec((tm, d), lambda i: (i, 0)),
            pl.BlockSpec((1, d), lambda i: (0, 0)),
            _resident((None, d, F_WIDTH), lambda i: (layer, 0, COL_XF // F_WIDTH)),
            _resident((None, d, M_WIDTH), lambda i: (layer, 0, COL_XM // M_WIDTH)),
        ],
        out_specs=[pl.BlockSpec((tm, F_WIDTH), lambda i: (i, 0)),
                   pl.BlockSpec((tm, M_WIDTH), lambda i: (i, 0))],
        out_shape=[jax.ShapeDtypeStruct((t, F_WIDTH), BF16),
                   jax.ShapeDtypeStruct((t, M_WIDTH), BF16)],
        compiler_params=pltpu.CompilerParams(
            dimension_semantics=("parallel",),
            vmem_limit_bytes=VMEM_LIMIT),
        name="inproj",
    )(x2, norm_w, w_in, w_in)


def _fft_constants(seq):
    n2 = FFT_N2
    n1 = seq // n2
    sub = BF16_SUBLANES
    nblk = n1 // sub
    i = np.arange(nblk)[:, None, None, None]
    k2 = np.arange(n2)[None, :, None, None]
    j = np.arange(sub)[None, None, :, None]
    m2 = np.arange(n2)[None, None, None, :]
    ang = 2.0 * np.pi * (((m2 * k2) % n2) / n2 + (((sub * i + j) * k2) % seq) / seq)
    val = np.stack([np.cos(ang), -np.sin(ang)], axis=1)
    wa = np.zeros((nblk, 2, n2, sub, n2, sub), np.float64)
    for jj in range(sub):
        wa[:, :, :, jj, :, jj] = val[:, :, :, jj, :]
    wa = wa.reshape(nblk, 2 * n2 * sub, n2 * sub)
    a = np.arange(n1)
    angb = 2.0 * np.pi * ((a[:, None] * a[None, :]) % n1) / n1
    wr, wi = np.cos(angb), -np.sin(angb)
    wb = np.block([[wr, -wi], [wi, wr]])
    c = np.arange(F_GROUP_DIM)
    angc = 2.0 * np.pi * ((c[:, None] * c[None, :]) % F_GROUP_DIM) / F_GROUP_DIM
    wc = np.concatenate([np.cos(angc), np.sin(angc)], axis=0) / np.sqrt(seq * F_GROUP_DIM)
    return wa, wb, wc


def _fft_a_kernel(x_ref, wa_ref, o_ref, *, nsub):
    n2 = FFT_N2
    sub = BF16_SUBLANES
    c = x_ref.shape[-1]
    for i in range(nsub):
        rows = slice(sub * i, sub * (i + 1))
        xs = x_ref[:, rows, :].reshape(n2 * sub, c)
        z = _dot(wa_ref[i], xs)
        o_ref[:, :, rows, :] = z.astype(BF16).reshape(2, n2, sub, c)


def _fft_b_kernel(z_ref, wb_ref, wc_ref, o_ref):
    nk, n1, c = z_ref.shape[1:]
    for kk in range(nk):
        xs = z_ref[:, kk].reshape(2 * n1, c)
        g = _dot(wb_ref[...], xs).astype(BF16)
        for grp in range(c // F_GROUP_DIM):
            cols = slice(F_GROUP_DIM * grp, F_GROUP_DIM * (grp + 1))
            lhs = jnp.concatenate([g[:n1, cols], g[n1:, cols]], axis=1)
            o_ref[kk, :, cols] = _dot(lhs, wc_ref[...]).astype(BF16)


def _fourier_mix(xf, bsz, seq, consts, nsub=4, kb=8):
    wa, wb, wc = consts
    n2 = FFT_N2
    n1 = seq // n2
    rblk = nsub * BF16_SUBLANES
    proj4 = xf.reshape(bsz, n2, n1, F_WIDTH)
    za = pl.pallas_call(
        functools.partial(_fft_a_kernel, nsub=nsub),
        grid=(n1 // rblk, bsz),
        in_specs=[
            pl.BlockSpec((None, n2, rblk, F_WIDTH), lambda i, b: (b, 0, i, 0)),
            pl.BlockSpec((nsub,) + wa.shape[1:], lambda i, b: (i, 0, 0)),
        ],
        out_specs=pl.BlockSpec((None, 2, n2, rblk, F_WIDTH), lambda i, b: (b, 0, 0, i, 0)),
        out_shape=jax.ShapeDtypeStruct((bsz, 2, n2, n1, F_WIDTH), BF16),
        compiler_params=pltpu.CompilerParams(
            dimension_semantics=("parallel", "parallel"),
            vmem_limit_bytes=VMEM_LIMIT),
        name="fft_stage_a",
    )(proj4, wa)
    yt = pl.pallas_call(
        _fft_b_kernel,
        grid=(bsz, n2 // kb),
        in_specs=[
            pl.BlockSpec((None, 2, kb, n1, F_WIDTH), lambda b, k: (b, 0, k, 0, 0)),
            _resident(wb.shape, lambda b, k: (0, 0)),
            _resident(wc.shape, lambda b, k: (0, 0)),
        ],
        out_specs=pl.BlockSpec((None, kb, n1, F_WIDTH), lambda b, k: (b, k, 0, 0)),
        out_shape=jax.ShapeDtypeStruct((bsz, n2, n1, F_WIDTH), BF16),
        compiler_params=pltpu.CompilerParams(
            dimension_semantics=("parallel", "parallel"),
            vmem_limit_bytes=VMEM_LIMIT),
        name="fft_stage_b",
    )(za, wb, wc)
    return yt


def _log_sigmoid(x):
    return jnp.minimum(x, 0.0) - jnp.log1p(jnp.exp(-jnp.abs(x)))


def _conv_qkv_kernel(prev_ref, cur_ref, next_ref, cw_ref, cb_ref, wq_ref, wk_ref, wkt_ref, wv_ref,
                     wg_ref, bg_ref, q_ref, k_ref, kt_ref, v_ref, xc_ref, g_ref, xs_ref,
                     *, tiles_per_seq):
    tm = cur_ref.shape[0]
    halo = BF16_SUBLANES
    pos = pl.program_id(0) % tiles_per_seq
    keep_prev = (pos != 0).astype(F32)
    keep_next = (pos != tiles_per_seq - 1).astype(F32)
    xs_ref[0:halo, :] = prev_ref[...].astype(F32) * keep_prev
    xs_ref[halo:halo + tm, :] = cur_ref[...].astype(F32)
    xs_ref[halo + tm:, :] = next_ref[...].astype(F32) * keep_next

    gacc = jnp.zeros((tm, GATE_LANES), F32)
    for g in range(M_WIDTH // MXU_DIM):
        cols = slice(MXU_DIM * g, MXU_DIM * (g + 1))
        acc = jnp.broadcast_to(cb_ref[:, cols], (tm, MXU_DIM))
        ext = xs_ref[halo - SUBLANES:halo + tm + SUBLANES, cols]
        for j in range(CONV_K):
            shift = (CONV_K // 2 - j) % ext.shape[0]
            tap = ext if shift == 0 else pltpu.roll(ext, shift, axis=0)
            acc = acc + cw_ref[j:j + 1, cols] * tap[SUBLANES:SUBLANES + tm]
        xcb = (acc * (1.0 + jnp.tanh(acc))).astype(BF16)
        q = _dot(xcb, wq_ref[g])
        k = _dot(xcb, wk_ref[g])
        kt_ref[cols, :] = _dot_nt(wkt_ref[g], xcb).astype(BF16)
        v = _dot(cur_ref[:, cols], wv_ref[g])
        qb, kb, vb = q.astype(BF16), k.astype(BF16), v.astype(BF16)
        gacc = gacc + _dot(qb, wg_ref[0, g]) + _dot(kb, wg_ref[1, g]) + _dot(vb, wg_ref[2, g])
        q_ref[:, cols] = qb
        k_ref[:, cols] = kb
        v_ref[:, cols] = vb
        xc_ref[:, cols] = xcb
    gates = gacc + bg_ref[...]
    lane = lax.broadcasted_iota(jnp.int32, gates.shape, 1)
    is_forget = (lane // M_HEADS) % 2 == 1
    g_ref[...] = jnp.where(is_forget, _log_sigmoid(gates), gates)


def _conv_qkv(xm, seq, cw, cb, wq, wk, wkt, wv, wg, bg, tm=256):
    t = xm.shape[0]
    halo = BF16_SUBLANES
    hb = tm // halo
    nhalo = t // halo
    out_bf = jax.ShapeDtypeStruct((t, M_WIDTH), BF16)
    row_spec = pl.BlockSpec((tm, M_WIDTH), lambda i: (i, 0))

    def full(a):
        return _resident(a.shape, lambda i: (0,) * a.ndim)

    return pl.pallas_call(
        functools.partial(_conv_qkv_kernel, tiles_per_seq=seq // tm),
        grid=(t // tm,),
        in_specs=[
            pl.BlockSpec((halo, M_WIDTH), lambda i: (jnp.maximum(i * hb - 1, 0), 0)),
            row_spec,
            pl.BlockSpec((halo, M_WIDTH), lambda i: (jnp.minimum((i + 1) * hb, nhalo - 1), 0)),
            full(cw), full(cb), full(wq), full(wk), full(wkt), full(wv), full(wg), full(bg),
        ],
        out_specs=[row_spec, row_spec, pl.BlockSpec((M_WIDTH, tm), lambda i: (0, i)),
                   row_spec, row_spec,
                   pl.BlockSpec((tm, GATE_LANES), lambda i: (i, 0))],
        out_shape=[out_bf, out_bf, jax.ShapeDtypeStruct((M_WIDTH, t), BF16), out_bf, out_bf,
                   jax.ShapeDtypeStruct((t, GATE_LANES), F32)],
        scratch_shapes=[pltpu.VMEM((tm + 2 * halo, M_WIDTH), F32)],
        compiler_params=pltpu.CompilerParams(
            dimension_semantics=("parallel",),
            vmem_limit_bytes=VMEM_LIMIT),
        name="conv_qkv_gates",
    )(xm, xm, xm, cw, cb, wq, wk, wkt, wv, wg, bg)


GS_C, GS_M, GS_INTER, GS_ENEGM, GS_WS, GS_DECAY, GS_ROWS = 0, 1, 2, 3, 4, 5, 6


def _lane_scan(x, op, fill, reverse):
    n = x.shape[1]
    lane = lax.broadcasted_iota(jnp.int32, x.shape, 1)
    sh = 1
    while sh < n:
        if reverse:
            x = op(x, jnp.where(lane < n - sh, pltpu.roll(x, n - sh, axis=1), fill))
        else:
            x = op(x, jnp.where(lane >= sh, pltpu.roll(x, sh, axis=1), fill))
        sh *= 2
    return x


def _gate_prep(i_pre, log_f, gs_ref, tmp_ref, *, reverse):
    nc, L = i_pre.shape
    b = _lane_scan(log_f, jnp.add, 0.0, reverse)
    c = i_pre - b
    cm = _lane_scan(c, jnp.maximum, -jnp.inf, reverse)
    end = slice(0, 1) if reverse else slice(L - 1, L)
    g = b[:, end]
    cmt = cm[:, end]
    tmp_ref[0] = jnp.broadcast_to(g, (nc, LANES))
    tmp_ref[1] = jnp.broadcast_to(g + cmt, (nc, LANES))

    m = jnp.zeros((1, LANES), F32)
    for step in range(nc):
        idx = nc - 1 - step if reverse else step
        tmp_ref[2, idx:idx + 1, :] = m
        m = jnp.maximum(m + tmp_ref[0, idx:idx + 1, :], tmp_ref[1, idx:idx + 1, :])
    m_prev = tmp_ref[2][:, 0:1]
    big_m = jnp.maximum(m_prev, cm)
    mx = jnp.maximum(m_prev, cmt)
    gs_ref[GS_C] = c
    gs_ref[GS_M] = big_m
    gs_ref[GS_INTER] = jnp.exp(m_prev - big_m)
    gs_ref[GS_ENEGM] = jnp.exp(-b - big_m)
    gs_ref[GS_WS] = jnp.exp(c - mx)
    gs_ref[GS_DECAY] = jnp.broadcast_to(jnp.exp(m_prev - mx), (nc, L))


def _mlstm_chunk(q, k, kt, v, gs_ref, ci, ct_ref, n_ref, *, reverse):
    L = q.shape[0]

    def row(r):
        return gs_ref[r, pl.ds(ci, 1), :]

    c_row = row(GS_C)
    sub = lax.broadcasted_iota(jnp.int32, (SUBLANES, L), 0)
    packed = jnp.where(sub == 0, row(GS_M),
                       jnp.where(sub == 1, row(GS_INTER),
                                 jnp.where(sub == 2, row(GS_ENEGM), row(GS_WS))))
    tr = jnp.concatenate([packed, jnp.zeros((LANES - SUBLANES, L), F32)], axis=0).T
    m_col, inter, enegm, ws = tr[:, 0:1], tr[:, 1:2], tr[:, 2:3], tr[:, 3:4]
    decay = row(GS_DECAY)[:, 0:1]

    r_i = lax.broadcasted_iota(jnp.int32, (L, L), 0)
    c_i = lax.broadcasted_iota(jnp.int32, (L, L), 1)
    mask = (c_i >= r_i) if reverse else (c_i <= r_i)
    p = jnp.exp(jnp.where(mask, c_row - m_col, -jnp.inf))
    s = _dot_nt(q, k) * p
    qi = inter * q.astype(F32)
    num = _dot(s.astype(BF16), v) + _dot(qi.astype(BF16), ct_ref[...].astype(BF16))
    den = jnp.sum(s, axis=1, keepdims=True) + jnp.sum(qi * n_ref[...], axis=1, keepdims=True)
    h = num * (1.0 / jnp.maximum(jnp.abs(den), enegm))

    vw = ws.astype(BF16) * v
    ct_ref[...] = decay * ct_ref[...] + _dot(kt, vw)
    n_ref[...] = decay * n_ref[...] + jnp.sum(ws * k.astype(F32), axis=0, keepdims=True)
    return h


def _mlstm_kernel(gates_ref, qf_ref, kf_ref, ktf_ref, vf_ref, qb_ref, kb_ref, ktb_ref, vb_ref,
                  hf_ref, hb_ref,
                  ct_ref, n_ref, gs_ref, tmp_ref, *, hp, nc, cps):
    c = pl.program_id(2)
    dh = M_HEAD_DIM
    L = MCHUNK

    @pl.when(c == 0)
    def _():
        ct_ref[...] = jnp.zeros_like(ct_ref)
        n_ref[...] = jnp.zeros_like(n_ref)
        for hh in range(hp):
            _gate_prep(gates_ref[hh, 0], gates_ref[hh, 1], gs_ref.at[hh, 0], tmp_ref.at[hh, 0],
                       reverse=False)
            _gate_prep(gates_ref[hh, 2], gates_ref[hh, 3], gs_ref.at[hh, 1], tmp_ref.at[hh, 1],
                       reverse=True)

    for sub in range(cps):
        rf = slice(L * sub, L * (sub + 1))
        rb = slice(L * (cps - 1 - sub), L * (cps - sub))
        for hh in range(hp):
            cols = slice(dh * hh, dh * (hh + 1))
            hf = _mlstm_chunk(qf_ref[rf, cols], kf_ref[rf, cols], ktf_ref[cols, rf],
                              vf_ref[rf, cols], gs_ref.at[hh, 0], cps * c + sub,
                              ct_ref.at[hh, 0], n_ref.at[hh, 0], reverse=False)
            hf_ref[rf, cols] = hf.astype(hf_ref.dtype)
            hb = _mlstm_chunk(qb_ref[rb, cols], kb_ref[rb, cols], ktb_ref[cols, rb],
                              vb_ref[rb, cols], gs_ref.at[hh, 1], nc - 1 - (cps * c + sub),
                              ct_ref.at[hh, 1], n_ref.at[hh, 1], reverse=True)
            hb_ref[rb, cols] = hb.astype(hb_ref.dtype)


def _mlstm(q, k, kt, v, gates4, bsz, seq, hp=MLSTM_HEADS_PER_STEP, cps=MLSTM_CHUNKS_PER_STEP):
    t = q.shape[0]
    L = MCHUNK
    nc = seq // L
    dh = M_HEAD_DIM
    width = hp * dh
    rows = cps * L
    ns = nc // cps

    qkv_f = pl.BlockSpec((rows, width), lambda b, h, c: (b * ns + c, h))
    qkv_b = pl.BlockSpec((rows, width), lambda b, h, c: (b * ns + ns - 1 - c, h))
    kt_f = pl.BlockSpec((width, rows), lambda b, h, c: (h, b * ns + c))
    kt_b = pl.BlockSpec((width, rows), lambda b, h, c: (h, b * ns + ns - 1 - c))
    g_spec = pl.BlockSpec((None, hp, 4, nc, L), lambda b, h, c: (b, h, 0, 0, 0))
    out = jax.ShapeDtypeStruct((t, M_WIDTH), BF16)
    return pl.pallas_call(
        functools.partial(_mlstm_kernel, hp=hp, nc=nc, cps=cps),
        grid=(bsz, M_HEADS // hp, ns),
        in_specs=[g_spec, qkv_f, qkv_f, kt_f, qkv_f, qkv_b, qkv_b, kt_b, qkv_b],
        out_specs=[qkv_f, qkv_b],
        out_shape=[out, out],
        scratch_shapes=[
            pltpu.VMEM((hp, 2, dh, dh), F32),
            pltpu.VMEM((hp, 2, 1, dh), F32),
            pltpu.VMEM((hp, 2, GS_ROWS, nc, L), F32),
            pltpu.VMEM((hp, 2, 3, nc, LANES), F32),
        ],
        compiler_params=pltpu.CompilerParams(
            dimension_semantics=("parallel", "parallel", "arbitrary"),
            vmem_limit_bytes=VMEM_LIMIT),
        name="mlstm",
    )(gates4, q, k, kt, v, q, k, kt, v)


def _merge_kernel(hf_ref, hb_ref, xc_ref, yf_ref, x_ref, nw_ref, wzf_ref, wzm_ref, wom_ref, wg_ref,
                  hn_ref, sk_ref, wf_ref, wm_ref, wo_ref, fn_ref, o_ref, *, final_norm, sub_rows):
    for r in range(o_ref.shape[0] // sub_rows):
        rows = slice(sub_rows * r, sub_rows * (r + 1))
        hin = _rmsnorm_bf16(x_ref[rows, :], nw_ref[...])
        y_b = None
        for h in range(M_HEADS):
            cols = slice(M_HEAD_DIM * h, M_HEAD_DIM * (h + 1))
            hc2 = (hf_ref[rows, cols].astype(F32) + hb_ref[rows, cols].astype(F32))
            hc2 = hc2 * (1.0 + jnp.tanh(_dot(hin, wom_ref[:, cols])))
            hc = hc2 * lax.rsqrt(jnp.mean(hc2 * hc2, axis=-1, keepdims=True) + 4.0 * EPS)
            u = hc * hn_ref[:, cols] + sk_ref[:, cols] * xc_ref[rows, cols].astype(F32)
            hz = _dot(hin, wzm_ref[:, cols])
            u = u * (hz * (1.0 + jnp.tanh(hz)))
            part = _dot(u.astype(BF16), wm_ref[cols, :])
            y_b = part if y_b is None else y_b + part
        hzf = _dot(hin, wzf_ref[...])
        ya_in = yf_ref[rows, :].astype(F32) * (hzf * (1.0 + jnp.tanh(hzf)))
        y_a = _dot(ya_in.astype(BF16), wf_ref[...])
        t_a = jnp.tanh(_dot(hin, wg_ref[:, :D_MODEL]))
        t_b = jnp.tanh(_dot(hin, wg_ref[:, D_MODEL:]))
        merged2 = (y_a + y_b) + (t_a * y_a + t_b * y_b)
        xn = x_ref[rows, :] + _dot(merged2.astype(BF16), wo_ref[...])
        if final_norm:
            xn = xn * lax.rsqrt(jnp.mean(xn * xn, axis=-1, keepdims=True) + EPS) * fn_ref[...]
        o_ref[rows, :] = xn


def _merge(hf, hb, xc, yf, x2, nw, w_zf, w_gate, hn, sk, wf, wm, wo, fn, final_norm,
           tm=512, sub_rows=512):
    t, d = x2.shape

    def rows(width):
        return pl.BlockSpec((tm, width), lambda i: (i, 0))

    def full(a):
        return _resident(a.shape, lambda i: (0,) * a.ndim)

    def w_gate_cols(col, width):
        return _resident((d, width), lambda i: (0, col // width))

    return pl.pallas_call(
        functools.partial(_merge_kernel, final_norm=final_norm, sub_rows=sub_rows),
        grid=(t // tm,),
        in_specs=[
            rows(M_WIDTH), rows(M_WIDTH), rows(M_WIDTH), rows(F_WIDTH), rows(D_MODEL),
            full(nw),
            full(w_zf), w_gate_cols(GCOL_ZM, M_WIDTH),
            w_gate_cols(GCOL_OM, M_WIDTH), w_gate_cols(GCOL_G, 2 * D_MODEL),
            full(hn), full(sk), full(wf), full(wm), full(wo), full(fn),
        ],
        out_specs=rows(D_MODEL),
        out_shape=jax.ShapeDtypeStruct((t, D_MODEL), F32),
        compiler_params=pltpu.CompilerParams(
            dimension_semantics=("parallel",),
            vmem_limit_bytes=VMEM_LIMIT),
        name="merge_out",
    )(hf, hb, xc, yf, x2, nw, w_zf, w_gate, w_gate, w_gate, hn, sk, wf, wm, wo, fn)


def _block_diag_tiles(w):
    rows = w.reshape(-1, MXU_DIM, QKV_BLOCK)
    dense = jnp.tile(rows, (1, 1, MXU_DIM // QKV_BLOCK))
    r = lax.broadcasted_iota(jnp.int32, (MXU_DIM, MXU_DIM), 0) // QKV_BLOCK
    c = lax.broadcasted_iota(jnp.int32, (MXU_DIM, MXU_DIM), 1) // QKV_BLOCK
    return jnp.where(r == c, dense, 0.0).astype(BF16)


def kernel(x, norm_w, w_in, w_fourier, conv_w, conv_b, w_q, w_k, w_v,
           w_igate_fwd, b_igate_fwd, w_fgate_fwd, b_fgate_fwd,
           w_igate_bwd, b_igate_bwd, w_fgate_bwd, b_fgate_bwd,
           hnorm_w, skip_w, w_mlstm, w_out, final_norm_w):
    bsz, seq, d = x.shape
    depth = w_in.shape[0]
    assert d == D_MODEL and seq % (FFT_N2 * BF16_SUBLANES * 2) == 0 and seq % MCHUNK == 0
    t = bsz * seq
    nc = seq // MCHUNK
    consts = tuple(jnp.asarray(a, dtype=F32).astype(BF16) for a in _fft_constants(seq))
    fn = final_norm_w.reshape(1, d)

    x2 = x.reshape(t, d)
    for l in range(depth):
        w_gate = (0.5 * w_in[l, :, COL_ZM:]).astype(BF16)
        w_zf = (0.5 * w_in[l, :, COL_ZF:COL_XM]).astype(BF16)
        nw = norm_w[l].reshape(1, d)

        wg = jnp.concatenate([w_igate_fwd[l], w_fgate_fwd[l], w_igate_bwd[l], w_fgate_bwd[l]], axis=1)
        wg = jnp.pad(wg, ((0, 0), (0, GATE_LANES - wg.shape[1])))
        wg = wg.reshape(3, M_WIDTH // MXU_DIM, MXU_DIM, GATE_LANES)
        qscale = M_HEAD_DIM ** -0.5
        wg = (wg * jnp.asarray([1.0 / qscale, 1.0, 1.0], F32).reshape(3, 1, 1, 1)).astype(BF16)
        bg = jnp.concatenate([b_igate_fwd[l], b_fgate_fwd[l], b_igate_bwd[l], b_fgate_bwd[l]])
        bg = jnp.pad(bg, (0, GATE_LANES - bg.shape[0])).reshape(1, GATE_LANES)
        wk_tiles = _block_diag_tiles(w_k[l])
        wkt_tiles = _block_diag_tiles(jnp.swapaxes(w_k[l], -1, -2))
        xf, xm = _inproj(x2, nw, w_in, l)
        q, k, kt, v, xc, gates = _conv_qkv(
            xm, seq, 0.5 * conv_w[l], 0.5 * conv_b[l].reshape(1, M_WIDTH),
            _block_diag_tiles(w_q[l] * qscale), wk_tiles, wkt_tiles,
            _block_diag_tiles(w_v[l]), wg, bg)

        yt = _fourier_mix(xf, bsz, seq, consts)
        yf = jnp.transpose(yt, (0, 2, 1, 3)).reshape(t, F_WIDTH)

        g4 = gates[:, :4 * M_HEADS].reshape(bsz, nc, MCHUNK, 4, M_HEADS)
        g4 = jnp.transpose(g4, (0, 4, 3, 1, 2))
        hf, hb = _mlstm(q, k, kt, v, g4, bsz, seq)

        x2 = _merge(hf, hb, xc, yf, x2, nw, w_zf, w_gate,
                    hnorm_w[l].reshape(1, M_WIDTH), skip_w[l].reshape(1, M_WIDTH),
                    w_fourier[l].astype(BF16), w_mlstm[l].astype(BF16),
                    (0.5 * w_out[l]).astype(BF16),
                    fn, final_norm=(l == depth - 1))
    return x2.reshape(bsz, seq, d)
```

```python
import functools

import numpy as np
import jax
import jax.numpy as jnp
from jax import lax
from jax.experimental import pallas as pl
from jax.experimental.pallas import tpu as pltpu

F32 = jnp.float32
BF16 = jnp.bfloat16

D_MODEL = 1024
F_WIDTH = D_MODEL
F_GROUPS = 4
F_GROUP_DIM = F_WIDTH // F_GROUPS
M_WIDTH = 2 * D_MODEL
M_HEADS = 4
M_HEAD_DIM = M_WIDTH // M_HEADS
QKV_BLOCK = 4
CONV_K = 5
EPS = 1e-6
IN_COLS = 2 * F_WIDTH + 3 * M_WIDTH + 2 * D_MODEL

MCHUNK = 256
MLSTM_HEADS_PER_STEP = 2
MLSTM_CHUNKS_PER_STEP = 2

COL_XF = 0
COL_ZF = F_WIDTH
COL_XM = 2 * F_WIDTH
COL_ZM = COL_XM + M_WIDTH
COL_OM = COL_ZM + M_WIDTH
COL_G = COL_OM + M_WIDTH
GCOL_ZM = 0
GCOL_OM = M_WIDTH
GCOL_G = 2 * M_WIDTH

LANES = 128
SUBLANES = 8
BF16_SUBLANES = 16
MXU_DIM = 256
GATE_LANES = 128
VMEM_LIMIT = 56 * 1024 * 1024

FFT_N2 = 32
FFT_A_N1 = 32


def _dot(a, b):
    return jnp.dot(a, b, preferred_element_type=F32)


def _dot_nt(a, b):
    return lax.dot_general(a, b, (((1,), (1,)), ((), ())), preferred_element_type=F32)


def _rmsnorm_bf16(x, w):
    ms = jnp.mean(x * x, axis=-1, keepdims=True)
    return (x * lax.rsqrt(ms + EPS) * w).astype(BF16)


def _resident(shape, index_map):
    return pl.BlockSpec(shape, index_map, pipeline_mode=pl.Buffered(1))


def _inproj_kernel(x_ref, nw_ref, wf_ref, wm_ref, xf_ref, xm_ref, *, sub_rows):
    wf = wf_ref[...].astype(BF16)
    wm = wm_ref[...].astype(BF16)
    for r in range(x_ref.shape[0] // sub_rows):
        rows = slice(sub_rows * r, sub_rows * (r + 1))
        h = _rmsnorm_bf16(x_ref[rows, :], nw_ref[...])
        xf_ref[rows, :] = _dot(h, wf).astype(BF16)
        xm_ref[rows, :] = _dot(h, wm).astype(BF16)


def _inproj(x2, norm_w, w_in, layer, tm=1024, sub_rows=256):
    t, d = x2.shape
    return pl.pallas_call(
        functools.partial(_inproj_kernel, sub_rows=sub_rows),
        grid=(t // tm,),
        in_specs=[
            pl.BlockSpec((tm, d), lambda i: (i, 0)),
            pl.BlockSpec((1, d), lambda i: (0, 0)),
            _resident((None, d, F_WIDTH), lambda i: (layer, 0, COL_XF // F_WIDTH)),
            _resident((None, d, M_WIDTH), lambda i: (layer, 0, COL_XM // M_WIDTH)),
        ],
        out_specs=[pl.BlockSpec((tm, F_WIDTH), lambda i: (i, 0)),
                   pl.BlockSpec((tm, M_WIDTH), lambda i: (i, 0))],
        out_shape=[jax.ShapeDtypeStruct((t, F_WIDTH), BF16),
                   jax.ShapeDtypeStruct((t, M_WIDTH), BF16)],
        compiler_params=pltpu.CompilerParams(
            dimension_semantics=("parallel",),
            vmem_limit_bytes=VMEM_LIMIT),
        name="inproj",
    )(x2, norm_w, w_in, w_in)


def _fft_constants(seq):
    n2 = FFT_N2
    n1 = seq // n2
    k2 = np.arange(n2)
    ang2 = 2.0 * np.pi * ((k2[:, None] * k2[None, :]) % n2) / n2
    wa = np.concatenate([np.cos(ang2), -np.sin(ang2)], axis=0)
    angt = 2.0 * np.pi * ((k2[:, None] * np.arange(n1)[None, :]) % seq) / seq
    tw = np.stack([np.cos(angt), np.sin(angt)], axis=0)
    tw = tw.reshape(2, n2, n1 // FFT_A_N1, FFT_A_N1).transpose(2, 0, 1, 3)
    a = np.arange(n1)
    angb = 2.0 * np.pi * ((a[:, None] * a[None, :]) % n1) / n1
    wr, wi = np.cos(angb), -np.sin(angb)
    wb = np.block([[wr, -wi], [wi, wr]])
    c = np.arange(F_GROUP_DIM)
    angc = 2.0 * np.pi * ((c[:, None] * c[None, :]) % F_GROUP_DIM) / F_GROUP_DIM
    wc = np.concatenate([np.cos(angc), np.sin(angc)], axis=0) / np.sqrt(seq * F_GROUP_DIM)
    return wa, tw, wb, wc


def _fft_a_kernel(x_ref, wa_ref, tw_ref, o_ref):
    n2 = FFT_N2
    c = F_WIDTH
    t = _dot(wa_ref[...], x_ref[...])
    tr, ti = t[:n2], t[n2:]
    for j in range(FFT_A_N1):
        cols = slice(c * j, c * (j + 1))
        cs = tw_ref[0, :, j:j + 1]
        sn = tw_ref[1, :, j:j + 1]
        o_ref[0, :, cols] = (tr[:, cols] * cs + ti[:, cols] * sn).astype(BF16)
        o_ref[1, :, cols] = (ti[:, cols] * cs - tr[:, cols] * sn).astype(BF16)


def _fft_b_kernel(z_ref, wb_ref, wc_ref, o_ref):
    nk, n1, c = z_ref.shape[1:]
    for kk in range(nk):
        xs = z_ref[:, kk].reshape(2 * n1, c)
        g = _dot(wb_ref[...], xs).astype(BF16)
        for grp in range(c // F_GROUP_DIM):
            cols = slice(F_GROUP_DIM * grp, F_GROUP_DIM * (grp + 1))
            lhs = jnp.concatenate([g[:n1, cols], g[n1:, cols]], axis=1)
            o_ref[kk, :, cols] = _dot(lhs, wc_ref[...]).astype(BF16)


def _fourier_mix(xf, bsz, seq, consts, kb=8):
    wa, tw, wb, wc = consts
    n2 = FFT_N2
    n1 = seq // n2
    lanes = FFT_A_N1 * F_WIDTH
    za = pl.pallas_call(
        _fft_a_kernel,
        grid=(bsz, n1 // FFT_A_N1),
        in_specs=[
            pl.BlockSpec((None, n2, lanes), lambda b, i: (b, 0, i)),
            _resident(wa.shape, lambda b, i: (0, 0)),
            pl.BlockSpec((None,) + tw.shape[1:], lambda b, i: (i, 0, 0, 0)),
        ],
        out_specs=pl.BlockSpec((None, 2, n2, lanes), lambda b, i: (b, 0, 0, i)),
        out_shape=jax.ShapeDtypeStruct((bsz, 2, n2, n1 * F_WIDTH), BF16),
        compiler_params=pltpu.CompilerParams(
            dimension_semantics=("parallel", "parallel"),
            vmem_limit_bytes=VMEM_LIMIT),
        name="fft_stage_a",
    )(xf.reshape(bsz, n2, n1 * F_WIDTH), wa, tw).reshape(bsz, 2, n2, n1, F_WIDTH)
    yt = pl.pallas_call(
        _fft_b_kernel,
        grid=(bsz, n2 // kb),
        in_specs=[
            pl.BlockSpec((None, 2, kb, n1, F_WIDTH), lambda b, k: (b, 0, k, 0, 0)),
            _resident(wb.shape, lambda b, k: (0, 0)),
            _resident(wc.shape, lambda b, k: (0, 0)),
        ],
        out_specs=pl.BlockSpec((None, kb, n1, F_WIDTH), lambda b, k: (b, k, 0, 0)),
        out_shape=jax.ShapeDtypeStruct((bsz, n2, n1, F_WIDTH), BF16),
        compiler_params=pltpu.CompilerParams(
            dimension_semantics=("parallel", "parallel"),
            vmem_limit_bytes=VMEM_LIMIT),
        name="fft_stage_b",
    )(za, wb, wc)
    return yt


def _log_sigmoid(x):
    return jnp.minimum(x, 0.0) - jnp.log1p(jnp.exp(-jnp.abs(x)))


def _conv_qkv_kernel(prev_ref, cur_ref, next_ref, cw_ref, cb_ref, wq_ref, wk_ref, wkt_ref, wv_ref,
                     wg_ref, bg_ref, q_ref, k_ref, kt_ref, v_ref, xc_ref, g_ref, xs_ref,
                     *, tiles_per_seq):
    tm = cur_ref.shape[0]
    halo = BF16_SUBLANES
    pos = pl.program_id(0) % tiles_per_seq
    keep_prev = (pos != 0).astype(F32)
    keep_next = (pos != tiles_per_seq - 1).astype(F32)
    xs_ref[0:halo, :] = prev_ref[...].astype(F32) * keep_prev
    xs_ref[halo:halo + tm, :] = cur_ref[...].astype(F32)
    xs_ref[halo + tm:, :] = next_ref[...].astype(F32) * keep_next

    gacc = jnp.zeros((tm, GATE_LANES), F32)
    for g in range(M_WIDTH // MXU_DIM):
        cols = slice(MXU_DIM * g, MXU_DIM * (g + 1))
        acc = jnp.broadcast_to(cb_ref[:, cols], (tm, MXU_DIM))
        ext = xs_ref[halo - SUBLANES:halo + tm + SUBLANES, cols]
        for j in range(CONV_K):
            shift = (CONV_K // 2 - j) % ext.shape[0]
            tap = ext if shift == 0 else pltpu.roll(ext, shift, axis=0)
            acc = acc + cw_ref[j:j + 1, cols] * tap[SUBLANES:SUBLANES + tm]
        xcb = (acc * (1.0 + jnp.tanh(acc))).astype(BF16)
        q = _dot(xcb, wq_ref[g])
        k = _dot(xcb, wk_ref[g])
        kt_ref[cols, :] = _dot_nt(wkt_ref[g], xcb).astype(BF16)
        v = _dot(cur_ref[:, cols], wv_ref[g])
        qb, kb, vb = q.astype(BF16), k.astype(BF16), v.astype(BF16)
        gacc = gacc + _dot(qb, wg_ref[0, g]) + _dot(kb, wg_ref[1, g]) + _dot(vb, wg_ref[2, g])
        q_ref[:, cols] = qb
        k_ref[:, cols] = kb
        v_ref[:, cols] = vb
        xc_ref[:, cols] = xcb
    gates = gacc + bg_ref[...]
    lane = lax.broadcasted_iota(jnp.int32, gates.shape, 1)
    is_forget = (lane // M_HEADS) % 2 == 1
    g_ref[...] = jnp.where(is_forget, _log_sigmoid(gates), gates)


def _conv_qkv(xm, seq, cw, cb, wq, wk, wkt, wv, wg, bg, tm=256):
    t = xm.shape[0]
    halo = BF16_SUBLANES
    hb = tm // halo
    nhalo = t // halo
    out_bf = jax.ShapeDtypeStruct((t, M_WIDTH), BF16)
    row_spec = pl.BlockSpec((tm, M_WIDTH), lambda i: (i, 0))

    def full(a):
        return _resident(a.shape, lambda i: (0,) * a.ndim)

    return pl.pallas_call(
        functools.partial(_conv_qkv_kernel, tiles_per_seq=seq // tm),
        grid=(t // tm,),
        in_specs=[
            pl.BlockSpec((halo, M_WIDTH), lambda i: (jnp.maximum(i * hb - 1, 0), 0)),
            row_spec,
            pl.BlockSpec((halo, M_WIDTH), lambda i: (jnp.minimum((i + 1) * hb, nhalo - 1), 0)),
            full(cw), full(cb), full(wq), full(wk), full(wkt), full(wv), full(wg), full(bg),
        ],
        out_specs=[row_spec, row_spec, pl.BlockSpec((M_WIDTH, tm), lambda i: (0, i)),
                   row_spec, row_spec,
                   pl.BlockSpec((tm, GATE_LANES), lambda i: (i, 0))],
        out_shape=[out_bf, out_bf, jax.ShapeDtypeStruct((M_WIDTH, t), BF16), out_bf, out_bf,
                   jax.ShapeDtypeStruct((t, GATE_LANES), F32)],
        scratch_shapes=[pltpu.VMEM((tm + 2 * halo, M_WIDTH), F32)],
        compiler_params=pltpu.CompilerParams(
            dimension_semantics=("parallel",),
            vmem_limit_bytes=VMEM_LIMIT),
        name="conv_qkv_gates",
    )(xm, xm, xm, cw, cb, wq, wk, wkt, wv, wg, bg)


GS_C, GS_M, GS_INTER, GS_ENEGM, GS_WS, GS_DECAY, GS_ROWS = 0, 1, 2, 3, 4, 5, 6


def _lane_scan(x, op, fill, reverse):
    n = x.shape[1]
    lane = lax.broadcasted_iota(jnp.int32, x.shape, 1)
    sh = 1
    while sh < n:
        if reverse:
            x = op(x, jnp.where(lane < n - sh, pltpu.roll(x, n - sh, axis=1), fill))
        else:
            x = op(x, jnp.where(lane >= sh, pltpu.roll(x, sh, axis=1), fill))
        sh *= 2
    return x


def _gate_prep(i_pre, log_f, gs_ref, tmp_ref, *, reverse):
    nc, L = i_pre.shape
    b = _lane_scan(log_f, jnp.add, 0.0, reverse)
    c = i_pre - b
    cm = _lane_scan(c, jnp.maximum, -jnp.inf, reverse)
    end = slice(0, 1) if reverse else slice(L - 1, L)
    g = b[:, end]
    cmt = cm[:, end]
    tmp_ref[0] = jnp.broadcast_to(g, (nc, LANES))
    tmp_ref[1] = jnp.broadcast_to(g + cmt, (nc, LANES))

    m = jnp.zeros((1, LANES), F32)
    for step in range(nc):
        idx = nc - 1 - step if reverse else step
        tmp_ref[2, idx:idx + 1, :] = m
        m = jnp.maximum(m + tmp_ref[0, idx:idx + 1, :], tmp_ref[1, idx:idx + 1, :])
    m_prev = tmp_ref[2][:, 0:1]
    big_m = jnp.maximum(m_prev, cm)
    mx = jnp.maximum(m_prev, cmt)
    gs_ref[GS_C] = c
    gs_ref[GS_M] = big_m
    gs_ref[GS_INTER] = jnp.exp(m_prev - big_m)
    gs_ref[GS_ENEGM] = jnp.exp(-b - big_m)
    gs_ref[GS_WS] = jnp.exp(c - mx)
    gs_ref[GS_DECAY] = jnp.broadcast_to(jnp.exp(m_prev - mx), (nc, L))


def _mlstm_chunk(q, k, kt, v, gs_ref, ci, ct_ref, n_ref, *, reverse):
    L = q.shape[0]

    def row(r):
        return gs_ref[r, pl.ds(ci, 1), :]

    c_row = row(GS_C)
    sub = lax.broadcasted_iota(jnp.int32, (SUBLANES, L), 0)
    packed = jnp.where(sub == 0, row(GS_M),
                       jnp.where(sub == 1, row(GS_INTER),
                                 jnp.where(sub == 2, row(GS_ENEGM), row(GS_WS))))
    tr = jnp.concatenate([packed, jnp.zeros((LANES - SUBLANES, L), F32)], axis=0).T
    m_col, inter, enegm, ws = tr[:, 0:1], tr[:, 1:2], tr[:, 2:3], tr[:, 3:4]
    decay = row(GS_DECAY)[:, 0:1]

    r_i = lax.broadcasted_iota(jnp.int32, (L, L), 0)
    c_i = lax.broadcasted_iota(jnp.int32, (L, L), 1)
    mask = (c_i >= r_i) if reverse else (c_i <= r_i)
    p = jnp.exp(jnp.where(mask, c_row - m_col, -jnp.inf))
    s = _dot_nt(q, k) * p
    qi = inter * q.astype(F32)
    num = _dot(s.astype(BF16), v) + _dot(qi.astype(BF16), ct_ref[...].astype(BF16))
    den = jnp.sum(s, axis=1, keepdims=True) + jnp.sum(qi * n_ref[...], axis=1, keepdims=True)
    h = num * (1.0 / jnp.maximum(jnp.abs(den), enegm))

    vw = ws.astype(BF16) * v
    ct_ref[...] = decay * ct_ref[...] + _dot(kt, vw)
    n_ref[...] = decay * n_ref[...] + jnp.sum(ws * k.astype(F32), axis=0, keepdims=True)
    return h


def _mlstm_kernel(gates_ref, qf_ref, kf_ref, ktf_ref, vf_ref, qb_ref, kb_ref, ktb_ref, vb_ref,
                  hf_ref, hb_ref,
                  ct_ref, n_ref, gs_ref, tmp_ref, *, hp, nc, cps):
    c = pl.program_id(2)
    dh = M_HEAD_DIM
    L = MCHUNK

    @pl.when(c == 0)
    def _():
        ct_ref[...] = jnp.zeros_like(ct_ref)
        n_ref[...] = jnp.zeros_like(n_ref)
        for hh in range(hp):
            _gate_prep(gates_ref[hh, 0], gates_ref[hh, 1], gs_ref.at[hh, 0], tmp_ref.at[hh, 0],
                       reverse=False)
            _gate_prep(gates_ref[hh, 2], gates_ref[hh, 3], gs_ref.at[hh, 1], tmp_ref.at[hh, 1],
                       reverse=True)

    for sub in range(cps):
        rf = slice(L * sub, L * (sub + 1))
        rb = slice(L * (cps - 1 - sub), L * (cps - sub))
        for hh in range(hp):
            cols = slice(dh * hh, dh * (hh + 1))
            hf = _mlstm_chunk(qf_ref[rf, cols], kf_ref[rf, cols], ktf_ref[cols, rf],
                              vf_ref[rf, cols], gs_ref.at[hh, 0], cps * c + sub,
                              ct_ref.at[hh, 0], n_ref.at[hh, 0], reverse=False)
            hf_ref[rf, cols] = hf.astype(hf_ref.dtype)
            hb = _mlstm_chunk(qb_ref[rb, cols], kb_ref[rb, cols], ktb_ref[cols, rb],
                              vb_ref[rb, cols], gs_ref.at[hh, 1], nc - 1 - (cps * c + sub),
                              ct_ref.at[hh, 1], n_ref.at[hh, 1], reverse=True)
            hb_ref[rb, cols] = hb.astype(hb_ref.dtype)


def _mlstm(q, k, kt, v, gates4, bsz, seq, hp=MLSTM_HEADS_PER_STEP, cps=MLSTM_CHUNKS_PER_STEP):
    t = q.shape[0]
    L = MCHUNK
    nc = seq // L
    dh = M_HEAD_DIM
    width = hp * dh
    rows = cps * L
    ns = nc // cps

    qkv_f = pl.BlockSpec((rows, width), lambda b, h, c: (b * ns + c, h))
    qkv_b = pl.BlockSpec((rows, width), lambda b, h, c: (b * ns + ns - 1 - c, h))
    kt_f = pl.BlockSpec((width, rows), lambda b, h, c: (h, b * ns + c))
    kt_b = pl.BlockSpec((width, rows), lambda b, h, c: (h, b * ns + ns - 1 - c))
    g_spec = pl.BlockSpec((None, hp, 4, nc, L), lambda b, h, c: (b, h, 0, 0, 0))
    out = jax.ShapeDtypeStruct((t, M_WIDTH), BF16)
    return pl.pallas_call(
        functools.partial(_mlstm_kernel, hp=hp, nc=nc, cps=cps),
        grid=(bsz, M_HEADS // hp, ns),
        in_specs=[g_spec, qkv_f, qkv_f, kt_f, qkv_f, qkv_b, qkv_b, kt_b, qkv_b],
        out_specs=[qkv_f, qkv_b],
        out_shape=[out, out],
        scratch_shapes=[
            pltpu.VMEM((hp, 2, dh, dh), F32),
            pltpu.VMEM((hp, 2, 1, dh), F32),
            pltpu.VMEM((hp, 2, GS_ROWS, nc, L), F32),
            pltpu.VMEM((hp, 2, 3, nc, LANES), F32),
        ],
        compiler_params=pltpu.CompilerParams(
            dimension_semantics=("parallel", "parallel", "arbitrary"),
            vmem_limit_bytes=VMEM_LIMIT),
        name="mlstm",
    )(gates4, q, k, kt, v, q, k, kt, v)


def _merge_kernel(hf_ref, hb_ref, xc_ref, yf_ref, x_ref, nw_ref, wzf_ref, wzm_ref, wom_ref, wg_ref,
                  hn_ref, sk_ref, wf_ref, wm_ref, wo_ref, fn_ref, o_ref, *, final_norm, sub_rows):
    for r in range(o_ref.shape[0] // sub_rows):
        rows = slice(sub_rows * r, sub_rows * (r + 1))
        hin = _rmsnorm_bf16(x_ref[rows, :], nw_ref[...])
        y_b = None
        for h in range(M_HEADS):
            cols = slice(M_HEAD_DIM * h, M_HEAD_DIM * (h + 1))
            hc2 = (hf_ref[rows, cols].astype(F32) + hb_ref[rows, cols].astype(F32))
            hc2 = hc2 * (1.0 + jnp.tanh(_dot(hin, wom_ref[:, cols])))
            hc = hc2 * lax.rsqrt(jnp.mean(hc2 * hc2, axis=-1, keepdims=True) + 4.0 * EPS)
            u = hc * hn_ref[:, cols] + sk_ref[:, cols] * xc_ref[rows, cols].astype(F32)
            hz = _dot(hin, wzm_ref[:, cols])
            u = u * (hz * (1.0 + jnp.tanh(hz)))
            part = _dot(u.astype(BF16), wm_ref[cols, :])
            y_b = part if y_b is None else y_b + part
        hzf = _dot(hin, wzf_ref[...])
        ya_in = yf_ref[rows, :].astype(F32) * (hzf * (1.0 + jnp.tanh(hzf)))
        y_a = _dot(ya_in.astype(BF16), wf_ref[...])
        t_a = jnp.tanh(_dot(hin, wg_ref[:, :D_MODEL]))
        t_b = jnp.tanh(_dot(hin, wg_ref[:, D_MODEL:]))
        merged2 = (y_a + y_b) + (t_a * y_a + t_b * y_b)
        xn = x_ref[rows, :] + _dot(merged2.astype(BF16), wo_ref[...])
        if final_norm:
            xn = xn * lax.rsqrt(jnp.mean(xn * xn, axis=-1, keepdims=True) + EPS) * fn_ref[...]
        o_ref[rows, :] = xn


def _merge(hf, hb, xc, yf, x2, nw, w_zf, w_gate, hn, sk, wf, wm, wo, fn, final_norm,
           tm=512, sub_rows=512):
    t, d = x2.shape

    def rows(width):
        return pl.BlockSpec((tm, width), lambda i: (i, 0))

    def full(a):
        return _resident(a.shape, lambda i: (0,) * a.ndim)

    def w_gate_cols(col, width):
        return _resident((d, width), lambda i: (0, col // width))

    return pl.pallas_call(
        functools.partial(_merge_kernel, final_norm=final_norm, sub_rows=sub_rows),
        grid=(t // tm,),
        in_specs=[
            rows(M_WIDTH), rows(M_WIDTH), rows(M_WIDTH), rows(F_WIDTH), rows(D_MODEL),
            full(nw),
            full(w_zf), w_gate_cols(GCOL_ZM, M_WIDTH),
            w_gate_cols(GCOL_OM, M_WIDTH), w_gate_cols(GCOL_G, 2 * D_MODEL),
            full(hn), full(sk), full(wf), full(wm), full(wo), full(fn),
        ],
        out_specs=rows(D_MODEL),
        out_shape=jax.ShapeDtypeStruct((t, D_MODEL), F32),
        compiler_params=pltpu.CompilerParams(
            dimension_semantics=("parallel",),
            vmem_limit_bytes=VMEM_LIMIT),
        name="merge_out",
    )(hf, hb, xc, yf, x2, nw, w_zf, w_gate, w_gate, w_gate, hn, sk, wf, wm, wo, fn)


def _block_diag_tiles(w):
    rows = w.reshape(-1, MXU_DIM, QKV_BLOCK)
    dense = jnp.tile(rows, (1, 1, MXU_DIM // QKV_BLOCK))
    r = lax.broadcasted_iota(jnp.int32, (MXU_DIM, MXU_DIM), 0) // QKV_BLOCK
    c = lax.broadcasted_iota(jnp.int32, (MXU_DIM, MXU_DIM), 1) // QKV_BLOCK
    return jnp.where(r == c, dense, 0.0).astype(BF16)


def kernel(x, norm_w, w_in, w_fourier, conv_w, conv_b, w_q, w_k, w_v,
           w_igate_fwd, b_igate_fwd, w_fgate_fwd, b_fgate_fwd,
           w_igate_bwd, b_igate_bwd, w_fgate_bwd, b_fgate_bwd,
           hnorm_w, skip_w, w_mlstm, w_out, final_norm_w):
    bsz, seq, d = x.shape
    depth = w_in.shape[0]
    assert d == D_MODEL and seq % (FFT_N2 * FFT_A_N1) == 0 and seq % MCHUNK == 0
    t = bsz * seq
    nc = seq // MCHUNK
    wa, tw, wb, wc = (jnp.asarray(a, dtype=F32) for a in _fft_constants(seq))
    consts = (wa.astype(BF16), tw, wb.astype(BF16), wc.astype(BF16))
    fn = final_norm_w.reshape(1, d)

    x2 = x.reshape(t, d)
    for l in range(depth):
        w_gate = (0.5 * w_in[l, :, COL_ZM:]).astype(BF16)
        w_zf = (0.5 * w_in[l, :, COL_ZF:COL_XM]).astype(BF16)
        nw = norm_w[l].reshape(1, d)

        wg = jnp.concatenate([w_igate_fwd[l], w_fgate_fwd[l], w_igate_bwd[l], w_fgate_bwd[l]], axis=1)
        wg = jnp.pad(wg, ((0, 0), (0, GATE_LANES - wg.shape[1])))
        wg = wg.reshape(3, M_WIDTH // MXU_DIM, MXU_DIM, GATE_LANES)
        qscale = M_HEAD_DIM ** -0.5
        wg = (wg * jnp.asarray([1.0 / qscale, 1.0, 1.0], F32).reshape(3, 1, 1, 1)).astype(BF16)
        bg = jnp.concatenate([b_igate_fwd[l], b_fgate_fwd[l], b_igate_bwd[l], b_fgate_bwd[l]])
        bg = jnp.pad(bg, (0, GATE_LANES - bg.shape[0])).reshape(1, GATE_LANES)
        wk_tiles = _block_diag_tiles(w_k[l])
        wkt_tiles = _block_diag_tiles(jnp.swapaxes(w_k[l], -1, -2))
        xf, xm = _inproj(x2, nw, w_in, l)
        q, k, kt, v, xc, gates = _conv_qkv(
            xm, seq, 0.5 * conv_w[l], 0.5 * conv_b[l].reshape(1, M_WIDTH),
            _block_diag_tiles(w_q[l] * qscale), wk_tiles, wkt_tiles,
            _block_diag_tiles(w_v[l]), wg, bg)

        yt = _fourier_mix(xf, bsz, seq, consts)
        yf = jnp.transpose(yt, (0, 2, 1, 3)).reshape(t, F_WIDTH)

        g4 = gates[:, :4 * M_HEADS].reshape(bsz, nc, MCHUNK, 4, M_HEADS)
        g4 = jnp.transpose(g4, (0, 4, 3, 1, 2))
        hf, hb = _mlstm(q, k, kt, v, g4, bsz, seq)

        x2 = _merge(hf, hb, xc, yf, x2, nw, w_zf, w_gate,
                    hnorm_w[l].reshape(1, M_WIDTH), skip_w[l].reshape(1, M_WIDTH),
                    w_fourier[l].astype(BF16), w_mlstm[l].astype(BF16),
                    (0.5 * w_out[l]).astype(BF16),
                    fn, final_norm=(l == depth - 1))
    return x2.reshape(bsz, seq, d)
```

```python
import functools

import numpy as np
import jax
import jax.numpy as jnp
from jax import lax
from jax.experimental import pallas as pl
from jax.experimental.pallas import tpu as pltpu

F32 = jnp.float32
BF16 = jnp.bfloat16

D_MODEL = 1024
F_WIDTH = D_MODEL
F_GROUPS = 4
F_GROUP_DIM = F_WIDTH // F_GROUPS
M_WIDTH = 2 * D_MODEL
M_HEADS = 4
M_HEAD_DIM = M_WIDTH // M_HEADS
QKV_BLOCK = 4
CONV_K = 5
EPS = 1e-6
IN_COLS = 2 * F_WIDTH + 3 * M_WIDTH + 2 * D_MODEL

MCHUNK = 256
MLSTM_HEADS_PER_STEP = 2
MLSTM_CHUNKS_PER_STEP = 2

COL_XF = 0
COL_ZF = F_WIDTH
COL_XM = 2 * F_WIDTH
COL_ZM = COL_XM + M_WIDTH
COL_OM = COL_ZM + M_WIDTH
COL_G = COL_OM + M_WIDTH
GCOL_ZM = 0
GCOL_OM = M_WIDTH
GCOL_G = 2 * M_WIDTH

LANES = 128
SUBLANES = 8
BF16_SUBLANES = 16
MXU_DIM = 256
GATE_LANES = 128
VMEM_LIMIT = 56 * 1024 * 1024

FFT_N2 = 32


def _dot(a, b):
    return jnp.dot(a, b, preferred_element_type=F32)


def _dot_nt(a, b):
    return lax.dot_general(a, b, (((1,), (1,)), ((), ())), preferred_element_type=F32)


def _rmsnorm_bf16(x, w):
    ms = jnp.mean(x * x, axis=-1, keepdims=True)
    return (x * lax.rsqrt(ms + EPS) * w).astype(BF16)


def _resident(shape, index_map):
    return pl.BlockSpec(shape, index_map, pipeline_mode=pl.Buffered(1))


def _inproj_kernel(x_ref, nw_ref, wf_ref, wm_ref, gzm_ref, gom_ref, gg_ref, gzf_ref, fo_ref, ml_ref,
                   ou_ref, xf_ref, xm_ref, wgate_ref, wzf_ref, wfo_ref, wml_ref, wou_ref,
                   *, sub_rows):
    wf = wf_ref[...].astype(BF16)
    wm = wm_ref[...].astype(BF16)
    for r in range(x_ref.shape[0] // sub_rows):
        rows = slice(sub_rows * r, sub_rows * (r + 1))
        h = _rmsnorm_bf16(x_ref[rows, :], nw_ref[...])
        xf_ref[rows, :] = _dot(h, wf).astype(BF16)
        xm_ref[rows, :] = _dot(h, wm).astype(BF16)
    wgate_ref[:, GCOL_ZM:GCOL_ZM + M_WIDTH] = (0.5 * gzm_ref[...]).astype(BF16)
    wgate_ref[:, GCOL_OM:GCOL_OM + M_WIDTH] = (0.5 * gom_ref[...]).astype(BF16)
    wgate_ref[:, GCOL_G:GCOL_G + 2 * D_MODEL] = (0.5 * gg_ref[...]).astype(BF16)
    wzf_ref[...] = (0.5 * gzf_ref[...]).astype(BF16)
    wfo_ref[...] = fo_ref[...].astype(BF16)
    wml_ref[...] = ml_ref[...].astype(BF16)
    wou_ref[...] = (0.5 * ou_ref[...]).astype(BF16)


def _inproj(x2, norm_w, w_in, w_fourier, w_mlstm, w_out, layer, tm=1024, sub_rows=256):
    t, d = x2.shape
    ns = t // tm
    wrows = d // ns
    assert d % ns == 0 and wrows % BF16_SUBLANES == 0 and M_WIDTH == 2 * d

    def in_chunk(rows, width, col):
        return pl.BlockSpec((None, rows, width), lambda i: (layer, i, col // width))

    def out_chunk(rows, width):
        return pl.BlockSpec((rows, width), lambda i: (i, 0))

    return pl.pallas_call(
        functools.partial(_inproj_kernel, sub_rows=sub_rows),
        grid=(ns,),
        in_specs=[
            pl.BlockSpec((tm, d), lambda i: (i, 0)),
            pl.BlockSpec((1, d), lambda i: (0, 0)),
            _resident((None, d, F_WIDTH), lambda i: (layer, 0, COL_XF // F_WIDTH)),
            _resident((None, d, M_WIDTH), lambda i: (layer, 0, COL_XM // M_WIDTH)),
            in_chunk(wrows, M_WIDTH, COL_ZM), in_chunk(wrows, M_WIDTH, COL_OM),
            in_chunk(wrows, 2 * D_MODEL, COL_G), in_chunk(wrows, F_WIDTH, COL_ZF),
            in_chunk(wrows, d, 0), in_chunk(2 * wrows, d, 0), in_chunk(wrows, d, 0),
        ],
        out_specs=[pl.BlockSpec((tm, F_WIDTH), lambda i: (i, 0)),
                   pl.BlockSpec((tm, M_WIDTH), lambda i: (i, 0)),
                   out_chunk(wrows, 3 * M_WIDTH), out_chunk(wrows, F_WIDTH),
                   out_chunk(wrows, d), out_chunk(2 * wrows, d), out_chunk(wrows, d)],
        out_shape=[jax.ShapeDtypeStruct((t, F_WIDTH), BF16),
                   jax.ShapeDtypeStruct((t, M_WIDTH), BF16),
                   jax.ShapeDtypeStruct((d, 3 * M_WIDTH), BF16),
                   jax.ShapeDtypeStruct((d, F_WIDTH), BF16),
                   jax.ShapeDtypeStruct((F_WIDTH, d), BF16),
                   jax.ShapeDtypeStruct((M_WIDTH, d), BF16),
                   jax.ShapeDtypeStruct((d, d), BF16)],
        compiler_params=pltpu.CompilerParams(
            dimension_semantics=("parallel",),
            vmem_limit_bytes=VMEM_LIMIT),
        name="inproj",
    )(x2, norm_w, w_in, w_in, w_in, w_in, w_in, w_in, w_fourier, w_mlstm, w_out)


def _fft_constants(seq):
    n2 = FFT_N2
    n1 = seq // n2
    sub = BF16_SUBLANES
    nblk = n1 // sub
    i = np.arange(nblk)[:, None, None, None]
    k2 = np.arange(n2)[None, :, None, None]
    j = np.arange(sub)[None, None, :, None]
    m2 = np.arange(n2)[None, None, None, :]
    ang = 2.0 * np.pi * (((m2 * k2) % n2) / n2 + (((sub * i + j) * k2) % seq) / seq)
    val = np.stack([np.cos(ang), -np.sin(ang)], axis=1)
    wa = np.zeros((nblk, 2, n2, sub, n2, sub), np.float64)
    for jj in range(sub):
        wa[:, :, :, jj, :, jj] = val[:, :, :, jj, :]
    wa = wa.reshape(nblk, 2 * n2 * sub, n2 * sub)
    a = np.arange(n1)
    angb = 2.0 * np.pi * ((a[:, None] * a[None, :]) % n1) / n1
    wr, wi = np.cos(angb), -np.sin(angb)
    wb = np.block([[wr, -wi], [wi, wr]])
    c = np.arange(F_GROUP_DIM)
    angc = 2.0 * np.pi * ((c[:, None] * c[None, :]) % F_GROUP_DIM) / F_GROUP_DIM
    wc = np.concatenate([np.cos(angc), np.sin(angc)], axis=0) / np.sqrt(seq * F_GROUP_DIM)
    return wa, wb, wc


def _fft_a_kernel(x_ref, wa_ref, o_ref, *, nsub):
    n2 = FFT_N2
    sub = BF16_SUBLANES
    c = x_ref.shape[-1]
    for i in range(nsub):
        rows = slice(sub * i, sub * (i + 1))
        xs = x_ref[:, rows, :].reshape(n2 * sub, c)
        z = _dot(wa_ref[i], xs)
        o_ref[:, :, rows, :] = z.astype(BF16).reshape(2, n2, sub, c)


def _fft_b_kernel(z_ref, wb_ref, wc_ref, o_ref):
    nk, n1, c = z_ref.shape[1:]
    for kk in range(nk):
        xs = z_ref[:, kk].reshape(2 * n1, c)
        g = _dot(wb_ref[...], xs).astype(BF16)
        for grp in range(c // F_GROUP_DIM):
            cols = slice(F_GROUP_DIM * grp, F_GROUP_DIM * (grp + 1))
            lhs = jnp.concatenate([g[:n1, cols], g[n1:, cols]], axis=1)
            o_ref[kk, :, cols] = _dot(lhs, wc_ref[...]).astype(BF16)


def _fourier_mix(xf, bsz, seq, consts, nsub=4, kb=8):
    wa, wb, wc = consts
    n2 = FFT_N2
    n1 = seq // n2
    rblk = nsub * BF16_SUBLANES
    proj4 = xf.reshape(bsz, n2, n1, F_WIDTH)
    za = pl.pallas_call(
        functools.partial(_fft_a_kernel, nsub=nsub),
        grid=(n1 // rblk, bsz),
        in_specs=[
            pl.BlockSpec((None, n2, rblk, F_WIDTH), lambda i, b: (b, 0, i, 0)),
            pl.BlockSpec((nsub,) + wa.shape[1:], lambda i, b: (i, 0, 0)),
        ],
        out_specs=pl.BlockSpec((None, 2, n2, rblk, F_WIDTH), lambda i, b: (b, 0, 0, i, 0)),
        out_shape=jax.ShapeDtypeStruct((bsz, 2, n2, n1, F_WIDTH), BF16),
        compiler_params=pltpu.CompilerParams(
            dimension_semantics=("parallel", "parallel"),
            vmem_limit_bytes=VMEM_LIMIT),
        name="fft_stage_a",
    )(proj4, wa)
    yt = pl.pallas_call(
        _fft_b_kernel,
        grid=(bsz, n2 // kb),
        in_specs=[
            pl.BlockSpec((None, 2, kb, n1, F_WIDTH), lambda b, k: (b, 0, k, 0, 0)),
            _resident(wb.shape, lambda b, k: (0, 0)),
            _resident(wc.shape, lambda b, k: (0, 0)),
        ],
        out_specs=pl.BlockSpec((None, kb, n1, F_WIDTH), lambda b, k: (b, k, 0, 0)),
        out_shape=jax.ShapeDtypeStruct((bsz, n2, n1, F_WIDTH), BF16),
        compiler_params=pltpu.CompilerParams(
            dimension_semantics=("parallel", "parallel"),
            vmem_limit_bytes=VMEM_LIMIT),
        name="fft_stage_b",
    )(za, wb, wc)
    return yt


def _log_sigmoid(x):
    return jnp.minimum(x, 0.0) - jnp.log1p(jnp.exp(-jnp.abs(x)))


def _conv_qkv_kernel(prev_ref, cur_ref, next_ref, cw_ref, cb_ref, wq_ref, wk_ref, wkt_ref, wv_ref,
                     wg_ref, bg_ref, q_ref, k_ref, kt_ref, v_ref, xc_ref, g_ref, xs_ref,
                     *, tiles_per_seq):
    tm = cur_ref.shape[0]
    halo = BF16_SUBLANES
    pos = pl.program_id(0) % tiles_per_seq
    keep_prev = (pos != 0).astype(F32)
    keep_next = (pos != tiles_per_seq - 1).astype(F32)
    xs_ref[0:halo, :] = prev_ref[...].astype(F32) * keep_prev
    xs_ref[halo:halo + tm, :] = cur_ref[...].astype(F32)
    xs_ref[halo + tm:, :] = next_ref[...].astype(F32) * keep_next

    gacc = jnp.zeros((tm, GATE_LANES), F32)
    for g in range(M_WIDTH // MXU_DIM):
        cols = slice(MXU_DIM * g, MXU_DIM * (g + 1))
        acc = jnp.broadcast_to(cb_ref[:, cols], (tm, MXU_DIM))
        ext = xs_ref[halo - SUBLANES:halo + tm + SUBLANES, cols]
        for j in range(CONV_K):
            shift = (CONV_K // 2 - j) % ext.shape[0]
            tap = ext if shift == 0 else pltpu.roll(ext, shift, axis=0)
            acc = acc + cw_ref[j:j + 1, cols] * tap[SUBLANES:SUBLANES + tm]
        xcb = (acc * (1.0 + jnp.tanh(acc))).astype(BF16)
        q = _dot(xcb, wq_ref[g])
        k = _dot(xcb, wk_ref[g])
        kt_ref[cols, :] = _dot_nt(wkt_ref[g], xcb).astype(BF16)
        v = _dot(cur_ref[:, cols], wv_ref[g])
        qb, kb, vb = q.astype(BF16), k.astype(BF16), v.astype(BF16)
        gacc = gacc + _dot(qb, wg_ref[0, g]) + _dot(kb, wg_ref[1, g]) + _dot(vb, wg_ref[2, g])
        q_ref[:, cols] = qb
        k_ref[:, cols] = kb
        v_ref[:, cols] = vb
        xc_ref[:, cols] = xcb
    gates = gacc + bg_ref[...]
    lane = lax.broadcasted_iota(jnp.int32, gates.shape, 1)
    is_forget = (lane // M_HEADS) % 2 == 1
    g_ref[...] = jnp.where(is_forget, _log_sigmoid(gates), gates)


def _conv_qkv(xm, seq, cw, cb, wq, wk, wkt, wv, wg, bg, tm=256):
    t = xm.shape[0]
    halo = BF16_SUBLANES
    hb = tm // halo
    nhalo = t // halo
    out_bf = jax.ShapeDtypeStruct((t, M_WIDTH), BF16)
    row_spec = pl.BlockSpec((tm, M_WIDTH), lambda i: (i, 0))

    def full(a):
        return _resident(a.shape, lambda i: (0,) * a.ndim)

    return pl.pallas_call(
        functools.partial(_conv_qkv_kernel, tiles_per_seq=seq // tm),
        grid=(t // tm,),
        in_specs=[
            pl.BlockSpec((halo, M_WIDTH), lambda i: (jnp.maximum(i * hb - 1, 0), 0)),
            row_spec,
            pl.BlockSpec((halo, M_WIDTH), lambda i: (jnp.minimum((i + 1) * hb, nhalo - 1), 0)),
            full(cw), full(cb), full(wq), full(wk), full(wkt), full(wv), full(wg), full(bg),
        ],
        out_specs=[row_spec, row_spec, pl.BlockSpec((M_WIDTH, tm), lambda i: (0, i)),
                   row_spec, row_spec,
                   pl.BlockSpec((tm, GATE_LANES), lambda i: (i, 0))],
        out_shape=[out_bf, out_bf, jax.ShapeDtypeStruct((M_WIDTH, t), BF16), out_bf, out_bf,
                   jax.ShapeDtypeStruct((t, GATE_LANES), F32)],
        scratch_shapes=[pltpu.VMEM((tm + 2 * halo, M_WIDTH), F32)],
        compiler_params=pltpu.CompilerParams(
            dimension_semantics=("parallel",),
            vmem_limit_bytes=VMEM_LIMIT),
        name="conv_qkv_gates",
    )(xm, xm, xm, cw, cb, wq, wk, wkt, wv, wg, bg)


GS_C, GS_M, GS_INTER, GS_ENEGM, GS_WS, GS_DECAY, GS_ROWS = 0, 1, 2, 3, 4, 5, 6


def _lane_scan(x, op, fill, reverse):
    n = x.shape[1]
    lane = lax.broadcasted_iota(jnp.int32, x.shape, 1)
    sh = 1
    while sh < n:
        if reverse:
            x = op(x, jnp.where(lane < n - sh, pltpu.roll(x, n - sh, axis=1), fill))
        else:
            x = op(x, jnp.where(lane >= sh, pltpu.roll(x, sh, axis=1), fill))
        sh *= 2
    return x


def _gate_prep(i_pre, log_f, gs_ref, tmp_ref, *, reverse):
    nc, L = i_pre.shape
    b = _lane_scan(log_f, jnp.add, 0.0, reverse)
    c = i_pre - b
    cm = _lane_scan(c, jnp.maximum, -jnp.inf, reverse)
    end = slice(0, 1) if reverse else slice(L - 1, L)
    g = b[:, end]
    cmt = cm[:, end]
    tmp_ref[0] = jnp.broadcast_to(g, (nc, LANES))
    tmp_ref[1] = jnp.broadcast_to(g + cmt, (nc, LANES))

    m = jnp.zeros((1, LANES), F32)
    for step in range(nc):
        idx = nc - 1 - step if reverse else step
        tmp_ref[2, idx:idx + 1, :] = m
        m = jnp.maximum(m + tmp_ref[0, idx:idx + 1, :], tmp_ref[1, idx:idx + 1, :])
    m_prev = tmp_ref[2][:, 0:1]
    big_m = jnp.maximum(m_prev, cm)
    mx = jnp.maximum(m_prev, cmt)
    gs_ref[GS_C] = c
    gs_ref[GS_M] = big_m
    gs_ref[GS_INTER] = jnp.exp(m_prev - big_m)
    gs_ref[GS_ENEGM] = jnp.exp(-b - big_m)
    gs_ref[GS_WS] = jnp.exp(c - mx)
    gs_ref[GS_DECAY] = jnp.broadcast_to(jnp.exp(m_prev - mx), (nc, L))


def _mlstm_chunk(q, k, kt, v, gs_ref, ci, ct_ref, n_ref, *, reverse):
    L = q.shape[0]

    def row(r):
        return gs_ref[r, pl.ds(ci, 1), :]

    c_row = row(GS_C)
    sub = lax.broadcasted_iota(jnp.int32, (SUBLANES, L), 0)
    packed = jnp.where(sub == 0, row(GS_M),
                       jnp.where(sub == 1, row(GS_INTER),
                                 jnp.where(sub == 2, row(GS_ENEGM), row(GS_WS))))
    tr = jnp.concatenate([packed, jnp.zeros((LANES - SUBLANES, L), F32)], axis=0).T
    m_col, inter, enegm, ws = tr[:, 0:1], tr[:, 1:2], tr[:, 2:3], tr[:, 3:4]
    decay = row(GS_DECAY)[:, 0:1]

    r_i = lax.broadcasted_iota(jnp.int32, (L, L), 0)
    c_i = lax.broadcasted_iota(jnp.int32, (L, L), 1)
    mask = (c_i >= r_i) if reverse else (c_i <= r_i)
    p = jnp.exp(jnp.where(mask, c_row - m_col, -jnp.inf))
    s = _dot_nt(q, k) * p
    qi = inter * q.astype(F32)
    num = _dot(s.astype(BF16), v) + _dot(qi.astype(BF16), ct_ref[...].astype(BF16))
    den = jnp.sum(s, axis=1, keepdims=True) + jnp.sum(qi * n_ref[...], axis=1, keepdims=True)
    h = num * (1.0 / jnp.maximum(jnp.abs(den), enegm))

    vw = ws.astype(BF16) * v
    ct_ref[...] = decay * ct_ref[...] + _dot(kt, vw)
    n_ref[...] = decay * n_ref[...] + jnp.sum(ws * k.astype(F32), axis=0, keepdims=True)
    return h


def _mlstm_kernel(gates_ref, qf_ref, kf_ref, ktf_ref, vf_ref, qb_ref, kb_ref, ktb_ref, vb_ref,
                  hf_ref, hb_ref,
                  ct_ref, n_ref, gs_ref, tmp_ref, *, hp, nc, cps):
    c = pl.program_id(2)
    dh = M_HEAD_DIM
    L = MCHUNK

    @pl.when(c == 0)
    def _():
        ct_ref[...] = jnp.zeros_like(ct_ref)
        n_ref[...] = jnp.zeros_like(n_ref)
        for hh in range(hp):
            _gate_prep(gates_ref[hh, 0], gates_ref[hh, 1], gs_ref.at[hh, 0], tmp_ref.at[hh, 0],
                       reverse=False)
            _gate_prep(gates_ref[hh, 2], gates_ref[hh, 3], gs_ref.at[hh, 1], tmp_ref.at[hh, 1],
                       reverse=True)

    for sub in range(cps):
        rf = slice(L * sub, L * (sub + 1))
        rb = slice(L * (cps - 1 - sub), L * (cps - sub))
        for hh in range(hp):
            cols = slice(dh * hh, dh * (hh + 1))
            hf = _mlstm_chunk(qf_ref[rf, cols], kf_ref[rf, cols], ktf_ref[cols, rf],
                              vf_ref[rf, cols], gs_ref.at[hh, 0], cps * c + sub,
                              ct_ref.at[hh, 0], n_ref.at[hh, 0], reverse=False)
            hf_ref[rf, cols] = hf.astype(hf_ref.dtype)
            hb = _mlstm_chunk(qb_ref[rb, cols], kb_ref[rb, cols], ktb_ref[cols, rb],
                              vb_ref[rb, cols], gs_ref.at[hh, 1], nc - 1 - (cps * c + sub),
                              ct_ref.at[hh, 1], n_ref.at[hh, 1], reverse=True)
            hb_ref[rb, cols] = hb.astype(hb_ref.dtype)


def _mlstm(q, k, kt, v, gates4, bsz, seq, hp=MLSTM_HEADS_PER_STEP, cps=MLSTM_CHUNKS_PER_STEP):
    t = q.shape[0]
    L = MCHUNK
    nc = seq // L
    dh = M_HEAD_DIM
    width = hp * dh
    rows = cps * L
    ns = nc // cps

    qkv_f = pl.BlockSpec((rows, width), lambda b, h, c: (b * ns + c, h))
    qkv_b = pl.BlockSpec((rows, width), lambda b, h, c: (b * ns + ns - 1 - c, h))
    kt_f = pl.BlockSpec((width, rows), lambda b, h, c: (h, b * ns + c))
    kt_b = pl.BlockSpec((width, rows), lambda b, h, c: (h, b * ns + ns - 1 - c))
    g_spec = pl.BlockSpec((None, hp, 4, nc, L), lambda b, h, c: (b, h, 0, 0, 0))
    out = jax.ShapeDtypeStruct((t, M_WIDTH), BF16)
    return pl.pallas_call(
        functools.partial(_mlstm_kernel, hp=hp, nc=nc, cps=cps),
        grid=(bsz, M_HEADS // hp, ns),
        in_specs=[g_spec, qkv_f, qkv_f, kt_f, qkv_f, qkv_b, qkv_b, kt_b, qkv_b],
        out_specs=[qkv_f, qkv_b],
        out_shape=[out, out],
        scratch_shapes=[
            pltpu.VMEM((hp, 2, dh, dh), F32),
            pltpu.VMEM((hp, 2, 1, dh), F32),
            pltpu.VMEM((hp, 2, GS_ROWS, nc, L), F32),
            pltpu.VMEM((hp, 2, 3, nc, LANES), F32),
        ],
        compiler_params=pltpu.CompilerParams(
            dimension_semantics=("parallel", "parallel", "arbitrary"),
            vmem_limit_bytes=VMEM_LIMIT),
        name="mlstm",
    )(gates4, q, k, kt, v, q, k, kt, v)


def _merge_kernel(hf_ref, hb_ref, xc_ref, yf_ref, x_ref, nw_ref, wzf_ref, wzm_ref, wom_ref, wg_ref,
                  hn_ref, sk_ref, wf_ref, wm_ref, wo_ref, fn_ref, o_ref, *, final_norm, sub_rows):
    for r in range(o_ref.shape[0] // sub_rows):
        rows = slice(sub_rows * r, sub_rows * (r + 1))
        hin = _rmsnorm_bf16(x_ref[rows, :], nw_ref[...])
        y_b = None
        for h in range(M_HEADS):
            cols = slice(M_HEAD_DIM * h, M_HEAD_DIM * (h + 1))
            hc2 = (hf_ref[rows, cols].astype(F32) + hb_ref[rows, cols].astype(F32))
            hc2 = hc2 * (1.0 + jnp.tanh(_dot(hin, wom_ref[:, cols])))
            hc = hc2 * lax.rsqrt(jnp.mean(hc2 * hc2, axis=-1, keepdims=True) + 4.0 * EPS)
            u = hc * hn_ref[:, cols] + sk_ref[:, cols] * xc_ref[rows, cols].astype(F32)
            hz = _dot(hin, wzm_ref[:, cols])
            u = u * (hz * (1.0 + jnp.tanh(hz)))
            part = _dot(u.astype(BF16), wm_ref[cols, :])
            y_b = part if y_b is None else y_b + part
        hzf = _dot(hin, wzf_ref[...])
        ya_in = yf_ref[rows, :].astype(F32) * (hzf * (1.0 + jnp.tanh(hzf)))
        y_a = _dot(ya_in.astype(BF16), wf_ref[...])
        t_a = jnp.tanh(_dot(hin, wg_ref[:, :D_MODEL]))
        t_b = jnp.tanh(_dot(hin, wg_ref[:, D_MODEL:]))
        merged2 = (y_a + y_b) + (t_a * y_a + t_b * y_b)
        xn = x_ref[rows, :] + _dot(merged2.astype(BF16), wo_ref[...])
        if final_norm:
            xn = xn * lax.rsqrt(jnp.mean(xn * xn, axis=-1, keepdims=True) + EPS) * fn_ref[...]
        o_ref[rows, :] = xn


def _merge(hf, hb, xc, yf, x2, nw, w_zf, w_gate, hn, sk, wf, wm, wo, fn, final_norm,
           tm=512, sub_rows=512):
    t, d = x2.shape

    def rows(width):
        return pl.BlockSpec((tm, width), lambda i: (i, 0))

    def full(a):
        return _resident(a.shape, lambda i: (0,) * a.ndim)

    def w_gate_cols(col, width):
        return _resident((d, width), lambda i: (0, col // width))

    return pl.pallas_call(
        functools.partial(_merge_kernel, final_norm=final_norm, sub_rows=sub_rows),
        grid=(t // tm,),
        in_specs=[
            rows(M_WIDTH), rows(M_WIDTH), rows(M_WIDTH), rows(F_WIDTH), rows(D_MODEL),
            full(nw),
            full(w_zf), w_gate_cols(GCOL_ZM, M_WIDTH),
            w_gate_cols(GCOL_OM, M_WIDTH), w_gate_cols(GCOL_G, 2 * D_MODEL),
            full(hn), full(sk), full(wf), full(wm), full(wo), full(fn),
        ],
        out_specs=rows(D_MODEL),
        out_shape=jax.ShapeDtypeStruct((t, D_MODEL), F32),
        compiler_params=pltpu.CompilerParams(
            dimension_semantics=("parallel",),
            vmem_limit_bytes=VMEM_LIMIT),
        name="merge_out",
    )(hf, hb, xc, yf, x2, nw, w_zf, w_gate, w_gate, w_gate, hn, sk, wf, wm, wo, fn)


def _block_diag_tiles(w):
    rows = w.reshape(-1, MXU_DIM, QKV_BLOCK)
    dense = jnp.tile(rows, (1, 1, MXU_DIM // QKV_BLOCK))
    r = lax.broadcasted_iota(jnp.int32, (MXU_DIM, MXU_DIM), 0) // QKV_BLOCK
    c = lax.broadcasted_iota(jnp.int32, (MXU_DIM, MXU_DIM), 1) // QKV_BLOCK
    return jnp.where(r == c, dense, 0.0).astype(BF16)


def kernel(x, norm_w, w_in, w_fourier, conv_w, conv_b, w_q, w_k, w_v,
           w_igate_fwd, b_igate_fwd, w_fgate_fwd, b_fgate_fwd,
           w_igate_bwd, b_igate_bwd, w_fgate_bwd, b_fgate_bwd,
           hnorm_w, skip_w, w_mlstm, w_out, final_norm_w):
    bsz, seq, d = x.shape
    depth = w_in.shape[0]
    assert d == D_MODEL and seq % (FFT_N2 * BF16_SUBLANES * 2) == 0 and seq % MCHUNK == 0
    t = bsz * seq
    nc = seq // MCHUNK
    consts = tuple(jnp.asarray(a, dtype=F32).astype(BF16) for a in _fft_constants(seq))
    fn = final_norm_w.reshape(1, d)

    x2 = x.reshape(t, d)
    for l in range(depth):
        nw = norm_w[l].reshape(1, d)

        wg = jnp.concatenate([w_igate_fwd[l], w_fgate_fwd[l], w_igate_bwd[l], w_fgate_bwd[l]], axis=1)
        wg = jnp.pad(wg, ((0, 0), (0, GATE_LANES - wg.shape[1])))
        wg = wg.reshape(3, M_WIDTH // MXU_DIM, MXU_DIM, GATE_LANES)
        qscale = M_HEAD_DIM ** -0.5
        wg = (wg * jnp.asarray([1.0 / qscale, 1.0, 1.0], F32).reshape(3, 1, 1, 1)).astype(BF16)
        bg = jnp.concatenate([b_igate_fwd[l], b_fgate_fwd[l], b_igate_bwd[l], b_fgate_bwd[l]])
        bg = jnp.pad(bg, (0, GATE_LANES - bg.shape[0])).reshape(1, GATE_LANES)
        wk_tiles = _block_diag_tiles(w_k[l])
        wkt_tiles = _block_diag_tiles(jnp.swapaxes(w_k[l], -1, -2))
        xf, xm, w_gate, w_zf, wf_bf, wm_bf, wo_bf = _inproj(
            x2, nw, w_in, w_fourier, w_mlstm, w_out, l)
        q, k, kt, v, xc, gates = _conv_qkv(
            xm, seq, 0.5 * conv_w[l], 0.5 * conv_b[l].reshape(1, M_WIDTH),
            _block_diag_tiles(w_q[l] * qscale), wk_tiles, wkt_tiles,
            _block_diag_tiles(w_v[l]), wg, bg)

        yt = _fourier_mix(xf, bsz, seq, consts)
        yf = jnp.transpose(yt, (0, 2, 1, 3)).reshape(t, F_WIDTH)

        g4 = gates[:, :4 * M_HEADS].reshape(bsz, nc, MCHUNK, 4, M_HEADS)
        g4 = jnp.transpose(g4, (0, 4, 3, 1, 2))
        hf, hb = _mlstm(q, k, kt, v, g4, bsz, seq)

        x2 = _merge(hf, hb, xc, yf, x2, nw, w_zf, w_gate,
                    hnorm_w[l].reshape(1, M_WIDTH), skip_w[l].reshape(1, M_WIDTH),
                    wf_bf, wm_bf, wo_bf, fn, final_norm=(l == depth - 1))
    return x2.reshape(bsz, seq, d)
```

```python
import functools

import numpy as np
import jax
import jax.numpy as jnp
from jax import lax
from jax.experimental import pallas as pl
from jax.experimental.pallas import tpu as pltpu

F32 = jnp.float32
BF16 = jnp.bfloat16

D_MODEL = 1024
F_WIDTH = D_MODEL
F_GROUPS = 4
F_GROUP_DIM = F_WIDTH // F_GROUPS
M_WIDTH = 2 * D_MODEL
M_HEADS = 4
M_HEAD_DIM = M_WIDTH // M_HEADS
QKV_BLOCK = 4
CONV_K = 5
EPS = 1e-6
IN_COLS = 2 * F_WIDTH + 3 * M_WIDTH + 2 * D_MODEL

MCHUNK = 256
MLSTM_HEADS_PER_STEP = 2
MLSTM_CHUNKS_PER_STEP = 2

COL_XF = 0
COL_ZF = F_WIDTH
COL_XM = 2 * F_WIDTH
COL_ZM = COL_XM + M_WIDTH
COL_OM = COL_ZM + M_WIDTH
COL_G = COL_OM + M_WIDTH
GCOL_ZM = 0
GCOL_OM = M_WIDTH
GCOL_G = 2 * M_WIDTH

LANES = 128
SUBLANES = 8
BF16_SUBLANES = 16
MXU_DIM = 256
GATE_LANES = 128
VMEM_LIMIT = 56 * 1024 * 1024

FFT_N2 = 32


def _dot(a, b):
    return jnp.dot(a, b, preferred_element_type=F32)


def _dot_nt(a, b):
    return lax.dot_general(a, b, (((1,), (1,)), ((), ())), preferred_element_type=F32)


def _rmsnorm_bf16(x, w):
    ms = jnp.mean(x * x, axis=-1, keepdims=True)
    return (x * lax.rsqrt(ms + EPS) * w).astype(BF16)


def _resident(shape, index_map):
    return pl.BlockSpec(shape, index_map, pipeline_mode=pl.Buffered(1))


def _inproj_kernel(x_ref, nw_ref, wf_ref, wm_ref, gzm_ref, gom_ref, gg_ref, gzf_ref, fo_ref, ml_ref,
                   ou_ref, xf_ref, xm_ref, wgate_ref, wzf_ref, wfo_ref, wml_ref, wou_ref,
                   *, sub_rows):
    wf = wf_ref[...].astype(BF16)
    wm = wm_ref[...].astype(BF16)
    for r in range(x_ref.shape[0] // sub_rows):
        rows = slice(sub_rows * r, sub_rows * (r + 1))
        h = _rmsnorm_bf16(x_ref[rows, :], nw_ref[...])
        xf_ref[rows, :] = _dot(h, wf).astype(BF16)
        xm_ref[rows, :] = _dot(h, wm).astype(BF16)
    wgate_ref[:, GCOL_ZM:GCOL_ZM + M_WIDTH] = (0.5 * gzm_ref[...]).astype(BF16)
    wgate_ref[:, GCOL_OM:GCOL_OM + M_WIDTH] = (0.5 * gom_ref[...]).astype(BF16)
    wgate_ref[:, GCOL_G:GCOL_G + 2 * D_MODEL] = (0.5 * gg_ref[...]).astype(BF16)
    wzf_ref[...] = (0.5 * gzf_ref[...]).astype(BF16)
    wfo_ref[...] = fo_ref[...].astype(BF16)
    wml_ref[...] = ml_ref[...].astype(BF16)
    wou_ref[...] = (0.5 * ou_ref[...]).astype(BF16)


def _inproj(x2, norm_w, w_in, w_fourier, w_mlstm, w_out, layer, tm=1024, sub_rows=256):
    t, d = x2.shape
    ns = t // tm
    wrows = d // ns
    assert d % ns == 0 and wrows % BF16_SUBLANES == 0 and M_WIDTH == 2 * d

    def in_chunk(rows, width, col):
        return pl.BlockSpec((None, rows, width), lambda i: (layer, i, col // width))

    def out_chunk(rows, width):
        return pl.BlockSpec((rows, width), lambda i: (i, 0))

    return pl.pallas_call(
        functools.partial(_inproj_kernel, sub_rows=sub_rows),
        grid=(ns,),
        in_specs=[
            pl.BlockSpec((tm, d), lambda i: (i, 0)),
            pl.BlockSpec((1, d), lambda i: (0, 0)),
            _resident((None, d, F_WIDTH), lambda i: (layer, 0, COL_XF // F_WIDTH)),
            _resident((None, d, M_WIDTH), lambda i: (layer, 0, COL_XM // M_WIDTH)),
            in_chunk(wrows, M_WIDTH, COL_ZM), in_chunk(wrows, M_WIDTH, COL_OM),
            in_chunk(wrows, 2 * D_MODEL, COL_G), in_chunk(wrows, F_WIDTH, COL_ZF),
            in_chunk(wrows, d, 0), in_chunk(2 * wrows, d, 0), in_chunk(wrows, d, 0),
        ],
        out_specs=[pl.BlockSpec((tm, F_WIDTH), lambda i: (i, 0)),
                   pl.BlockSpec((tm, M_WIDTH), lambda i: (i, 0)),
                   out_chunk(wrows, 3 * M_WIDTH), out_chunk(wrows, F_WIDTH),
                   out_chunk(wrows, d), out_chunk(2 * wrows, d), out_chunk(wrows, d)],
        out_shape=[jax.ShapeDtypeStruct((t, F_WIDTH), BF16),
                   jax.ShapeDtypeStruct((t, M_WIDTH), BF16),
                   jax.ShapeDtypeStruct((d, 3 * M_WIDTH), BF16),
                   jax.ShapeDtypeStruct((d, F_WIDTH), BF16),
                   jax.ShapeDtypeStruct((F_WIDTH, d), BF16),
                   jax.ShapeDtypeStruct((M_WIDTH, d), BF16),
                   jax.ShapeDtypeStruct((d, d), BF16)],
        compiler_params=pltpu.CompilerParams(
            dimension_semantics=("parallel",),
            vmem_limit_bytes=VMEM_LIMIT),
        name="inproj",
    )(x2, norm_w, w_in, w_in, w_in, w_in, w_in, w_in, w_fourier, w_mlstm, w_out)


def _fft_constants(seq):
    n2 = FFT_N2
    n1 = seq // n2
    sub = BF16_SUBLANES
    nblk = n1 // sub
    i = np.arange(nblk)[:, None, None, None]
    k2 = np.arange(n2)[None, :, None, None]
    j = np.arange(sub)[None, None, :, None]
    m2 = np.arange(n2)[None, None, None, :]
    ang = 2.0 * np.pi * (((m2 * k2) % n2) / n2 + (((sub * i + j) * k2) % seq) / seq)
    val = np.stack([np.cos(ang), -np.sin(ang)], axis=1)
    wa = np.zeros((nblk, 2, n2, sub, n2, sub), np.float64)
    for jj in range(sub):
        wa[:, :, :, jj, :, jj] = val[:, :, :, jj, :]
    wa = wa.reshape(nblk, 2 * n2 * sub, n2 * sub)
    a = np.arange(n1)
    angb = 2.0 * np.pi * ((a[:, None] * a[None, :]) % n1) / n1
    wr, wi = np.cos(angb), -np.sin(angb)
    wb = np.block([[wr, -wi], [wi, wr]])
    c = np.arange(F_GROUP_DIM)
    angc = 2.0 * np.pi * ((c[:, None] * c[None, :]) % F_GROUP_DIM) / F_GROUP_DIM
    wc = np.concatenate([np.cos(angc), np.sin(angc)], axis=0) / np.sqrt(seq * F_GROUP_DIM)
    return wa, wb, wc


def _fft_a_kernel(x_ref, wa_ref, o_ref, *, nsub):
    n2 = FFT_N2
    sub = BF16_SUBLANES
    c = x_ref.shape[-1]
    for i in range(nsub):
        rows = slice(sub * i, sub * (i + 1))
        xs = x_ref[:, rows, :].reshape(n2 * sub, c)
        z = _dot(wa_ref[i], xs)
        o_ref[:, :, rows, :] = z.astype(BF16).reshape(2, n2, sub, c)


def _fft_b_kernel(z_ref, wb_ref, wc_ref, o_ref):
    nk, n1, c = z_ref.shape[1:]
    for kk in range(nk):
        xs = z_ref[:, kk].reshape(2 * n1, c)
        g = _dot(wb_ref[...], xs).astype(BF16)
        for grp in range(c // F_GROUP_DIM):
            cols = slice(F_GROUP_DIM * grp, F_GROUP_DIM * (grp + 1))
            lhs = jnp.concatenate([g[:n1, cols], g[n1:, cols]], axis=1)
            o_ref[kk, :, cols] = _dot(lhs, wc_ref[...]).astype(BF16)


def _fourier_mix(xf, bsz, seq, consts, nsub=4, kb=8):
    wa, wb, wc = consts
    n2 = FFT_N2
    n1 = seq // n2
    rblk = nsub * BF16_SUBLANES
    proj4 = xf.reshape(bsz, n2, n1, F_WIDTH)
    za = pl.pallas_call(
        functools.partial(_fft_a_kernel, nsub=nsub),
        grid=(n1 // rblk, bsz),
        in_specs=[
            pl.BlockSpec((None, n2, rblk, F_WIDTH), lambda i, b: (b, 0, i, 0)),
            pl.BlockSpec((nsub,) + wa.shape[1:], lambda i, b: (i, 0, 0)),
        ],
        out_specs=pl.BlockSpec((None, 2, n2, rblk, F_WIDTH), lambda i, b: (b, 0, 0, i, 0)),
        out_shape=jax.ShapeDtypeStruct((bsz, 2, n2, n1, F_WIDTH), BF16),
        compiler_params=pltpu.CompilerParams(
            dimension_semantics=("parallel", "parallel"),
            vmem_limit_bytes=VMEM_LIMIT),
        name="fft_stage_a",
    )(proj4, wa)
    yt = pl.pallas_call(
        _fft_b_kernel,
        grid=(bsz, n2 // kb),
        in_specs=[
            pl.BlockSpec((None, 2, kb, n1, F_WIDTH), lambda b, k: (b, 0, k, 0, 0)),
            _resident(wb.shape, lambda b, k: (0, 0)),
            _resident(wc.shape, lambda b, k: (0, 0)),
        ],
        out_specs=pl.BlockSpec((None, kb, n1, F_WIDTH), lambda b, k: (b, k, 0, 0)),
        out_shape=jax.ShapeDtypeStruct((bsz, n2, n1, F_WIDTH), BF16),
        compiler_params=pltpu.CompilerParams(
            dimension_semantics=("parallel", "parallel"),
            vmem_limit_bytes=VMEM_LIMIT),
        name="fft_stage_b",
    )(za, wb, wc)
    return yt


def _log_sigmoid(x):
    return jnp.minimum(x, 0.0) - jnp.log1p(jnp.exp(-jnp.abs(x)))


def _conv_qkv_kernel(prev_ref, cur_ref, next_ref, cw_ref, cb_ref, wt_ref,
                     wg_ref, bg_ref, q_ref, k_ref, kt_ref, v_ref, xc_ref, g_ref, xs_ref,
                     *, tiles_per_seq):
    wq_ref, wk_ref, wkt_ref, wv_ref = (wt_ref.at[n] for n in range(4))
    tm = cur_ref.shape[0]
    halo = BF16_SUBLANES
    pos = pl.program_id(0) % tiles_per_seq
    keep_prev = (pos != 0).astype(F32)
    keep_next = (pos != tiles_per_seq - 1).astype(F32)
    xs_ref[0:halo, :] = prev_ref[...].astype(F32) * keep_prev
    xs_ref[halo:halo + tm, :] = cur_ref[...].astype(F32)
    xs_ref[halo + tm:, :] = next_ref[...].astype(F32) * keep_next

    gacc = jnp.zeros((tm, GATE_LANES), F32)
    for g in range(M_WIDTH // MXU_DIM):
        cols = slice(MXU_DIM * g, MXU_DIM * (g + 1))
        acc = jnp.broadcast_to(cb_ref[:, cols], (tm, MXU_DIM))
        ext = xs_ref[halo - SUBLANES:halo + tm + SUBLANES, cols]
        for j in range(CONV_K):
            shift = (CONV_K // 2 - j) % ext.shape[0]
            tap = ext if shift == 0 else pltpu.roll(ext, shift, axis=0)
            acc = acc + cw_ref[j:j + 1, cols] * tap[SUBLANES:SUBLANES + tm]
        xcb = (acc * (1.0 + jnp.tanh(acc))).astype(BF16)
        q = _dot(xcb, wq_ref[g])
        k = _dot(xcb, wk_ref[g])
        kt_ref[cols, :] = _dot_nt(wkt_ref[g], xcb).astype(BF16)
        v = _dot(cur_ref[:, cols], wv_ref[g])
        qb, kb, vb = q.astype(BF16), k.astype(BF16), v.astype(BF16)
        gacc = gacc + _dot(qb, wg_ref[0, g]) + _dot(kb, wg_ref[1, g]) + _dot(vb, wg_ref[2, g])
        q_ref[:, cols] = qb
        k_ref[:, cols] = kb
        v_ref[:, cols] = vb
        xc_ref[:, cols] = xcb
    gates = gacc + bg_ref[...]
    lane = lax.broadcasted_iota(jnp.int32, gates.shape, 1)
    is_forget = (lane // M_HEADS) % 2 == 1
    g_ref[...] = jnp.where(is_forget, _log_sigmoid(gates), gates)


def _conv_qkv(xm, seq, cw, cb, wt, wg, bg, tm=256):
    t = xm.shape[0]
    halo = BF16_SUBLANES
    hb = tm // halo
    nhalo = t // halo
    out_bf = jax.ShapeDtypeStruct((t, M_WIDTH), BF16)
    row_spec = pl.BlockSpec((tm, M_WIDTH), lambda i: (i, 0))

    def full(a):
        return _resident(a.shape, lambda i: (0,) * a.ndim)

    return pl.pallas_call(
        functools.partial(_conv_qkv_kernel, tiles_per_seq=seq // tm),
        grid=(t // tm,),
        in_specs=[
            pl.BlockSpec((halo, M_WIDTH), lambda i: (jnp.maximum(i * hb - 1, 0), 0)),
            row_spec,
            pl.BlockSpec((halo, M_WIDTH), lambda i: (jnp.minimum((i + 1) * hb, nhalo - 1), 0)),
            full(cw), full(cb), full(wt), full(wg), full(bg),
        ],
        out_specs=[row_spec, row_spec, pl.BlockSpec((M_WIDTH, tm), lambda i: (0, i)),
                   row_spec, row_spec,
                   pl.BlockSpec((tm, GATE_LANES), lambda i: (i, 0))],
        out_shape=[out_bf, out_bf, jax.ShapeDtypeStruct((M_WIDTH, t), BF16), out_bf, out_bf,
                   jax.ShapeDtypeStruct((t, GATE_LANES), F32)],
        scratch_shapes=[pltpu.VMEM((tm + 2 * halo, M_WIDTH), F32)],
        compiler_params=pltpu.CompilerParams(
            dimension_semantics=("parallel",),
            vmem_limit_bytes=VMEM_LIMIT),
        name="conv_qkv_gates",
    )(xm, xm, xm, cw, cb, wt, wg, bg)


GS_C, GS_M, GS_INTER, GS_ENEGM, GS_WS, GS_DECAY, GS_ROWS = 0, 1, 2, 3, 4, 5, 6


def _lane_scan(x, op, fill, reverse):
    n = x.shape[1]
    lane = lax.broadcasted_iota(jnp.int32, x.shape, 1)
    sh = 1
    while sh < n:
        if reverse:
            x = op(x, jnp.where(lane < n - sh, pltpu.roll(x, n - sh, axis=1), fill))
        else:
            x = op(x, jnp.where(lane >= sh, pltpu.roll(x, sh, axis=1), fill))
        sh *= 2
    return x


def _gate_prep(i_pre, log_f, gs_ref, tmp_ref, *, reverse):
    nc, L = i_pre.shape
    b = _lane_scan(log_f, jnp.add, 0.0, reverse)
    c = i_pre - b
    cm = _lane_scan(c, jnp.maximum, -jnp.inf, reverse)
    end = slice(0, 1) if reverse else slice(L - 1, L)
    g = b[:, end]
    cmt = cm[:, end]
    tmp_ref[0] = jnp.broadcast_to(g, (nc, LANES))
    tmp_ref[1] = jnp.broadcast_to(g + cmt, (nc, LANES))

    m = jnp.zeros((1, LANES), F32)
    for step in range(nc):
        idx = nc - 1 - step if reverse else step
        tmp_ref[2, idx:idx + 1, :] = m
        m = jnp.maximum(m + tmp_ref[0, idx:idx + 1, :], tmp_ref[1, idx:idx + 1, :])
    m_prev = tmp_ref[2][:, 0:1]
    big_m = jnp.maximum(m_prev, cm)
    mx = jnp.maximum(m_prev, cmt)
    gs_ref[GS_C] = c
    gs_ref[GS_M] = big_m
    gs_ref[GS_INTER] = jnp.exp(m_prev - big_m)
    gs_ref[GS_ENEGM] = jnp.exp(-b - big_m)
    gs_ref[GS_WS] = jnp.exp(c - mx)
    gs_ref[GS_DECAY] = jnp.broadcast_to(jnp.exp(m_prev - mx), (nc, L))


def _mlstm_chunk(q, k, kt, v, gs_ref, ci, ct_ref, n_ref, *, reverse):
    L = q.shape[0]

    def row(r):
        return gs_ref[r, pl.ds(ci, 1), :]

    c_row = row(GS_C)
    sub = lax.broadcasted_iota(jnp.int32, (SUBLANES, L), 0)
    packed = jnp.where(sub == 0, row(GS_M),
                       jnp.where(sub == 1, row(GS_INTER),
                                 jnp.where(sub == 2, row(GS_ENEGM), row(GS_WS))))
    tr = jnp.concatenate([packed, jnp.zeros((LANES - SUBLANES, L), F32)], axis=0).T
    m_col, inter, enegm, ws = tr[:, 0:1], tr[:, 1:2], tr[:, 2:3], tr[:, 3:4]
    decay = row(GS_DECAY)[:, 0:1]

    r_i = lax.broadcasted_iota(jnp.int32, (L, L), 0)
    c_i = lax.broadcasted_iota(jnp.int32, (L, L), 1)
    mask = (c_i >= r_i) if reverse else (c_i <= r_i)
    p = jnp.exp(jnp.where(mask, c_row - m_col, -jnp.inf))
    s = _dot_nt(q, k) * p
    qi = inter * q.astype(F32)
    num = _dot(s.astype(BF16), v) + _dot(qi.astype(BF16), ct_ref[...].astype(BF16))
    den = jnp.sum(s, axis=1, keepdims=True) + jnp.sum(qi * n_ref[...], axis=1, keepdims=True)
    h = num * (1.0 / jnp.maximum(jnp.abs(den), enegm))

    vw = ws.astype(BF16) * v
    ct_ref[...] = decay * ct_ref[...] + _dot(kt, vw)
    n_ref[...] = decay * n_ref[...] + jnp.sum(ws * k.astype(F32), axis=0, keepdims=True)
    return h


def _mlstm_kernel(gates_ref, qf_ref, kf_ref, ktf_ref, vf_ref, qb_ref, kb_ref, ktb_ref, vb_ref,
                  hf_ref, hb_ref,
                  ct_ref, n_ref, gs_ref, tmp_ref, *, hp, nc, cps):
    c = pl.program_id(2)
    dh = M_HEAD_DIM
    L = MCHUNK

    @pl.when(c == 0)
    def _():
        ct_ref[...] = jnp.zeros_like(ct_ref)
        n_ref[...] = jnp.zeros_like(n_ref)
        for hh in range(hp):
            _gate_prep(gates_ref[hh, 0], gates_ref[hh, 1], gs_ref.at[hh, 0], tmp_ref.at[hh, 0],
                       reverse=False)
            _gate_prep(gates_ref[hh, 2], gates_ref[hh, 3], gs_ref.at[hh, 1], tmp_ref.at[hh, 1],
                       reverse=True)

    for sub in range(cps):
        rf = slice(L * sub, L * (sub + 1))
        rb = slice(L * (cps - 1 - sub), L * (cps - sub))
        for hh in range(hp):
            cols = slice(dh * hh, dh * (hh + 1))
            hf = _mlstm_chunk(qf_ref[rf, cols], kf_ref[rf, cols], ktf_ref[cols, rf],
                              vf_ref[rf, cols], gs_ref.at[hh, 0], cps * c + sub,
                              ct_ref.at[hh, 0], n_ref.at[hh, 0], reverse=False)
            hf_ref[rf, cols] = hf.astype(hf_ref.dtype)
            hb = _mlstm_chunk(qb_ref[rb, cols], kb_ref[rb, cols], ktb_ref[cols, rb],
                              vb_ref[rb, cols], gs_ref.at[hh, 1], nc - 1 - (cps * c + sub),
                              ct_ref.at[hh, 1], n_ref.at[hh, 1], reverse=True)
            hb_ref[rb, cols] = hb.astype(hb_ref.dtype)


def _mlstm(q, k, kt, v, gates4, bsz, seq, hp=MLSTM_HEADS_PER_STEP, cps=MLSTM_CHUNKS_PER_STEP):
    t = q.shape[0]
    L = MCHUNK
    nc = seq // L
    dh = M_HEAD_DIM
    width = hp * dh
    rows = cps * L
    ns = nc // cps

    qkv_f = pl.BlockSpec((rows, width), lambda b, h, c: (b * ns + c, h))
    qkv_b = pl.BlockSpec((rows, width), lambda b, h, c: (b * ns + ns - 1 - c, h))
    kt_f = pl.BlockSpec((width, rows), lambda b, h, c: (h, b * ns + c))
    kt_b = pl.BlockSpec((width, rows), lambda b, h, c: (h, b * ns + ns - 1 - c))
    g_spec = pl.BlockSpec((None, hp, 4, nc, L), lambda b, h, c: (b, h, 0, 0, 0))
    out = jax.ShapeDtypeStruct((t, M_WIDTH), BF16)
    return pl.pallas_call(
        functools.partial(_mlstm_kernel, hp=hp, nc=nc, cps=cps),
        grid=(bsz, M_HEADS // hp, ns),
        in_specs=[g_spec, qkv_f, qkv_f, kt_f, qkv_f, qkv_b, qkv_b, kt_b, qkv_b],
        out_specs=[qkv_f, qkv_b],
        out_shape=[out, out],
        scratch_shapes=[
            pltpu.VMEM((hp, 2, dh, dh), F32),
            pltpu.VMEM((hp, 2, 1, dh), F32),
            pltpu.VMEM((hp, 2, GS_ROWS, nc, L), F32),
            pltpu.VMEM((hp, 2, 3, nc, LANES), F32),
        ],
        compiler_params=pltpu.CompilerParams(
            dimension_semantics=("parallel", "parallel", "arbitrary"),
            vmem_limit_bytes=VMEM_LIMIT),
        name="mlstm",
    )(gates4, q, k, kt, v, q, k, kt, v)


def _merge_kernel(hf_ref, hb_ref, xc_ref, yf_ref, x_ref, nw_ref, wzf_ref, wzm_ref, wom_ref, wg_ref,
                  hn_ref, sk_ref, wf_ref, wm_ref, wo_ref, fn_ref, o_ref, *, final_norm, sub_rows):
    for r in range(o_ref.shape[0] // sub_rows):
        rows = slice(sub_rows * r, sub_rows * (r + 1))
        hin = _rmsnorm_bf16(x_ref[rows, :], nw_ref[...])
        y_b = None
        for h in range(M_HEADS):
            cols = slice(M_HEAD_DIM * h, M_HEAD_DIM * (h + 1))
            hc2 = (hf_ref[rows, cols].astype(F32) + hb_ref[rows, cols].astype(F32))
            hc2 = hc2 * (1.0 + jnp.tanh(_dot(hin, wom_ref[:, cols])))
            hc = hc2 * lax.rsqrt(jnp.mean(hc2 * hc2, axis=-1, keepdims=True) + 4.0 * EPS)
            u = hc * hn_ref[:, cols] + sk_ref[:, cols] * xc_ref[rows, cols].astype(F32)
            hz = _dot(hin, wzm_ref[:, cols])
            u = u * (hz * (1.0 + jnp.tanh(hz)))
            part = _dot(u.astype(BF16), wm_ref[cols, :])
            y_b = part if y_b is None else y_b + part
        hzf = _dot(hin, wzf_ref[...])
        ya_in = yf_ref[rows, :].astype(F32) * (hzf * (1.0 + jnp.tanh(hzf)))
        y_a = _dot(ya_in.astype(BF16), wf_ref[...])
        t_a = jnp.tanh(_dot(hin, wg_ref[:, :D_MODEL]))
        t_b = jnp.tanh(_dot(hin, wg_ref[:, D_MODEL:]))
        merged2 = (y_a + y_b) + (t_a * y_a + t_b * y_b)
        xn = x_ref[rows, :] + _dot(merged2.astype(BF16), wo_ref[...])
        if final_norm:
            xn = xn * lax.rsqrt(jnp.mean(xn * xn, axis=-1, keepdims=True) + EPS) * fn_ref[...]
        o_ref[rows, :] = xn


def _merge(hf, hb, xc, yf, x2, nw, w_zf, w_gate, hn, sk, wf, wm, wo, fn, final_norm,
           tm=512, sub_rows=512):
    t, d = x2.shape

    def rows(width):
        return pl.BlockSpec((tm, width), lambda i: (i, 0))

    def full(a):
        return _resident(a.shape, lambda i: (0,) * a.ndim)

    def w_gate_cols(col, width):
        return _resident((d, width), lambda i: (0, col // width))

    return pl.pallas_call(
        functools.partial(_merge_kernel, final_norm=final_norm, sub_rows=sub_rows),
        grid=(t // tm,),
        in_specs=[
            rows(M_WIDTH), rows(M_WIDTH), rows(M_WIDTH), rows(F_WIDTH), rows(D_MODEL),
            full(nw),
            full(w_zf), w_gate_cols(GCOL_ZM, M_WIDTH),
            w_gate_cols(GCOL_OM, M_WIDTH), w_gate_cols(GCOL_G, 2 * D_MODEL),
            full(hn), full(sk), full(wf), full(wm), full(wo), full(fn),
        ],
        out_specs=rows(D_MODEL),
        out_shape=jax.ShapeDtypeStruct((t, D_MODEL), F32),
        compiler_params=pltpu.CompilerParams(
            dimension_semantics=("parallel",),
            vmem_limit_bytes=VMEM_LIMIT),
        name="merge_out",
    )(hf, hb, xc, yf, x2, nw, w_zf, w_gate, w_gate, w_gate, hn, sk, wf, wm, wo, fn)


def _block_diag_tiles(w):
    rows = w.reshape(-1, MXU_DIM, QKV_BLOCK)
    dense = jnp.tile(rows, (1, 1, MXU_DIM // QKV_BLOCK))
    r = lax.broadcasted_iota(jnp.int32, (MXU_DIM, MXU_DIM), 0) // QKV_BLOCK
    c = lax.broadcasted_iota(jnp.int32, (MXU_DIM, MXU_DIM), 1) // QKV_BLOCK
    tiles = jnp.where(r == c, dense, 0.0).astype(BF16)
    return tiles.reshape(w.shape[0], -1, MXU_DIM, MXU_DIM)


def kernel(x, norm_w, w_in, w_fourier, conv_w, conv_b, w_q, w_k, w_v,
           w_igate_fwd, b_igate_fwd, w_fgate_fwd, b_fgate_fwd,
           w_igate_bwd, b_igate_bwd, w_fgate_bwd, b_fgate_bwd,
           hnorm_w, skip_w, w_mlstm, w_out, final_norm_w):
    bsz, seq, d = x.shape
    depth = w_in.shape[0]
    assert d == D_MODEL and seq % (FFT_N2 * BF16_SUBLANES * 2) == 0 and seq % MCHUNK == 0
    t = bsz * seq
    nc = seq // MCHUNK
    consts = tuple(jnp.asarray(a, dtype=F32).astype(BF16) for a in _fft_constants(seq))
    fn = final_norm_w.reshape(1, d)

    x2 = x.reshape(t, d)
    for l in range(depth):
        nw = norm_w[l].reshape(1, d)

        wg = jnp.concatenate([w_igate_fwd[l], w_fgate_fwd[l], w_igate_bwd[l], w_fgate_bwd[l]], axis=1)
        wg = jnp.pad(wg, ((0, 0), (0, GATE_LANES - wg.shape[1])))
        wg = wg.reshape(3, M_WIDTH // MXU_DIM, MXU_DIM, GATE_LANES)
        qscale = M_HEAD_DIM ** -0.5
        wg = (wg * jnp.asarray([1.0 / qscale, 1.0, 1.0], F32).reshape(3, 1, 1, 1)).astype(BF16)
        bg = jnp.concatenate([b_igate_fwd[l], b_fgate_fwd[l], b_igate_bwd[l], b_fgate_bwd[l]])
        bg = jnp.pad(bg, (0, GATE_LANES - bg.shape[0])).reshape(1, GATE_LANES)
        wt = _block_diag_tiles(jnp.stack(
            [w_q[l] * qscale, w_k[l], jnp.swapaxes(w_k[l], -1, -2), w_v[l]]))
        xf, xm, w_gate, w_zf, wf_bf, wm_bf, wo_bf = _inproj(
            x2, nw, w_in, w_fourier, w_mlstm, w_out, l)
        q, k, kt, v, xc, gates = _conv_qkv(
            xm, seq, 0.5 * conv_w[l], 0.5 * conv_b[l].reshape(1, M_WIDTH), wt, wg, bg)

        yt = _fourier_mix(xf, bsz, seq, consts)
        yf = jnp.transpose(yt, (0, 2, 1, 3)).reshape(t, F_WIDTH)

        g4 = gates[:, :4 * M_HEADS].reshape(bsz, nc, MCHUNK, 4, M_HEADS)
        g4 = jnp.transpose(g4, (0, 4, 3, 1, 2))
        hf, hb = _mlstm(q, k, kt, v, g4, bsz, seq)

        x2 = _merge(hf, hb, xc, yf, x2, nw, w_zf, w_gate,
                    hnorm_w[l].reshape(1, M_WIDTH), skip_w[l].reshape(1, M_WIDTH),
                    wf_bf, wm_bf, wo_bf, fn, final_norm=(l == depth - 1))
    return x2.reshape(bsz, seq, d)
```

```python
import functools

import numpy as np
import jax
import jax.numpy as jnp
from jax import lax
from jax.experimental import pallas as pl
from jax.experimental.pallas import tpu as pltpu

F32 = jnp.float32
BF16 = jnp.bfloat16

D_MODEL = 1024
F_WIDTH = D_MODEL
F_GROUPS = 4
F_GROUP_DIM = F_WIDTH // F_GROUPS
M_WIDTH = 2 * D_MODEL
M_HEADS = 4
M_HEAD_DIM = M_WIDTH // M_HEADS
QKV_BLOCK = 4
CONV_K = 5
EPS = 1e-6
IN_COLS = 2 * F_WIDTH + 3 * M_WIDTH + 2 * D_MODEL

MCHUNK = 256
MLSTM_HEADS_PER_STEP = 2
MLSTM_CHUNKS_PER_STEP = 2

COL_XF = 0
COL_ZF = F_WIDTH
COL_XM = 2 * F_WIDTH
COL_ZM = COL_XM + M_WIDTH
COL_OM = COL_ZM + M_WIDTH
COL_G = COL_OM + M_WIDTH
GCOL_ZM = 0
GCOL_OM = M_WIDTH
GCOL_G = 2 * M_WIDTH

LANES = 128
SUBLANES = 8
BF16_SUBLANES = 16
MXU_DIM = 256
GATE_LANES = 128
VMEM_LIMIT = 56 * 1024 * 1024

FFT_N2 = 32


def _dot(a, b):
    return jnp.dot(a, b, preferred_element_type=F32)


def _dot_nt(a, b):
    return lax.dot_general(a, b, (((1,), (1,)), ((), ())), preferred_element_type=F32)


def _rmsnorm_bf16(x, w):
    ms = jnp.mean(x * x, axis=-1, keepdims=True)
    return (x * lax.rsqrt(ms + EPS) * w).astype(BF16)


def _resident(shape, index_map):
    return pl.BlockSpec(shape, index_map, pipeline_mode=pl.Buffered(1))


def _inproj_kernel(x_ref, nw_ref, wf_ref, wm_ref, gzm_ref, gom_ref, gg_ref, gzf_ref, fo_ref, ml_ref,
                   ou_ref, xf_ref, xm_ref, wgate_ref, wzf_ref, wfo_ref, wml_ref, wou_ref,
                   *, sub_rows):
    wf = wf_ref[...].astype(BF16)
    wm = wm_ref[...].astype(BF16)
    for r in range(x_ref.shape[0] // sub_rows):
        rows = slice(sub_rows * r, sub_rows * (r + 1))
        h = _rmsnorm_bf16(x_ref[rows, :], nw_ref[...])
        xf_ref[rows, :] = _dot(h, wf).astype(BF16)
        xm_ref[rows, :] = _dot(h, wm).astype(BF16)
    wgate_ref[:, GCOL_ZM:GCOL_ZM + M_WIDTH] = (0.5 * gzm_ref[...]).astype(BF16)
    wgate_ref[:, GCOL_OM:GCOL_OM + M_WIDTH] = (0.5 * gom_ref[...]).astype(BF16)
    wgate_ref[:, GCOL_G:GCOL_G + 2 * D_MODEL] = (0.5 * gg_ref[...]).astype(BF16)
    wzf_ref[...] = (0.5 * gzf_ref[...]).astype(BF16)
    wfo_ref[...] = fo_ref[...].astype(BF16)
    wml_ref[...] = ml_ref[...].astype(BF16)
    wou_ref[...] = (0.5 * ou_ref[...]).astype(BF16)


def _inproj(x2, norm_w, w_in, w_fourier, w_mlstm, w_out, layer, tm=1024, sub_rows=256):
    t, d = x2.shape
    ns = t // tm
    wrows = d // ns
    assert d % ns == 0 and wrows % BF16_SUBLANES == 0 and M_WIDTH == 2 * d

    def in_chunk(rows, width, col):
        return pl.BlockSpec((None, rows, width), lambda i: (layer, i, col // width))

    def out_chunk(rows, width):
        return pl.BlockSpec((rows, width), lambda i: (i, 0))

    return pl.pallas_call(
        functools.partial(_inproj_kernel, sub_rows=sub_rows),
        grid=(ns,),
        in_specs=[
            pl.BlockSpec((tm, d), lambda i: (i, 0)),
            pl.BlockSpec((1, d), lambda i: (0, 0)),
            _resident((None, d, F_WIDTH), lambda i: (layer, 0, COL_XF // F_WIDTH)),
            _resident((None, d, M_WIDTH), lambda i: (layer, 0, COL_XM // M_WIDTH)),
            in_chunk(wrows, M_WIDTH, COL_ZM), in_chunk(wrows, M_WIDTH, COL_OM),
            in_chunk(wrows, 2 * D_MODEL, COL_G), in_chunk(wrows, F_WIDTH, COL_ZF),
            in_chunk(wrows, d, 0), in_chunk(2 * wrows, d, 0), in_chunk(wrows, d, 0),
        ],
        out_specs=[pl.BlockSpec((tm, F_WIDTH), lambda i: (i, 0)),
                   pl.BlockSpec((tm, M_WIDTH), lambda i: (i, 0)),
                   out_chunk(wrows, 3 * M_WIDTH), out_chunk(wrows, F_WIDTH),
                   out_chunk(wrows, d), out_chunk(2 * wrows, d), out_chunk(wrows, d)],
        out_shape=[jax.ShapeDtypeStruct((t, F_WIDTH), BF16),
                   jax.ShapeDtypeStruct((t, M_WIDTH), BF16),
                   jax.ShapeDtypeStruct((d, 3 * M_WIDTH), BF16),
                   jax.ShapeDtypeStruct((d, F_WIDTH), BF16),
                   jax.ShapeDtypeStruct((F_WIDTH, d), BF16),
                   jax.ShapeDtypeStruct((M_WIDTH, d), BF16),
                   jax.ShapeDtypeStruct((d, d), BF16)],
        compiler_params=pltpu.CompilerParams(
            dimension_semantics=("parallel",),
            vmem_limit_bytes=VMEM_LIMIT),
        name="inproj",
    )(x2, norm_w, w_in, w_in, w_in, w_in, w_in, w_in, w_fourier, w_mlstm, w_out)


def _fft_constants(seq):
    n2 = FFT_N2
    n1 = seq // n2
    sub = BF16_SUBLANES
    nblk = n1 // sub
    i = np.arange(nblk)[:, None, None, None]
    k2 = np.arange(n2)[None, :, None, None]
    j = np.arange(sub)[None, None, :, None]
    m2 = np.arange(n2)[None, None, None, :]
    ang = 2.0 * np.pi * (((m2 * k2) % n2) / n2 + (((sub * i + j) * k2) % seq) / seq)
    val = np.stack([np.cos(ang), -np.sin(ang)], axis=1)
    wa = np.zeros((nblk, 2, n2, sub, n2, sub), np.float64)
    for jj in range(sub):
        wa[:, :, :, jj, :, jj] = val[:, :, :, jj, :]
    wa = wa.reshape(nblk, 2 * n2 * sub, n2 * sub)
    a = np.arange(n1)
    angb = 2.0 * np.pi * ((a[:, None] * a[None, :]) % n1) / n1
    wr, wi = np.cos(angb), -np.sin(angb)
    wb = np.block([[wr, -wi], [wi, wr]])
    c = np.arange(F_GROUP_DIM)
    angc = 2.0 * np.pi * ((c[:, None] * c[None, :]) % F_GROUP_DIM) / F_GROUP_DIM
    wc = np.concatenate([np.cos(angc), np.sin(angc)], axis=0) / np.sqrt(seq * F_GROUP_DIM)
    return wa, wb, wc


def _fft_a_kernel(x_ref, wa_ref, o_ref, *, nsub):
    n2 = FFT_N2
    sub = BF16_SUBLANES
    c = x_ref.shape[-1]
    for i in range(nsub):
        rows = slice(sub * i, sub * (i + 1))
        xs = x_ref[:, rows, :].reshape(n2 * sub, c)
        z = _dot(wa_ref[i], xs)
        o_ref[:, :, rows, :] = z.astype(BF16).reshape(2, n2, sub, c)


def _fft_b_kernel(z_ref, wb_ref, wc_ref, o_ref):
    nk, n1, c = z_ref.shape[1:]
    for kk in range(nk):
        xs = z_ref[:, kk].reshape(2 * n1, c)
        g = _dot(wb_ref[...], xs).astype(BF16)
        for grp in range(c // F_GROUP_DIM):
            cols = slice(F_GROUP_DIM * grp, F_GROUP_DIM * (grp + 1))
            lhs = jnp.concatenate([g[:n1, cols], g[n1:, cols]], axis=1)
            o_ref[kk, :, cols] = _dot(lhs, wc_ref[...]).astype(BF16)


def _fourier_mix(xf, bsz, seq, consts, nsub=4, kb=8):
    wa, wb, wc = consts
    n2 = FFT_N2
    n1 = seq // n2
    rblk = nsub * BF16_SUBLANES
    proj4 = xf.reshape(bsz, n2, n1, F_WIDTH)
    za = pl.pallas_call(
        functools.partial(_fft_a_kernel, nsub=nsub),
        grid=(n1 // rblk, bsz),
        in_specs=[
            pl.BlockSpec((None, n2, rblk, F_WIDTH), lambda i, b: (b, 0, i, 0)),
            pl.BlockSpec((nsub,) + wa.shape[1:], lambda i, b: (i, 0, 0)),
        ],
        out_specs=pl.BlockSpec((None, 2, n2, rblk, F_WIDTH), lambda i, b: (b, 0, 0, i, 0)),
        out_shape=jax.ShapeDtypeStruct((bsz, 2, n2, n1, F_WIDTH), BF16),
        compiler_params=pltpu.CompilerParams(
            dimension_semantics=("parallel", "parallel"),
            vmem_limit_bytes=VMEM_LIMIT),
        name="fft_stage_a",
    )(proj4, wa)
    yt = pl.pallas_call(
        _fft_b_kernel,
        grid=(bsz, n2 // kb),
        in_specs=[
            pl.BlockSpec((None, 2, kb, n1, F_WIDTH), lambda b, k: (b, 0, k, 0, 0)),
            _resident(wb.shape, lambda b, k: (0, 0)),
            _resident(wc.shape, lambda b, k: (0, 0)),
        ],
        out_specs=pl.BlockSpec((None, kb, n1, F_WIDTH), lambda b, k: (b, k, 0, 0)),
        out_shape=jax.ShapeDtypeStruct((bsz, n2, n1, F_WIDTH), BF16),
        compiler_params=pltpu.CompilerParams(
            dimension_semantics=("parallel", "parallel"),
            vmem_limit_bytes=VMEM_LIMIT),
        name="fft_stage_b",
    )(za, wb, wc)
    return yt


def _log_sigmoid(x):
    return jnp.minimum(x, 0.0) - jnp.log1p(jnp.exp(-jnp.abs(x)))


def _conv_qkv_kernel(prev_ref, cur_ref, next_ref, cw_ref, cb_ref, wrow_ref, spread_ref,
                     wg_ref, bg_ref, q_ref, k_ref, kt_ref, v_ref, xc_ref, g_ref, xs_ref, wt_ref,
                     *, tiles_per_seq):
    @pl.when(pl.program_id(0) == 0)
    def _():
        r = lax.broadcasted_iota(jnp.int32, (MXU_DIM, MXU_DIM), 0) // QKV_BLOCK
        c = lax.broadcasted_iota(jnp.int32, (MXU_DIM, MXU_DIM), 1) // QKV_BLOCK
        for n in range(wt_ref.shape[0]):
            for g in range(wt_ref.shape[1]):
                tile = _dot(wrow_ref[n, g], spread_ref[...])
                wt_ref[n, g] = jnp.where(r == c, tile, 0.0).astype(BF16)

    wq_ref, wk_ref, wkt_ref, wv_ref = (wt_ref.at[n] for n in range(4))
    tm = cur_ref.shape[0]
    halo = BF16_SUBLANES
    pos = pl.program_id(0) % tiles_per_seq
    keep_prev = (pos != 0).astype(F32)
    keep_next = (pos != tiles_per_seq - 1).astype(F32)
    xs_ref[0:halo, :] = prev_ref[...].astype(F32) * keep_prev
    xs_ref[halo:halo + tm, :] = cur_ref[...].astype(F32)
    xs_ref[halo + tm:, :] = next_ref[...].astype(F32) * keep_next

    gacc = jnp.zeros((tm, GATE_LANES), F32)
    for g in range(M_WIDTH // MXU_DIM):
        cols = slice(MXU_DIM * g, MXU_DIM * (g + 1))
        acc = jnp.broadcast_to(cb_ref[:, cols], (tm, MXU_DIM))
        ext = xs_ref[halo - SUBLANES:halo + tm + SUBLANES, cols]
        for j in range(CONV_K):
            shift = (CONV_K // 2 - j) % ext.shape[0]
            tap = ext if shift == 0 else pltpu.roll(ext, shift, axis=0)
            acc = acc + cw_ref[j:j + 1, cols] * tap[SUBLANES:SUBLANES + tm]
        xcb = (acc * (1.0 + jnp.tanh(acc))).astype(BF16)
        q = _dot(xcb, wq_ref[g])
        k = _dot(xcb, wk_ref[g])
        kt_ref[cols, :] = _dot_nt(wkt_ref[g], xcb).astype(BF16)
        v = _dot(cur_ref[:, cols], wv_ref[g])
        qb, kb, vb = q.astype(BF16), k.astype(BF16), v.astype(BF16)
        gacc = gacc + _dot(qb, wg_ref[0, g]) + _dot(kb, wg_ref[1, g]) + _dot(vb, wg_ref[2, g])
        q_ref[:, cols] = qb
        k_ref[:, cols] = kb
        v_ref[:, cols] = vb
        xc_ref[:, cols] = xcb
    gates = gacc + bg_ref[...]
    lane = lax.broadcasted_iota(jnp.int32, gates.shape, 1)
    is_forget = (lane // M_HEADS) % 2 == 1
    g_ref[...] = jnp.where(is_forget, _log_sigmoid(gates), gates)


def _conv_qkv(xm, seq, cw, cb, wrow, spread, wg, bg, tm=256):
    t = xm.shape[0]
    halo = BF16_SUBLANES
    hb = tm // halo
    nhalo = t // halo
    out_bf = jax.ShapeDtypeStruct((t, M_WIDTH), BF16)
    row_spec = pl.BlockSpec((tm, M_WIDTH), lambda i: (i, 0))

    def full(a):
        return _resident(a.shape, lambda i: (0,) * a.ndim)

    return pl.pallas_call(
        functools.partial(_conv_qkv_kernel, tiles_per_seq=seq // tm),
        grid=(t // tm,),
        in_specs=[
            pl.BlockSpec((halo, M_WIDTH), lambda i: (jnp.maximum(i * hb - 1, 0), 0)),
            row_spec,
            pl.BlockSpec((halo, M_WIDTH), lambda i: (jnp.minimum((i + 1) * hb, nhalo - 1), 0)),
            full(cw), full(cb), full(wrow), full(spread), full(wg), full(bg),
        ],
        out_specs=[row_spec, row_spec, pl.BlockSpec((M_WIDTH, tm), lambda i: (0, i)),
                   row_spec, row_spec,
                   pl.BlockSpec((tm, GATE_LANES), lambda i: (i, 0))],
        out_shape=[out_bf, out_bf, jax.ShapeDtypeStruct((M_WIDTH, t), BF16), out_bf, out_bf,
                   jax.ShapeDtypeStruct((t, GATE_LANES), F32)],
        scratch_shapes=[pltpu.VMEM((tm + 2 * halo, M_WIDTH), F32),
                        pltpu.VMEM(wrow.shape[:3] + (MXU_DIM,), BF16)],
        compiler_params=pltpu.CompilerParams(
            dimension_semantics=("arbitrary",),
            vmem_limit_bytes=VMEM_LIMIT),
        name="conv_qkv_gates",
    )(xm, xm, xm, cw, cb, wrow, spread, wg, bg)


GS_C, GS_M, GS_INTER, GS_ENEGM, GS_WS, GS_DECAY, GS_ROWS = 0, 1, 2, 3, 4, 5, 6


def _lane_scan(x, op, fill, reverse):
    n = x.shape[1]
    lane = lax.broadcasted_iota(jnp.int32, x.shape, 1)
    sh = 1
    while sh < n:
        if reverse:
            x = op(x, jnp.where(lane < n - sh, pltpu.roll(x, n - sh, axis=1), fill))
        else:
            x = op(x, jnp.where(lane >= sh, pltpu.roll(x, sh, axis=1), fill))
        sh *= 2
    return x


def _gate_prep(i_pre, log_f, gs_ref, tmp_ref, *, reverse):
    nc, L = i_pre.shape
    b = _lane_scan(log_f, jnp.add, 0.0, reverse)
    c = i_pre - b
    cm = _lane_scan(c, jnp.maximum, -jnp.inf, reverse)
    end = slice(0, 1) if reverse else slice(L - 1, L)
    g = b[:, end]
    cmt = cm[:, end]
    tmp_ref[0] = jnp.broadcast_to(g, (nc, LANES))
    tmp_ref[1] = jnp.broadcast_to(g + cmt, (nc, LANES))

    m = jnp.zeros((1, LANES), F32)
    for step in range(nc):
        idx = nc - 1 - step if reverse else step
        tmp_ref[2, idx:idx + 1, :] = m
        m = jnp.maximum(m + tmp_ref[0, idx:idx + 1, :], tmp_ref[1, idx:idx + 1, :])
    m_prev = tmp_ref[2][:, 0:1]
    big_m = jnp.maximum(m_prev, cm)
    mx = jnp.maximum(m_prev, cmt)
    gs_ref[GS_C] = c
    gs_ref[GS_M] = big_m
    gs_ref[GS_INTER] = jnp.exp(m_prev - big_m)
    gs_ref[GS_ENEGM] = jnp.exp(-b - big_m)
    gs_ref[GS_WS] = jnp.exp(c - mx)
    gs_ref[GS_DECAY] = jnp.broadcast_to(jnp.exp(m_prev - mx), (nc, L))


def _mlstm_chunk(q, k, kt, v, gs_ref, ci, ct_ref, n_ref, *, reverse):
    L = q.shape[0]

    def row(r):
        return gs_ref[r, pl.ds(ci, 1), :]

    c_row = row(GS_C)
    sub = lax.broadcasted_iota(jnp.int32, (SUBLANES, L), 0)
    packed = jnp.where(sub == 0, row(GS_M),
                       jnp.where(sub == 1, row(GS_INTER),
                                 jnp.where(sub == 2, row(GS_ENEGM), row(GS_WS))))
    tr = jnp.concatenate([packed, jnp.zeros((LANES - SUBLANES, L), F32)], axis=0).T
    m_col, inter, enegm, ws = tr[:, 0:1], tr[:, 1:2], tr[:, 2:3], tr[:, 3:4]
    decay = row(GS_DECAY)[:, 0:1]

    r_i = lax.broadcasted_iota(jnp.int32, (L, L), 0)
    c_i = lax.broadcasted_iota(jnp.int32, (L, L), 1)
    mask = (c_i >= r_i) if reverse else (c_i <= r_i)
    p = jnp.exp(jnp.where(mask, c_row - m_col, -jnp.inf))
    s = _dot_nt(q, k) * p
    qi = inter * q.astype(F32)
    num = _dot(s.astype(BF16), v) + _dot(qi.astype(BF16), ct_ref[...].astype(BF16))
    den = jnp.sum(s, axis=1, keepdims=True) + jnp.sum(qi * n_ref[...], axis=1, keepdims=True)
    h = num * (1.0 / jnp.maximum(jnp.abs(den), enegm))

    vw = ws.astype(BF16) * v
    ct_ref[...] = decay * ct_ref[...] + _dot(kt, vw)
    n_ref[...] = decay * n_ref[...] + jnp.sum(ws * k.astype(F32), axis=0, keepdims=True)
    return h


def _mlstm_kernel(gates_ref, qf_ref, kf_ref, ktf_ref, vf_ref, qb_ref, kb_ref, ktb_ref, vb_ref,
                  hf_ref, hb_ref,
                  ct_ref, n_ref, gs_ref, tmp_ref, *, hp, nc, cps):
    c = pl.program_id(2)
    dh = M_HEAD_DIM
    L = MCHUNK

    @pl.when(c == 0)
    def _():
        ct_ref[...] = jnp.zeros_like(ct_ref)
        n_ref[...] = jnp.zeros_like(n_ref)
        for hh in range(hp):
            _gate_prep(gates_ref[hh, 0], gates_ref[hh, 1], gs_ref.at[hh, 0], tmp_ref.at[hh, 0],
                       reverse=False)
            _gate_prep(gates_ref[hh, 2], gates_ref[hh, 3], gs_ref.at[hh, 1], tmp_ref.at[hh, 1],
                       reverse=True)

    for sub in range(cps):
        rf = slice(L * sub, L * (sub + 1))
        rb = slice(L * (cps - 1 - sub), L * (cps - sub))
        for hh in range(hp):
            cols = slice(dh * hh, dh * (hh + 1))
            hf = _mlstm_chunk(qf_ref[rf, cols], kf_ref[rf, cols], ktf_ref[cols, rf],
                              vf_ref[rf, cols], gs_ref.at[hh, 0], cps * c + sub,
                              ct_ref.at[hh, 0], n_ref.at[hh, 0], reverse=False)
            hf_ref[rf, cols] = hf.astype(hf_ref.dtype)
            hb = _mlstm_chunk(qb_ref[rb, cols], kb_ref[rb, cols], ktb_ref[cols, rb],
                              vb_ref[rb, cols], gs_ref.at[hh, 1], nc - 1 - (cps * c + sub),
                              ct_ref.at[hh, 1], n_ref.at[hh, 1], reverse=True)
            hb_ref[rb, cols] = hb.astype(hb_ref.dtype)


def _mlstm(q, k, kt, v, gates4, bsz, seq, hp=MLSTM_HEADS_PER_STEP, cps=MLSTM_CHUNKS_PER_STEP):
    t = q.shape[0]
    L = MCHUNK
    nc = seq // L
    dh = M_HEAD_DIM
    width = hp * dh
    rows = cps * L
    ns = nc // cps

    qkv_f = pl.BlockSpec((rows, width), lambda b, h, c: (b * ns + c, h))
    qkv_b = pl.BlockSpec((rows, width), lambda b, h, c: (b * ns + ns - 1 - c, h))
    kt_f = pl.BlockSpec((width, rows), lambda b, h, c: (h, b * ns + c))
    kt_b = pl.BlockSpec((width, rows), lambda b, h, c: (h, b * ns + ns - 1 - c))
    g_spec = pl.BlockSpec((None, hp, 4, nc, L), lambda b, h, c: (b, h, 0, 0, 0))
    out = jax.ShapeDtypeStruct((t, M_WIDTH), BF16)
    return pl.pallas_call(
        functools.partial(_mlstm_kernel, hp=hp, nc=nc, cps=cps),
        grid=(bsz, M_HEADS // hp, ns),
        in_specs=[g_spec, qkv_f, qkv_f, kt_f, qkv_f, qkv_b, qkv_b, kt_b, qkv_b],
        out_specs=[qkv_f, qkv_b],
        out_shape=[out, out],
        scratch_shapes=[
            pltpu.VMEM((hp, 2, dh, dh), F32),
            pltpu.VMEM((hp, 2, 1, dh), F32),
            pltpu.VMEM((hp, 2, GS_ROWS, nc, L), F32),
            pltpu.VMEM((hp, 2, 3, nc, LANES), F32),
        ],
        compiler_params=pltpu.CompilerParams(
            dimension_semantics=("parallel", "parallel", "arbitrary"),
            vmem_limit_bytes=VMEM_LIMIT),
        name="mlstm",
    )(gates4, q, k, kt, v, q, k, kt, v)


def _merge_kernel(hf_ref, hb_ref, xc_ref, yf_ref, x_ref, nw_ref, wzf_ref, wzm_ref, wom_ref, wg_ref,
                  hn_ref, sk_ref, wf_ref, wm_ref, wo_ref, fn_ref, o_ref, *, final_norm, sub_rows):
    for r in range(o_ref.shape[0] // sub_rows):
        rows = slice(sub_rows * r, sub_rows * (r + 1))
        hin = _rmsnorm_bf16(x_ref[rows, :], nw_ref[...])
        y_b = None
        for h in range(M_HEADS):
            cols = slice(M_HEAD_DIM * h, M_HEAD_DIM * (h + 1))
            hc2 = (hf_ref[rows, cols].astype(F32) + hb_ref[rows, cols].astype(F32))
            hc2 = hc2 * (1.0 + jnp.tanh(_dot(hin, wom_ref[:, cols])))
            hc = hc2 * lax.rsqrt(jnp.mean(hc2 * hc2, axis=-1, keepdims=True) + 4.0 * EPS)
            u = hc * hn_ref[:, cols] + sk_ref[:, cols] * xc_ref[rows, cols].astype(F32)
            hz = _dot(hin, wzm_ref[:, cols])
            u = u * (hz * (1.0 + jnp.tanh(hz)))
            part = _dot(u.astype(BF16), wm_ref[cols, :])
            y_b = part if y_b is None else y_b + part
        hzf = _dot(hin, wzf_ref[...])
        ya_in = yf_ref[rows, :].astype(F32) * (hzf * (1.0 + jnp.tanh(hzf)))
        y_a = _dot(ya_in.astype(BF16), wf_ref[...])
        t_a = jnp.tanh(_dot(hin, wg_ref[:, :D_MODEL]))
        t_b = jnp.tanh(_dot(hin, wg_ref[:, D_MODEL:]))
        merged2 = (y_a + y_b) + (t_a * y_a + t_b * y_b)
        xn = x_ref[rows, :] + _dot(merged2.astype(BF16), wo_ref[...])
        if final_norm:
            xn = xn * lax.rsqrt(jnp.mean(xn * xn, axis=-1, keepdims=True) + EPS) * fn_ref[...]
        o_ref[rows, :] = xn


def _merge(hf, hb, xc, yf, x2, nw, w_zf, w_gate, hn, sk, wf, wm, wo, fn, final_norm,
           tm=512, sub_rows=512):
    t, d = x2.shape

    def rows(width):
        return pl.BlockSpec((tm, width), lambda i: (i, 0))

    def full(a):
        return _resident(a.shape, lambda i: (0,) * a.ndim)

    def w_gate_cols(col, width):
        return _resident((d, width), lambda i: (0, col // width))

    return pl.pallas_call(
        functools.partial(_merge_kernel, final_norm=final_norm, sub_rows=sub_rows),
        grid=(t // tm,),
        in_specs=[
            rows(M_WIDTH), rows(M_WIDTH), rows(M_WIDTH), rows(F_WIDTH), rows(D_MODEL),
            full(nw),
            full(w_zf), w_gate_cols(GCOL_ZM, M_WIDTH),
            w_gate_cols(GCOL_OM, M_WIDTH), w_gate_cols(GCOL_G, 2 * D_MODEL),
            full(hn), full(sk), full(wf), full(wm), full(wo), full(fn),
        ],
        out_specs=rows(D_MODEL),
        out_shape=jax.ShapeDtypeStruct((t, D_MODEL), F32),
        compiler_params=pltpu.CompilerParams(
            dimension_semantics=("parallel",),
            vmem_limit_bytes=VMEM_LIMIT),
        name="merge_out",
    )(hf, hb, xc, yf, x2, nw, w_zf, w_gate, w_gate, w_gate, hn, sk, wf, wm, wo, fn)


def _block_rows(ws):
    return jnp.stack(ws).reshape(len(ws), -1, MXU_DIM, QKV_BLOCK).astype(BF16)


def _spread_matrix():
    d = np.arange(QKV_BLOCK)[:, None]
    c = np.arange(MXU_DIM)[None, :]
    return jnp.asarray((c % QKV_BLOCK == d).astype(np.float32), dtype=BF16)


def kernel(x, norm_w, w_in, w_fourier, conv_w, conv_b, w_q, w_k, w_v,
           w_igate_fwd, b_igate_fwd, w_fgate_fwd, b_fgate_fwd,
           w_igate_bwd, b_igate_bwd, w_fgate_bwd, b_fgate_bwd,
           hnorm_w, skip_w, w_mlstm, w_out, final_norm_w):
    bsz, seq, d = x.shape
    depth = w_in.shape[0]
    assert d == D_MODEL and seq % (FFT_N2 * BF16_SUBLANES * 2) == 0 and seq % MCHUNK == 0
    t = bsz * seq
    nc = seq // MCHUNK
    consts = tuple(jnp.asarray(a, dtype=F32).astype(BF16) for a in _fft_constants(seq))
    fn = final_norm_w.reshape(1, d)

    x2 = x.reshape(t, d)
    for l in range(depth):
        nw = norm_w[l].reshape(1, d)

        wg = jnp.concatenate([w_igate_fwd[l], w_fgate_fwd[l], w_igate_bwd[l], w_fgate_bwd[l]], axis=1)
        wg = jnp.pad(wg, ((0, 0), (0, GATE_LANES - wg.shape[1])))
        wg = wg.reshape(3, M_WIDTH // MXU_DIM, MXU_DIM, GATE_LANES)
        qscale = M_HEAD_DIM ** -0.5
        wg = (wg * jnp.asarray([1.0 / qscale, 1.0, 1.0], F32).reshape(3, 1, 1, 1)).astype(BF16)
        bg = jnp.concatenate([b_igate_fwd[l], b_fgate_fwd[l], b_igate_bwd[l], b_fgate_bwd[l]])
        bg = jnp.pad(bg, (0, GATE_LANES - bg.shape[0])).reshape(1, GATE_LANES)
        wrow = _block_rows([w_q[l] * qscale, w_k[l], jnp.swapaxes(w_k[l], -1, -2), w_v[l]])
        xf, xm, w_gate, w_zf, wf_bf, wm_bf, wo_bf = _inproj(
            x2, nw, w_in, w_fourier, w_mlstm, w_out, l)
        q, k, kt, v, xc, gates = _conv_qkv(
            xm, seq, 0.5 * conv_w[l], 0.5 * conv_b[l].reshape(1, M_WIDTH),
            wrow, _spread_matrix(), wg, bg)

        yt = _fourier_mix(xf, bsz, seq, consts)
        yf = jnp.transpose(yt, (0, 2, 1, 3)).reshape(t, F_WIDTH)

        g4 = gates[:, :4 * M_HEADS].reshape(bsz, nc, MCHUNK, 4, M_HEADS)
        g4 = jnp.transpose(g4, (0, 4, 3, 1, 2))
        hf, hb = _mlstm(q, k, kt, v, g4, bsz, seq)

        x2 = _merge(hf, hb, xc, yf, x2, nw, w_zf, w_gate,
                    hnorm_w[l].reshape(1, M_WIDTH), skip_w[l].reshape(1, M_WIDTH),
                    wf_bf, wm_bf, wo_bf, fn, final_norm=(l == depth - 1))
    return x2.reshape(bsz, seq, d)
```

```python
import functools

import numpy as np
import jax
import jax.numpy as jnp
from jax import lax
from jax.experimental import pallas as pl
from jax.experimental.pallas import tpu as pltpu

F32 = jnp.float32
BF16 = jnp.bfloat16

D_MODEL = 1024
F_WIDTH = D_MODEL
F_GROUPS = 4
F_GROUP_DIM = F_WIDTH // F_GROUPS
M_WIDTH = 2 * D_MODEL
M_HEADS = 4
M_HEAD_DIM = M_WIDTH // M_HEADS
QKV_BLOCK = 4
CONV_K = 5
EPS = 1e-6
IN_COLS = 2 * F_WIDTH + 3 * M_WIDTH + 2 * D_MODEL

MCHUNK = 256
MLSTM_HEADS_PER_STEP = 2
MLSTM_CHUNKS_PER_STEP = 2

COL_XF = 0
COL_ZF = F_WIDTH
COL_XM = 2 * F_WIDTH
COL_ZM = COL_XM + M_WIDTH
COL_OM = COL_ZM + M_WIDTH
COL_G = COL_OM + M_WIDTH
GCOL_ZM = 0
GCOL_OM = M_WIDTH
GCOL_G = 2 * M_WIDTH

LANES = 128
SUBLANES = 8
BF16_SUBLANES = 16
MXU_DIM = 256
GATE_LANES = 128
VMEM_LIMIT = 56 * 1024 * 1024

FFT_N2 = 32


def _dot(a, b):
    return jnp.dot(a, b, preferred_element_type=F32)


def _dot_nt(a, b):
    return lax.dot_general(a, b, (((1,), (1,)), ((), ())), preferred_element_type=F32)


def _rmsnorm_bf16(x, w):
    ms = jnp.mean(x * x, axis=-1, keepdims=True)
    return (x * lax.rsqrt(ms + EPS) * w).astype(BF16)


def _resident(shape, index_map):
    return pl.BlockSpec(shape, index_map, pipeline_mode=pl.Buffered(1))


def _inproj_kernel(x_ref, nw_ref, wf_ref, wm_ref, gzm_ref, gom_ref, gg_ref, gzf_ref, fo_ref, ml_ref,
                   ou_ref, xf_ref, xm_ref, wgate_ref, wzf_ref, wfo_ref, wml_ref, wou_ref,
                   *, sub_rows):
    wf = wf_ref[...].astype(BF16)
    wm = wm_ref[...].astype(BF16)
    for r in range(x_ref.shape[0] // sub_rows):
        rows = slice(sub_rows * r, sub_rows * (r + 1))
        h = _rmsnorm_bf16(x_ref[rows, :], nw_ref[...])
        xf_ref[rows, :] = _dot(h, wf).astype(BF16)
        xm_ref[rows, :] = _dot(h, wm).astype(BF16)
    wgate_ref[:, GCOL_ZM:GCOL_ZM + M_WIDTH] = (0.5 * gzm_ref[...]).astype(BF16)
    wgate_ref[:, GCOL_OM:GCOL_OM + M_WIDTH] = (0.5 * gom_ref[...]).astype(BF16)
    wgate_ref[:, GCOL_G:GCOL_G + 2 * D_MODEL] = (0.5 * gg_ref[...]).astype(BF16)
    wzf_ref[...] = (0.5 * gzf_ref[...]).astype(BF16)
    wfo_ref[...] = fo_ref[...].astype(BF16)
    wml_ref[...] = ml_ref[...].astype(BF16)
    wou_ref[...] = (0.5 * ou_ref[...]).astype(BF16)


def _inproj(x2, norm_w, w_in, w_fourier, w_mlstm, w_out, layer, tm=1024, sub_rows=256):
    t, d = x2.shape
    ns = t // tm
    wrows = d // ns
    assert d % ns == 0 and wrows % BF16_SUBLANES == 0 and M_WIDTH == 2 * d

    def in_chunk(rows, width, col):
        return pl.BlockSpec((None, rows, width), lambda i: (layer, i, col // width))

    def out_chunk(rows, width):
        return pl.BlockSpec((rows, width), lambda i: (i, 0))

    return pl.pallas_call(
        functools.partial(_inproj_kernel, sub_rows=sub_rows),
        grid=(ns,),
        in_specs=[
            pl.BlockSpec((tm, d), lambda i: (i, 0)),
            pl.BlockSpec((1, d), lambda i: (0, 0)),
            _resident((None, d, F_WIDTH), lambda i: (layer, 0, COL_XF // F_WIDTH)),
            _resident((None, d, M_WIDTH), lambda i: (layer, 0, COL_XM // M_WIDTH)),
            in_chunk(wrows, M_WIDTH, COL_ZM), in_chunk(wrows, M_WIDTH, COL_OM),
            in_chunk(wrows, 2 * D_MODEL, COL_G), in_chunk(wrows, F_WIDTH, COL_ZF),
            in_chunk(wrows, d, 0), in_chunk(2 * wrows, d, 0), in_chunk(wrows, d, 0),
        ],
        out_specs=[pl.BlockSpec((tm, F_WIDTH), lambda i: (i, 0)),
                   pl.BlockSpec((tm, M_WIDTH), lambda i: (i, 0)),
                   out_chunk(wrows, 3 * M_WIDTH), out_chunk(wrows, F_WIDTH),
                   out_chunk(wrows, d), out_chunk(2 * wrows, d), out_chunk(wrows, d)],
        out_shape=[jax.ShapeDtypeStruct((t, F_WIDTH), BF16),
                   jax.ShapeDtypeStruct((t, M_WIDTH), BF16),
                   jax.ShapeDtypeStruct((d, 3 * M_WIDTH), BF16),
                   jax.ShapeDtypeStruct((d, F_WIDTH), BF16),
                   jax.ShapeDtypeStruct((F_WIDTH, d), BF16),
                   jax.ShapeDtypeStruct((M_WIDTH, d), BF16),
                   jax.ShapeDtypeStruct((d, d), BF16)],
        compiler_params=pltpu.CompilerParams(
            dimension_semantics=("parallel",),
            vmem_limit_bytes=VMEM_LIMIT),
        name="inproj",
    )(x2, norm_w, w_in, w_in, w_in, w_in, w_in, w_in, w_fourier, w_mlstm, w_out)


def _fft_constants(seq):
    n2 = FFT_N2
    n1 = seq // n2
    sub = BF16_SUBLANES
    nblk = n1 // sub
    i = np.arange(nblk)[:, None, None, None]
    k2 = np.arange(n2)[None, :, None, None]
    j = np.arange(sub)[None, None, :, None]
    m2 = np.arange(n2)[None, None, None, :]
    ang = 2.0 * np.pi * (((m2 * k2) % n2) / n2 + (((sub * i + j) * k2) % seq) / seq)
    val = np.stack([np.cos(ang), -np.sin(ang)], axis=1)
    wa = np.zeros((nblk, 2, n2, sub, n2, sub), np.float64)
    for jj in range(sub):
        wa[:, :, :, jj, :, jj] = val[:, :, :, jj, :]
    wa = wa.reshape(nblk, 2 * n2 * sub, n2 * sub)
    a = np.arange(n1)
    angb = 2.0 * np.pi * ((a[:, None] * a[None, :]) % n1) / n1
    wr, wi = np.cos(angb), -np.sin(angb)
    wb = np.block([[wr, -wi], [wi, wr]])
    c = np.arange(F_GROUP_DIM)
    angc = 2.0 * np.pi * ((c[:, None] * c[None, :]) % F_GROUP_DIM) / F_GROUP_DIM
    wc = np.concatenate([np.cos(angc), np.sin(angc)], axis=0) / np.sqrt(seq * F_GROUP_DIM)
    return wa, wb, wc


def _fft_a_kernel(x_ref, wa_ref, o_ref, *, nsub):
    n2 = FFT_N2
    sub = BF16_SUBLANES
    c = x_ref.shape[-1]
    for i in range(nsub):
        rows = slice(sub * i, sub * (i + 1))
        xs = x_ref[:, rows, :].reshape(n2 * sub, c)
        z = _dot(wa_ref[i], xs)
        o_ref[:, :, rows, :] = z.astype(BF16).reshape(2, n2, sub, c)


def _fft_b_kernel(z_ref, wb_ref, wc_ref, o_ref):
    nk, n1, c = z_ref.shape[1:]
    for kk in range(nk):
        xs = z_ref[:, kk].reshape(2 * n1, c)
        g = _dot(wb_ref[...], xs).astype(BF16)
        for grp in range(c // F_GROUP_DIM):
            cols = slice(F_GROUP_DIM * grp, F_GROUP_DIM * (grp + 1))
            lhs = jnp.concatenate([g[:n1, cols], g[n1:, cols]], axis=1)
            o_ref[kk, :, cols] = _dot(lhs, wc_ref[...]).astype(BF16)


def _fourier_mix(xf, bsz, seq, consts, nsub=4, kb=8):
    wa, wb, wc = consts
    n2 = FFT_N2
    n1 = seq // n2
    rblk = nsub * BF16_SUBLANES
    proj4 = xf.reshape(bsz, n2, n1, F_WIDTH)
    za = pl.pallas_call(
        functools.partial(_fft_a_kernel, nsub=nsub),
        grid=(n1 // rblk, bsz),
        in_specs=[
            pl.BlockSpec((None, n2, rblk, F_WIDTH), lambda i, b: (b, 0, i, 0)),
            pl.BlockSpec((nsub,) + wa.shape[1:], lambda i, b: (i, 0, 0)),
        ],
        out_specs=pl.BlockSpec((None, 2, n2, rblk, F_WIDTH), lambda i, b: (b, 0, 0, i, 0)),
        out_shape=jax.ShapeDtypeStruct((bsz, 2, n2, n1, F_WIDTH), BF16),
        compiler_params=pltpu.CompilerParams(
            dimension_semantics=("parallel", "parallel"),
            vmem_limit_bytes=VMEM_LIMIT),
        name="fft_stage_a",
    )(proj4, wa)
    yt = pl.pallas_call(
        _fft_b_kernel,
        grid=(bsz, n2 // kb),
        in_specs=[
            pl.BlockSpec((None, 2, kb, n1, F_WIDTH), lambda b, k: (b, 0, k, 0, 0)),
            _resident(wb.shape, lambda b, k: (0, 0)),
            _resident(wc.shape, lambda b, k: (0, 0)),
        ],
        out_specs=pl.BlockSpec((None, kb, n1, F_WIDTH), lambda b, k: (b, k, 0, 0)),
        out_shape=jax.ShapeDtypeStruct((bsz, n2, n1, F_WIDTH), BF16),
        compiler_params=pltpu.CompilerParams(
            dimension_semantics=("parallel", "parallel"),
            vmem_limit_bytes=VMEM_LIMIT),
        name="fft_stage_b",
    )(za, wb, wc)
    return yt


def _log_sigmoid(x):
    return jnp.minimum(x, 0.0) - jnp.log1p(jnp.exp(-jnp.abs(x)))


def _conv_qkv_kernel(prev_ref, cur_ref, next_ref, cw_ref, cb_ref, wrow_ref, spread_ref,
                     wg_ref, bg_ref, q_ref, k_ref, kt_ref, v_ref, xc_ref, g_ref, xs_ref, wt_ref,
                     *, tiles_per_seq):
    @pl.when(pl.program_id(0) == 0)
    def _():
        r = lax.broadcasted_iota(jnp.int32, (MXU_DIM, MXU_DIM), 0) // QKV_BLOCK
        c = lax.broadcasted_iota(jnp.int32, (MXU_DIM, MXU_DIM), 1) // QKV_BLOCK
        for n in range(wt_ref.shape[0]):
            for g in range(wt_ref.shape[1]):
                tile = _dot(wrow_ref[n, g], spread_ref[...])
                wt_ref[n, g] = jnp.where(r == c, tile, 0.0).astype(BF16)

    wq_ref, wk_ref, wkt_ref, wv_ref = (wt_ref.at[n] for n in range(4))
    tm = cur_ref.shape[0]
    halo = BF16_SUBLANES
    pos = pl.program_id(0) % tiles_per_seq
    keep_prev = (pos != 0).astype(F32)
    keep_next = (pos != tiles_per_seq - 1).astype(F32)
    xs_ref[0:halo, :] = prev_ref[...].astype(F32) * keep_prev
    xs_ref[halo:halo + tm, :] = cur_ref[...].astype(F32)
    xs_ref[halo + tm:, :] = next_ref[...].astype(F32) * keep_next

    gacc = jnp.zeros((tm, GATE_LANES), F32)
    for g in range(M_WIDTH // MXU_DIM):
        cols = slice(MXU_DIM * g, MXU_DIM * (g + 1))
        acc = jnp.broadcast_to(cb_ref[:, cols], (tm, MXU_DIM))
        ext = xs_ref[halo - SUBLANES:halo + tm + SUBLANES, cols]
        for j in range(CONV_K):
            shift = (CONV_K // 2 - j) % ext.shape[0]
            tap = ext if shift == 0 else pltpu.roll(ext, shift, axis=0)
            acc = acc + cw_ref[j:j + 1, cols] * tap[SUBLANES:SUBLANES + tm]
        xcb = (acc * (1.0 + jnp.tanh(acc))).astype(BF16)
        q = _dot(xcb, wq_ref[g])
        k = _dot(xcb, wk_ref[g])
        kt_ref[cols, :] = _dot_nt(wkt_ref[g], xcb).astype(BF16)
        v = _dot(cur_ref[:, cols], wv_ref[g])
        qb, kb, vb = q.astype(BF16), k.astype(BF16), v.astype(BF16)
        gacc = gacc + _dot(qb, wg_ref[0, g]) + _dot(kb, wg_ref[1, g]) + _dot(vb, wg_ref[2, g])
        q_ref[:, cols] = qb
        k_ref[:, cols] = kb
        v_ref[:, cols] = vb
        xc_ref[:, cols] = xcb
    gates = gacc + bg_ref[...]
    lane = lax.broadcasted_iota(jnp.int32, gates.shape, 1)
    is_forget = (lane // M_HEADS) % 2 == 1
    g_ref[...] = jnp.where(is_forget, _log_sigmoid(gates), gates)


def _conv_qkv(xm, seq, cw, cb, wrow, spread, wg, bg, tm=256):
    t = xm.shape[0]
    halo = BF16_SUBLANES
    hb = tm // halo
    nhalo = t // halo
    out_bf = jax.ShapeDtypeStruct((t, M_WIDTH), BF16)
    row_spec = pl.BlockSpec((tm, M_WIDTH), lambda i: (i, 0))

    def full(a):
        return _resident(a.shape, lambda i: (0,) * a.ndim)

    return pl.pallas_call(
        functools.partial(_conv_qkv_kernel, tiles_per_seq=seq // tm),
        grid=(t // tm,),
        in_specs=[
            pl.BlockSpec((halo, M_WIDTH), lambda i: (jnp.maximum(i * hb - 1, 0), 0)),
            row_spec,
            pl.BlockSpec((halo, M_WIDTH), lambda i: (jnp.minimum((i + 1) * hb, nhalo - 1), 0)),
            full(cw), full(cb), full(wrow), full(spread), full(wg), full(bg),
        ],
        out_specs=[row_spec, row_spec, pl.BlockSpec((M_WIDTH, tm), lambda i: (0, i)),
                   row_spec, row_spec,
                   pl.BlockSpec((tm, GATE_LANES), lambda i: (i, 0))],
        out_shape=[out_bf, out_bf, jax.ShapeDtypeStruct((M_WIDTH, t), BF16), out_bf, out_bf,
                   jax.ShapeDtypeStruct((t, GATE_LANES), F32)],
        scratch_shapes=[pltpu.VMEM((tm + 2 * halo, M_WIDTH), F32),
                        pltpu.VMEM(wrow.shape[:3] + (MXU_DIM,), BF16)],
        compiler_params=pltpu.CompilerParams(
            dimension_semantics=("arbitrary",),
            vmem_limit_bytes=VMEM_LIMIT),
        name="conv_qkv_gates",
    )(xm, xm, xm, cw, cb, wrow, spread, wg, bg)


GS_C, GS_M, GS_INTER, GS_ENEGM, GS_WS, GS_DECAY, GS_ROWS = 0, 1, 2, 3, 4, 5, 6


def _lane_scan(x, op, fill, reverse):
    n = x.shape[1]
    lane = lax.broadcasted_iota(jnp.int32, x.shape, 1)
    sh = 1
    while sh < n:
        if reverse:
            x = op(x, jnp.where(lane < n - sh, pltpu.roll(x, n - sh, axis=1), fill))
        else:
            x = op(x, jnp.where(lane >= sh, pltpu.roll(x, sh, axis=1), fill))
        sh *= 2
    return x


def _gate_prep(i_pre, log_f, gs_ref, tmp_ref, *, reverse):
    nc, L = i_pre.shape
    b = _lane_scan(log_f, jnp.add, 0.0, reverse)
    c = i_pre - b
    cm = _lane_scan(c, jnp.maximum, -jnp.inf, reverse)
    end = slice(0, 1) if reverse else slice(L - 1, L)
    g = b[:, end]
    cmt = cm[:, end]
    tmp_ref[0] = jnp.broadcast_to(g, (nc, LANES))
    tmp_ref[1] = jnp.broadcast_to(g + cmt, (nc, LANES))

    m = jnp.zeros((1, LANES), F32)
    for step in range(nc):
        idx = nc - 1 - step if reverse else step
        tmp_ref[2, idx:idx + 1, :] = m
        m = jnp.maximum(m + tmp_ref[0, idx:idx + 1, :], tmp_ref[1, idx:idx + 1, :])
    m_prev = tmp_ref[2][:, 0:1]
    big_m = jnp.maximum(m_prev, cm)
    mx = jnp.maximum(m_prev, cmt)
    gs_ref[GS_C] = c
    gs_ref[GS_M] = big_m
    gs_ref[GS_INTER] = jnp.exp(m_prev - big_m)
    gs_ref[GS_ENEGM] = jnp.exp(-b - big_m)
    gs_ref[GS_WS] = jnp.exp(c - mx)
    gs_ref[GS_DECAY] = jnp.broadcast_to(jnp.exp(m_prev - mx), (nc, L))


def _mlstm_chunk(q, k, kt, v, gs_ref, ci, ct_ref, n_ref, *, reverse):
    L = q.shape[0]

    def row(r):
        return gs_ref[r, pl.ds(ci, 1), :]

    c_row = row(GS_C)
    sub = lax.broadcasted_iota(jnp.int32, (SUBLANES, L), 0)
    packed = jnp.where(sub == 0, row(GS_M),
                       jnp.where(sub == 1, row(GS_INTER),
                                 jnp.where(sub == 2, row(GS_ENEGM), row(GS_WS))))
    tr = jnp.concatenate([packed, jnp.zeros((LANES - SUBLANES, L), F32)], axis=0).T
    m_col, inter, enegm, ws = tr[:, 0:1], tr[:, 1:2], tr[:, 2:3], tr[:, 3:4]
    decay = row(GS_DECAY)[:, 0:1]

    r_i = lax.broadcasted_iota(jnp.int32, (L, L), 0)
    c_i = lax.broadcasted_iota(jnp.int32, (L, L), 1)
    mask = (c_i >= r_i) if reverse else (c_i <= r_i)
    p = jnp.exp(jnp.where(mask, c_row - m_col, -jnp.inf))
    s = _dot_nt(q, k) * p
    qi = inter * q.astype(F32)
    num = _dot(s.astype(BF16), v) + _dot(qi.astype(BF16), ct_ref[...].astype(BF16))
    den = jnp.sum(s, axis=1, keepdims=True) + jnp.sum(qi * n_ref[...], axis=1, keepdims=True)
    inv = 1.0 / jnp.maximum(jnp.abs(den), enegm)

    vw = ws.astype(BF16) * v
    ct_ref[...] = decay * ct_ref[...] + _dot(kt, vw)
    n_ref[...] = decay * n_ref[...] + jnp.sum(ws * k.astype(F32), axis=0, keepdims=True)
    return num, inv


def _mlstm_kernel(gates_ref, qf_ref, kf_ref, ktf_ref, vf_ref, qb_ref, kb_ref, ktb_ref, vb_ref,
                  hf_ref, hb_ref, if_ref, ib_ref,
                  ct_ref, n_ref, gs_ref, tmp_ref, *, hp, nc, cps):
    c = pl.program_id(2)
    dh = M_HEAD_DIM
    L = MCHUNK

    @pl.when(c == 0)
    def _():
        ct_ref[...] = jnp.zeros_like(ct_ref)
        n_ref[...] = jnp.zeros_like(n_ref)
        for hh in range(hp):
            _gate_prep(gates_ref[hh, 0], gates_ref[hh, 1], gs_ref.at[hh, 0], tmp_ref.at[hh, 0],
                       reverse=False)
            _gate_prep(gates_ref[hh, 2], gates_ref[hh, 3], gs_ref.at[hh, 1], tmp_ref.at[hh, 1],
                       reverse=True)

    for sub in range(cps):
        rf = slice(L * sub, L * (sub + 1))
        rb = slice(L * (cps - 1 - sub), L * (cps - sub))
        lane = lax.broadcasted_iota(jnp.int32, (L, LANES), 1)
        inv_f = jnp.zeros((L, LANES), F32)
        inv_b = jnp.zeros((L, LANES), F32)
        for hh in range(hp):
            cols = slice(dh * hh, dh * (hh + 1))
            hf, ivf = _mlstm_chunk(qf_ref[rf, cols], kf_ref[rf, cols], ktf_ref[cols, rf],
                                   vf_ref[rf, cols], gs_ref.at[hh, 0], cps * c + sub,
                                   ct_ref.at[hh, 0], n_ref.at[hh, 0], reverse=False)
            hf_ref[rf, cols] = hf.astype(hf_ref.dtype)
            hb, ivb = _mlstm_chunk(qb_ref[rb, cols], kb_ref[rb, cols], ktb_ref[cols, rb],
                                   vb_ref[rb, cols], gs_ref.at[hh, 1], nc - 1 - (cps * c + sub),
                                   ct_ref.at[hh, 1], n_ref.at[hh, 1], reverse=True)
            hb_ref[rb, cols] = hb.astype(hb_ref.dtype)
            inv_f = jnp.where(lane == hh, ivf, inv_f)
            inv_b = jnp.where(lane == hh, ivb, inv_b)
        if_ref[rf, :] = inv_f
        ib_ref[rb, :] = inv_b


def _mlstm(q, k, kt, v, gates4, bsz, seq, hp=MLSTM_HEADS_PER_STEP, cps=MLSTM_CHUNKS_PER_STEP):
    t = q.shape[0]
    L = MCHUNK
    nc = seq // L
    dh = M_HEAD_DIM
    width = hp * dh
    rows = cps * L
    ns = nc // cps

    qkv_f = pl.BlockSpec((rows, width), lambda b, h, c: (b * ns + c, h))
    qkv_b = pl.BlockSpec((rows, width), lambda b, h, c: (b * ns + ns - 1 - c, h))
    kt_f = pl.BlockSpec((width, rows), lambda b, h, c: (h, b * ns + c))
    kt_b = pl.BlockSpec((width, rows), lambda b, h, c: (h, b * ns + ns - 1 - c))
    g_spec = pl.BlockSpec((None, hp, 4, nc, L), lambda b, h, c: (b, h, 0, 0, 0))
    out = jax.ShapeDtypeStruct((t, M_WIDTH), BF16)
    inv_out = jax.ShapeDtypeStruct((t, M_HEADS // hp * LANES), F32)
    return pl.pallas_call(
        functools.partial(_mlstm_kernel, hp=hp, nc=nc, cps=cps),
        grid=(bsz, M_HEADS // hp, ns),
        in_specs=[g_spec, qkv_f, qkv_f, kt_f, qkv_f, qkv_b, qkv_b, kt_b, qkv_b],
        out_specs=[qkv_f, qkv_b,
                   pl.BlockSpec((rows, LANES), lambda b, h, c: (b * ns + c, h)),
                   pl.BlockSpec((rows, LANES), lambda b, h, c: (b * ns + ns - 1 - c, h))],
        out_shape=[out, out, inv_out, inv_out],
        scratch_shapes=[
            pltpu.VMEM((hp, 2, dh, dh), F32),
            pltpu.VMEM((hp, 2, 1, dh), F32),
            pltpu.VMEM((hp, 2, GS_ROWS, nc, L), F32),
            pltpu.VMEM((hp, 2, 3, nc, LANES), F32),
        ],
        compiler_params=pltpu.CompilerParams(
            dimension_semantics=("parallel", "parallel", "arbitrary"),
            vmem_limit_bytes=VMEM_LIMIT),
        name="mlstm",
    )(gates4, q, k, kt, v, q, k, kt, v)


def _merge_kernel(hf_ref, hb_ref, if_ref, ib_ref, xc_ref, yf_ref, x_ref, nw_ref, wzf_ref, wzm_ref, wom_ref, wg_ref,
                  hn_ref, sk_ref, wf_ref, wm_ref, wo_ref, fn_ref, o_ref, *, final_norm, sub_rows):
    for r in range(o_ref.shape[0] // sub_rows):
        rows = slice(sub_rows * r, sub_rows * (r + 1))
        hin = _rmsnorm_bf16(x_ref[rows, :], nw_ref[...])
        y_b = None
        for h in range(M_HEADS):
            cols = slice(M_HEAD_DIM * h, M_HEAD_DIM * (h + 1))
            ic = (h // MLSTM_HEADS_PER_STEP) * LANES + h % MLSTM_HEADS_PER_STEP
            hc2 = (hf_ref[rows, cols].astype(F32) * if_ref[rows, ic:ic + 1]
                   + hb_ref[rows, cols].astype(F32) * ib_ref[rows, ic:ic + 1])
            hc2 = hc2 * (1.0 + jnp.tanh(_dot(hin, wom_ref[:, cols])))
            hc = hc2 * lax.rsqrt(jnp.mean(hc2 * hc2, axis=-1, keepdims=True) + 4.0 * EPS)
            u = hc * hn_ref[:, cols] + sk_ref[:, cols] * xc_ref[rows, cols].astype(F32)
            hz = _dot(hin, wzm_ref[:, cols])
            u = u * (hz * (1.0 + jnp.tanh(hz)))
            part = _dot(u.astype(BF16), wm_ref[cols, :])
            y_b = part if y_b is None else y_b + part
        hzf = _dot(hin, wzf_ref[...])
        ya_in = yf_ref[rows, :].astype(F32) * (hzf * (1.0 + jnp.tanh(hzf)))
        y_a = _dot(ya_in.astype(BF16), wf_ref[...])
        t_a = jnp.tanh(_dot(hin, wg_ref[:, :D_MODEL]))
        t_b = jnp.tanh(_dot(hin, wg_ref[:, D_MODEL:]))
        merged2 = (y_a + y_b) + (t_a * y_a + t_b * y_b)
        xn = x_ref[rows, :] + _dot(merged2.astype(BF16), wo_ref[...])
        if final_norm:
            xn = xn * lax.rsqrt(jnp.mean(xn * xn, axis=-1, keepdims=True) + EPS) * fn_ref[...]
        o_ref[rows, :] = xn


def _merge(hf, hb, inv_f, inv_b, xc, yf, x2, nw, w_zf, w_gate, hn, sk, wf, wm, wo, fn, final_norm,
           tm=512, sub_rows=512):
    t, d = x2.shape

    def rows(width):
        return pl.BlockSpec((tm, width), lambda i: (i, 0))

    def full(a):
        return _resident(a.shape, lambda i: (0,) * a.ndim)

    def w_gate_cols(col, width):
        return _resident((d, width), lambda i: (0, col // width))

    return pl.pallas_call(
        functools.partial(_merge_kernel, final_norm=final_norm, sub_rows=sub_rows),
        grid=(t // tm,),
        in_specs=[
            rows(M_WIDTH), rows(M_WIDTH), rows(inv_f.shape[1]), rows(inv_b.shape[1]),
            rows(M_WIDTH), rows(F_WIDTH), rows(D_MODEL),
            full(nw),
            full(w_zf), w_gate_cols(GCOL_ZM, M_WIDTH),
            w_gate_cols(GCOL_OM, M_WIDTH), w_gate_cols(GCOL_G, 2 * D_MODEL),
            full(hn), full(sk), full(wf), full(wm), full(wo), full(fn),
        ],
        out_specs=rows(D_MODEL),
        out_shape=jax.ShapeDtypeStruct((t, D_MODEL), F32),
        compiler_params=pltpu.CompilerParams(
            dimension_semantics=("parallel",),
            vmem_limit_bytes=VMEM_LIMIT),
        name="merge_out",
    )(hf, hb, inv_f, inv_b, xc, yf, x2, nw, w_zf, w_gate, w_gate, w_gate, hn, sk, wf, wm, wo, fn)


def _block_rows(ws):
    return jnp.stack(ws).reshape(len(ws), -1, MXU_DIM, QKV_BLOCK).astype(BF16)


def _spread_matrix():
    d = np.arange(QKV_BLOCK)[:, None]
    c = np.arange(MXU_DIM)[None, :]
    return jnp.asarray((c % QKV_BLOCK == d).astype(np.float32), dtype=BF16)


def kernel(x, norm_w, w_in, w_fourier, conv_w, conv_b, w_q, w_k, w_v,
           w_igate_fwd, b_igate_fwd, w_fgate_fwd, b_fgate_fwd,
           w_igate_bwd, b_igate_bwd, w_fgate_bwd, b_fgate_bwd,
           hnorm_w, skip_w, w_mlstm, w_out, final_norm_w):
    bsz, seq, d = x.shape
    depth = w_in.shape[0]
    assert d == D_MODEL and seq % (FFT_N2 * BF16_SUBLANES * 2) == 0 and seq % MCHUNK == 0
    t = bsz * seq
    nc = seq // MCHUNK
    consts = tuple(jnp.asarray(a, dtype=F32).astype(BF16) for a in _fft_constants(seq))
    fn = final_norm_w.reshape(1, d)

    x2 = x.reshape(t, d)
    for l in range(depth):
        nw = norm_w[l].reshape(1, d)

        wg = jnp.concatenate([w_igate_fwd[l], w_fgate_fwd[l], w_igate_bwd[l], w_fgate_bwd[l]], axis=1)
        wg = jnp.pad(wg, ((0, 0), (0, GATE_LANES - wg.shape[1])))
        wg = wg.reshape(3, M_WIDTH // MXU_DIM, MXU_DIM, GATE_LANES)
        qscale = M_HEAD_DIM ** -0.5
        wg = (wg * jnp.asarray([1.0 / qscale, 1.0, 1.0], F32).reshape(3, 1, 1, 1)).astype(BF16)
        bg = jnp.concatenate([b_igate_fwd[l], b_fgate_fwd[l], b_igate_bwd[l], b_fgate_bwd[l]])
        bg = jnp.pad(bg, (0, GATE_LANES - bg.shape[0])).reshape(1, GATE_LANES)
        wrow = _block_rows([w_q[l] * qscale, w_k[l], jnp.swapaxes(w_k[l], -1, -2), w_v[l]])
        xf, xm, w_gate, w_zf, wf_bf, wm_bf, wo_bf = _inproj(
            x2, nw, w_in, w_fourier, w_mlstm, w_out, l)
        q, k, kt, v, xc, gates = _conv_qkv(
            xm, seq, 0.5 * conv_w[l], 0.5 * conv_b[l].reshape(1, M_WIDTH),
            wrow, _spread_matrix(), wg, bg)

        yt = _fourier_mix(xf, bsz, seq, consts)
        yf = jnp.transpose(yt, (0, 2, 1, 3)).reshape(t, F_WIDTH)

        g4 = gates[:, :4 * M_HEADS].reshape(bsz, nc, MCHUNK, 4, M_HEADS)
        g4 = jnp.transpose(g4, (0, 4, 3, 1, 2))
        hf, hb, inv_f, inv_b = _mlstm(q, k, kt, v, g4, bsz, seq)

        x2 = _merge(hf, hb, inv_f, inv_b, xc, yf, x2, nw, w_zf, w_gate,
                    hnorm_w[l].reshape(1, M_WIDTH), skip_w[l].reshape(1, M_WIDTH),
                    wf_bf, wm_bf, wo_bf, fn, final_norm=(l == depth - 1))
    return x2.reshape(bsz, seq, d)
```
